```python
import math
import jax, jax.numpy as jnp
from jax import lax
import numpy as np

D_MODEL = 1024
BATCH = 8
SEQ = 8192
DEPTH = 2

CHUNK = 64
N_META = 16
Q_BLOCK = 128
N_A_LAYERS = DEPTH // 2
N_B_LAYERS = DEPTH - N_A_LAYERS
D_FF = 2816
FFN_RES = 0.5
RMS_EPS = 1e-6
SSM_EXPAND = 2
D_INNER = SSM_EXPAND * D_MODEL
SSM_HEADDIM = 64
SSM_HEADS = D_INNER // SSM_HEADDIM
SSM_GROUPS = 8
SSM_HPG = SSM_HEADS // SSM_GROUPS
D_STATE = 128
D_CONV = 4
CONV_DIM = D_INNER + 2 * SSM_GROUPS * D_STATE
IN_PROJ_DIM = D_INNER + CONV_DIM + SSM_HEADS
SB_HEADS = 16
SB_HEAD_DIM = D_MODEL // SB_HEADS

kernel_name = "yoco_mamba2_stickbreaking_macaron"


def rms_norm(x, g):
    x32 = x.astype(jnp.float32)
    y = x32 * lax.rsqrt(jnp.mean(x32 * x32, axis=-1, keepdims=True) + RMS_EPS)
    return (y * g.astype(jnp.float32)).astype(x.dtype)


def head_rms(t, g):
    t32 = t.astype(jnp.float32)
    return t32 * lax.rsqrt(jnp.mean(t32 * t32, axis=-1, keepdims=True) + RMS_EPS) * g.astype(jnp.float32)


def swiglu(x, w1, w3, w2):
    return (jax.nn.silu(x @ w1) * (x @ w3)) @ w2


def causal_depthwise_conv(u, w, b):
    y = lax.conv_general_dilated(u, w[:, None, :].astype(u.dtype), window_strides=(1,),
                                 padding=[(w.shape[0] - 1, 0)],
                                 dimension_numbers=('NWC', 'WIO', 'NWC'),
                                 feature_group_count=u.shape[-1])
    return y + b.astype(u.dtype)


def ssd_chunk_scan(xs, dt, A, bm, cm):
    b, L = xs.shape[:2]
    nc = L // CHUNK

    def to_chunks(t):
        return jnp.moveaxis(t.reshape((b, nc, CHUNK) + t.shape[2:]), 1, 0)

    causal = jnp.tril(jnp.ones((CHUNK, CHUNK), dtype=bool))

    def step(state, inp):
        xc, dtc, bc, cc = inp
        cum = jnp.cumsum(dtc * A, axis=1)
        seg = cum[:, :, None] - cum[:, None, :]
        decay = jnp.exp(jnp.where(causal[None, :, :, None, None], seg, -jnp.inf))
        cb = jnp.einsum('btgn,bsgn->btsg', cc, bc)
        m = cb[..., None] * decay * dtc[:, None]
        y_diag = jnp.einsum('btsgh,bsghp->btghp', m, xc)
        y_off = jnp.einsum('btgn,bghpn->btghp', cc, state) * jnp.exp(cum)[..., None]
        w_end = jnp.exp(cum[:, -1:] - cum) * dtc
        new_state = (state * jnp.exp(cum[:, -1])[..., None, None]
                     + jnp.einsum('bsgn,bsgh,bsghp->bghpn', bc, w_end, xc))
        return new_state, y_diag + y_off

    state0 = jnp.zeros((b, SSM_GROUPS, SSM_HPG, SSM_HEADDIM, D_STATE), jnp.float32)
    _, ys = lax.scan(step, state0, (to_chunks(xs), to_chunks(dt), to_chunks(bm), to_chunks(cm)))
    return jnp.moveaxis(ys, 0, 1).reshape(xs.shape)


def mamba2_mixer(u, in_proj, conv_w, conv_b, dt_bias, a_log, d_skip, norm_g, out_proj):
    b, L, _ = u.shape
    f32 = jnp.float32
    zxbcdt = u @ in_proj
    z, xbc, dt_raw = jnp.split(zxbcdt, [D_INNER, D_INNER + CONV_DIM], axis=-1)
    xbc = jax.nn.silu(causal_depthwise_conv(xbc, conv_w, conv_b))
    xs, bm, cm = jnp.split(xbc, [D_INNER, D_INNER + SSM_GROUPS * D_STATE], axis=-1)
    dt = jax.nn.softplus(dt_raw.astype(f32) + dt_bias.astype(f32)).reshape(b, L, SSM_GROUPS, SSM_HPG)
    A = -jnp.exp(a_log.astype(f32)).reshape(SSM_GROUPS, SSM_HPG)
    xs = xs.astype(f32).reshape(b, L, SSM_GROUPS, SSM_HPG, SSM_HEADDIM)
    bm = bm.astype(f32).reshape(b, L, SSM_GROUPS, D_STATE)
    cm = cm.astype(f32).reshape(b, L, SSM_GROUPS, D_STATE)
    front = (-N_META) % CHUNK
    back = (-(L + front)) % CHUNK

    def pad(t):
        return jnp.pad(t, [(0, 0), (front, back)] + [(0, 0)] * (t.ndim - 2))

    y = ssd_chunk_scan(pad(xs), pad(dt), A, pad(bm), pad(cm))[:, front:front + L]
    y = y + d_skip.astype(f32).reshape(SSM_GROUPS, SSM_HPG)[..., None] * xs
    y = y.reshape(b, L, D_INNER) * jax.nn.silu(z.astype(f32))
    yg = y.reshape(b, L, SSM_GROUPS, D_INNER // SSM_GROUPS)
    yg = yg * lax.rsqrt(jnp.mean(yg * yg, axis=-1, keepdims=True) + RMS_EPS)
    y = (yg.reshape(b, L, D_INNER) * norm_g.astype(f32)).astype(u.dtype)
    return y @ out_proj


def stick_breaking_attention(q, k, v):
    L = q.shape[1]
    edges = [0] + list(range(N_META, L, Q_BLOCK)) + [L]
    scale = SB_HEAD_DIM ** -0.5
    outs = []
    for q0, q1 in zip(edges[:-1], edges[1:]):
        z = jnp.einsum('bqhd,bkhd->bhqk', q[:, q0:q1], k[:, :q1]) * scale
        strict = jnp.arange(q1)[None, :] < jnp.arange(q0, q1)[:, None]
        log_keep = jnp.where(strict, jax.nn.log_sigmoid(-z), 0.0)
        log_w = jax.nn.log_sigmoid(z) + lax.cumsum(log_keep, axis=3, reverse=True) - log_keep
        w = jnp.where(strict, jnp.exp(log_w), 0.0)
        outs.append(jnp.einsum('bhqk,bkhd->bqhd', w, v[:, :q1]))
    return jnp.concatenate(outs, axis=1)


def stick_breaking_mixer(u, w_q, q_norm_g, k_shared, v_shared, w_o):
    b, L, _ = u.shape
    q = head_rms((u @ w_q).reshape(b, L, SB_HEADS, SB_HEAD_DIM), q_norm_g)
    o = stick_breaking_attention(q, k_shared, v_shared)
    return o.reshape(b, L, D_MODEL).astype(u.dtype) @ w_o


def _fwd_setup_inputs(seed: int = 0) -> dict:
    key = jax.random.key(seed)
    ks = jax.random.split(key, 24)
    f32 = jnp.float32

    def nrm(k, shape, fan_in):
        return jax.random.normal(k, shape, f32) * fan_in ** -0.5

    def gain(k, shape):
        return 1.0 + 0.02 * jax.random.normal(k, shape, f32)

    dt0 = jnp.exp(jax.random.uniform(ks[9], (N_A_LAYERS, SSM_HEADS), f32, math.log(1e-3), math.log(1e-1)))
    dt_bias = dt0 + jnp.log(-jnp.expm1(-dt0))
    return {
        "x": jax.random.normal(ks[0], (BATCH, SEQ, D_MODEL), f32),
        "meta_tokens": jax.random.normal(ks[1], (N_META, D_MODEL), f32),
        "norm_g": gain(ks[2], (DEPTH, 3, D_MODEL)),
        "ffn_w1": nrm(ks[3], (DEPTH, 2, D_MODEL, D_FF), D_MODEL),
        "ffn_w3": nrm(ks[4], (DEPTH, 2, D_MODEL, D_FF), D_MODEL),
        "ffn_w2": nrm(ks[5], (DEPTH, 2, D_FF, D_MODEL), D_FF),
        "ssm_in_proj": nrm(ks[6], (N_A_LAYERS, D_MODEL, IN_PROJ_DIM), D_MODEL),
        "ssm_conv_w": nrm(ks[7], (N_A_LAYERS, D_CONV, CONV_DIM), D_CONV),
        "ssm_conv_b": 0.02 * jax.random.normal(ks[8], (N_A_LAYERS, CONV_DIM), f32),
        "ssm_dt_bias": dt_bias,
        "ssm_a_log": jnp.log(jax.random.uniform(ks[10], (N_A_LAYERS, SSM_HEADS), f32, 1.0, 16.0)),
        "ssm_d": 1.0 + 0.1 * jax.random.normal(ks[11], (N_A_LAYERS, SSM_HEADS), f32),
        "ssm_norm_g": gain(ks[12], (N_A_LAYERS, D_INNER)),
        "ssm_out_proj": nrm(ks[13], (N_A_LAYERS, D_INNER, D_MODEL), D_INNER),
        "kv_norm_g": gain(ks[14], (D_MODEL,)),
        "w_k": nrm(ks[15], (D_MODEL, D_MODEL), D_MODEL),
        "k_norm_g": gain(ks[16], (SB_HEAD_DIM,)),
        "w_v": nrm(ks[17], (D_MODEL, D_MODEL), D_MODEL),
        "sb_w_q": nrm(ks[18], (N_B_LAYERS, D_MODEL, D_MODEL), D_MODEL),
        "sb_q_norm_g": gain(ks[19], (N_B_LAYERS, SB_HEAD_DIM)),
        "sb_w_o": nrm(ks[20], (N_B_LAYERS, D_MODEL, D_MODEL), D_MODEL),
    }


def _fwd_reference(x, meta_tokens, norm_g, ffn_w1, ffn_w3, ffn_w2, ssm_in_proj, ssm_conv_w, ssm_conv_b,
              ssm_dt_bias, ssm_a_log, ssm_d, ssm_norm_g, ssm_out_proj, kv_norm_g, w_k, k_norm_g, w_v,
              sb_w_q, sb_q_norm_g, sb_w_o):
    b = x.shape[0]
    meta = jnp.broadcast_to(meta_tokens[None].astype(x.dtype), (b, N_META, D_MODEL))
    h = jnp.concatenate([meta, x], axis=1)
    L = h.shape[1]
    k_shared = None
    v_shared = None
    for i in range(DEPTH):
        h = h + FFN_RES * swiglu(rms_norm(h, norm_g[i, 0]), ffn_w1[i, 0], ffn_w3[i, 0], ffn_w2[i, 0])
        u = rms_norm(h, norm_g[i, 1])
        if i < N_A_LAYERS:
            h = h + mamba2_mixer(u, ssm_in_proj[i], ssm_conv_w[i], ssm_conv_b[i], ssm_dt_bias[i],
                                 ssm_a_log[i], ssm_d[i], ssm_norm_g[i], ssm_out_proj[i])
        else:
            j = i - N_A_LAYERS
            h = h + stick_breaking_mixer(u, sb_w_q[j], sb_q_norm_g[j], k_shared, v_shared, sb_w_o[j])
        h = h + FFN_RES * swiglu(rms_norm(h, norm_g[i, 2]), ffn_w1[i, 1], ffn_w3[i, 1], ffn_w2[i, 1])
        if i == N_A_LAYERS - 1:
            kv_in = rms_norm(h, kv_norm_g)
            k_shared = head_rms((kv_in @ w_k).reshape(b, L, SB_HEADS, SB_HEAD_DIM), k_norm_g)
            v_shared = (kv_in @ w_v).reshape(b, L, SB_HEADS, SB_HEAD_DIM).astype(jnp.float32)
    return h[:, N_META:]


import jax as _jax
import jax.numpy as _jnp

TWIN_FORMAT = 'train_step'
FWD_PARAMS = ['x', 'meta_tokens', 'norm_g', 'ffn_w1', 'ffn_w3', 'ffn_w2', 'ssm_in_proj', 'ssm_conv_w', 'ssm_conv_b', 'ssm_dt_bias', 'ssm_a_log', 'ssm_d', 'ssm_norm_g', 'ssm_out_proj', 'kv_norm_g', 'w_k', 'k_norm_g', 'w_v', 'sb_w_q', 'sb_q_norm_g', 'sb_w_o']
TWIN_WEIGHTS = ['meta_tokens', 'norm_g', 'ffn_w1', 'ffn_w3', 'ffn_w2', 'ssm_in_proj', 'ssm_conv_w', 'ssm_conv_b', 'ssm_dt_bias', 'ssm_a_log', 'ssm_d', 'ssm_norm_g', 'ssm_out_proj', 'kv_norm_g', 'w_k', 'k_norm_g', 'w_v', 'sb_w_q', 'sb_q_norm_g', 'sb_w_o']
TWIN_DIFF_INPUT = 'x'
TWIN_INPUTS = ['x', 'meta_tokens', 'norm_g', 'ffn_w1', 'ffn_w3', 'ffn_w2', 'ssm_in_proj', 'ssm_conv_w', 'ssm_conv_b', 'ssm_dt_bias', 'ssm_a_log', 'ssm_d', 'ssm_norm_g', 'ssm_out_proj', 'kv_norm_g', 'w_k', 'k_norm_g', 'w_v', 'sb_w_q', 'sb_q_norm_g', 'sb_w_o', 'loss_target', 'm_meta_tokens', 'm_norm_g', 'm_ffn_w1', 'm_ffn_w3', 'm_ffn_w2', 'm_ssm_in_proj', 'm_ssm_conv_w', 'm_ssm_conv_b', 'm_ssm_dt_bias', 'm_ssm_a_log', 'm_ssm_d', 'm_ssm_norm_g', 'm_ssm_out_proj', 'm_kv_norm_g', 'm_w_k', 'm_k_norm_g', 'm_w_v', 'm_sb_w_q', 'm_sb_q_norm_g', 'm_sb_w_o', 'v_meta_tokens', 'v_norm_g', 'v_ffn_w1', 'v_ffn_w3', 'v_ffn_w2', 'v_ssm_in_proj', 'v_ssm_conv_w', 'v_ssm_conv_b', 'v_ssm_dt_bias', 'v_ssm_a_log', 'v_ssm_d', 'v_ssm_norm_g', 'v_ssm_out_proj', 'v_kv_norm_g', 'v_w_k', 'v_k_norm_g', 'v_w_v', 'v_sb_w_q', 'v_sb_q_norm_g', 'v_sb_w_o']
TWIN_OUTPUTS = ['loss', 'grad_x', 'grad_meta_tokens', 'grad_norm_g', 'grad_ffn_w1', 'grad_ffn_w3', 'grad_ffn_w2', 'grad_ssm_in_proj', 'grad_ssm_conv_w', 'grad_ssm_conv_b', 'grad_ssm_dt_bias', 'grad_ssm_a_log', 'grad_ssm_d', 'grad_ssm_norm_g', 'grad_ssm_out_proj', 'grad_kv_norm_g', 'grad_w_k', 'grad_k_norm_g', 'grad_w_v', 'grad_sb_w_q', 'grad_sb_q_norm_g', 'grad_sb_w_o', 'delta_meta_tokens', 'delta_norm_g', 'delta_ffn_w1', 'delta_ffn_w3', 'delta_ffn_w2', 'delta_ssm_in_proj', 'delta_ssm_conv_w', 'delta_ssm_conv_b', 'delta_ssm_dt_bias', 'delta_ssm_a_log', 'delta_ssm_d', 'delta_ssm_norm_g', 'delta_ssm_out_proj', 'delta_kv_norm_g', 'delta_w_k', 'delta_k_norm_g', 'delta_w_v', 'delta_sb_w_q', 'delta_sb_q_norm_g', 'delta_sb_w_o', 'new_m_meta_tokens', 'new_m_norm_g', 'new_m_ffn_w1', 'new_m_ffn_w3', 'new_m_ffn_w2', 'new_m_ssm_in_proj', 'new_m_ssm_conv_w', 'new_m_ssm_conv_b', 'new_m_ssm_dt_bias', 'new_m_ssm_a_log', 'new_m_ssm_d', 'new_m_ssm_norm_g', 'new_m_ssm_out_proj', 'new_m_kv_norm_g', 'new_m_w_k', 'new_m_k_norm_g', 'new_m_w_v', 'new_m_sb_w_q', 'new_m_sb_q_norm_g', 'new_m_sb_w_o', 'new_v_meta_tokens', 'new_v_norm_g', 'new_v_ffn_w1', 'new_v_ffn_w3', 'new_v_ffn_w2', 'new_v_ssm_in_proj', 'new_v_ssm_conv_w', 'new_v_ssm_conv_b', 'new_v_ssm_dt_bias', 'new_v_ssm_a_log', 'new_v_ssm_d', 'new_v_ssm_norm_g', 'new_v_ssm_out_proj', 'new_v_kv_norm_g', 'new_v_w_k', 'new_v_k_norm_g', 'new_v_w_v', 'new_v_sb_w_q', 'new_v_sb_q_norm_g', 'new_v_sb_w_o']
TWIN_LEAF_KINDS = {'loss': 'loss', 'grad_x': 'grad_x', 'grad_meta_tokens': 'grad_w', 'grad_norm_g': 'grad_w', 'grad_ffn_w1': 'grad_w', 'grad_ffn_w3': 'grad_w', 'grad_ffn_w2': 'grad_w', 'grad_ssm_in_proj': 'grad_w', 'grad_ssm_conv_w': 'grad_w', 'grad_ssm_conv_b': 'grad_w', 'grad_ssm_dt_bias': 'grad_w', 'grad_ssm_a_log': 'grad_w', 'grad_ssm_d': 'grad_w', 'grad_ssm_norm_g': 'grad_w', 'grad_ssm_out_proj': 'grad_w', 'grad_kv_norm_g': 'grad_w', 'grad_w_k': 'grad_w', 'grad_k_norm_g': 'grad_w', 'grad_w_v': 'grad_w', 'grad_sb_w_q': 'grad_w', 'grad_sb_q_norm_g': 'grad_w', 'grad_sb_w_o': 'grad_w', 'delta_meta_tokens': 'delta_w', 'delta_norm_g': 'delta_w', 'delta_ffn_w1': 'delta_w', 'delta_ffn_w3': 'delta_w', 'delta_ffn_w2': 'delta_w', 'delta_ssm_in_proj': 'delta_w', 'delta_ssm_conv_w': 'delta_w', 'delta_ssm_conv_b': 'delta_w', 'delta_ssm_dt_bias': 'delta_w', 'delta_ssm_a_log': 'delta_w', 'delta_ssm_d': 'delta_w', 'delta_ssm_norm_g': 'delta_w', 'delta_ssm_out_proj': 'delta_w', 'delta_kv_norm_g': 'delta_w', 'delta_w_k': 'delta_w', 'delta_k_norm_g': 'delta_w', 'delta_w_v': 'delta_w', 'delta_sb_w_q': 'delta_w', 'delta_sb_q_norm_g': 'delta_w', 'delta_sb_w_o': 'delta_w', 'new_m_meta_tokens': 'new_m', 'new_m_norm_g': 'new_m', 'new_m_ffn_w1': 'new_m', 'new_m_ffn_w3': 'new_m', 'new_m_ffn_w2': 'new_m', 'new_m_ssm_in_proj': 'new_m', 'new_m_ssm_conv_w': 'new_m', 'new_m_ssm_conv_b': 'new_m', 'new_m_ssm_dt_bias': 'new_m', 'new_m_ssm_a_log': 'new_m', 'new_m_ssm_d': 'new_m', 'new_m_ssm_norm_g': 'new_m', 'new_m_ssm_out_proj': 'new_m', 'new_m_kv_norm_g': 'new_m', 'new_m_w_k': 'new_m', 'new_m_k_norm_g': 'new_m', 'new_m_w_v': 'new_m', 'new_m_sb_w_q': 'new_m', 'new_m_sb_q_norm_g': 'new_m', 'new_m_sb_w_o': 'new_m', 'new_v_meta_tokens': 'new_v', 'new_v_norm_g': 'new_v', 'new_v_ffn_w1': 'new_v', 'new_v_ffn_w3': 'new_v', 'new_v_ffn_w2': 'new_v', 'new_v_ssm_in_proj': 'new_v', 'new_v_ssm_conv_w': 'new_v', 'new_v_ssm_conv_b': 'new_v', 'new_v_ssm_dt_bias': 'new_v', 'new_v_ssm_a_log': 'new_v', 'new_v_ssm_d': 'new_v', 'new_v_ssm_norm_g': 'new_v', 'new_v_ssm_out_proj': 'new_v', 'new_v_kv_norm_g': 'new_v', 'new_v_w_k': 'new_v', 'new_v_k_norm_g': 'new_v', 'new_v_w_v': 'new_v', 'new_v_sb_w_q': 'new_v', 'new_v_sb_q_norm_g': 'new_v', 'new_v_sb_w_o': 'new_v'}


def _forward(args):
    return _fwd_reference(*[args[k] for k in FWD_PARAMS])


def _output_shape():
    out = _jax.eval_shape(lambda: _forward(_fwd_setup_inputs(0)))
    return out.shape, out.dtype

N_MICROBATCH = 1
ADAM_LR = 0.001
ADAM_B1 = 0.9
ADAM_B2 = 0.999
ADAM_EPS = 1e-08
ADAM_WD = 0.01
ADAM_STEP = 10
PER_EXAMPLE_BATCH_AXIS = {'x': 0, 'loss_target': 0}
SHARED_INPUTS = []
_WEIGHT_DTYPES = {'meta_tokens': _jnp.float32, 'norm_g': _jnp.float32, 'ffn_w1': _jnp.float32, 'ffn_w3': _jnp.float32, 'ffn_w2': _jnp.float32, 'ssm_in_proj': _jnp.float32, 'ssm_conv_w': _jnp.float32, 'ssm_conv_b': _jnp.float32, 'ssm_dt_bias': _jnp.float32, 'ssm_a_log': _jnp.float32, 'ssm_d': _jnp.float32, 'ssm_norm_g': _jnp.float32, 'ssm_out_proj': _jnp.float32, 'kv_norm_g': _jnp.float32, 'w_k': _jnp.float32, 'k_norm_g': _jnp.float32, 'w_v': _jnp.float32, 'sb_w_q': _jnp.float32, 'sb_q_norm_g': _jnp.float32, 'sb_w_o': _jnp.float32}
MOMENT_SCALE = {'meta_tokens': 2.498106e-02, 'norm_g': 9.311497e+00, 'ffn_w1': 1.760345e-01, 'ffn_w3': 1.841576e-01, 'ffn_w2': 3.037675e-01, 'ssm_in_proj': 4.448264e-01, 'ssm_conv_w': 9.505396e-01, 'ssm_conv_b': 4.158009e+00, 'ssm_dt_bias': 1.045672e+00, 'ssm_a_log': 1.185878e+01, 'ssm_d': 1.414129e+01, 'ssm_norm_g': 4.283741e+01, 'ssm_out_proj': 3.372282e+00, 'kv_norm_g': 2.740471e+01, 'w_k': 4.008048e-01, 'k_norm_g': 6.217735e+01, 'w_v': 1.752183e+00, 'sb_w_q': 4.052054e-01, 'sb_q_norm_g': 6.222802e+01, 'sb_w_o': 1.552026e+00}


def _to_microbatches(a, axis):
    t = _jnp.moveaxis(a, axis, 0)
    t = t.reshape((N_MICROBATCH, t.shape[0] // N_MICROBATCH) + t.shape[1:])
    return _jnp.moveaxis(t, 1, axis + 1)


def setup_inputs(seed: int = 0) -> dict:
    inp = _fwd_setup_inputs(seed)
    key = _jax.random.fold_in(_jax.random.key(seed), 7919)
    shape, _ = _output_shape()
    out = dict(inp)
    out["loss_target"] = _jax.random.normal(_jax.random.fold_in(key, 0), shape, _jnp.float32)
    for i, name in enumerate(TWIN_WEIGHTS):
        w = inp[name].astype(_jnp.float32)
        if MOMENT_SCALE is None:
            s = _jnp.sqrt(_jnp.mean(_jnp.square(w)) + 1e-30)
        else:
            s = MOMENT_SCALE[name]
        km, kv = _jax.random.split(_jax.random.fold_in(key, i + 1))
        out[name] = w
        out["m_" + name] = s * _jax.random.normal(km, w.shape, _jnp.float32)
        out["v_" + name] = (s * s) * _jax.random.uniform(kv, w.shape, _jnp.float32, 0.5, 1.5)
    if N_MICROBATCH > 1:
        for name, axis in PER_EXAMPLE_BATCH_AXIS.items():
            out[name] = _to_microbatches(out[name], axis)
    return {'x': out['x'], 'meta_tokens': out['meta_tokens'], 'norm_g': out['norm_g'], 'ffn_w1': out['ffn_w1'], 'ffn_w3': out['ffn_w3'], 'ffn_w2': out['ffn_w2'], 'ssm_in_proj': out['ssm_in_proj'], 'ssm_conv_w': out['ssm_conv_w'], 'ssm_conv_b': out['ssm_conv_b'], 'ssm_dt_bias': out['ssm_dt_bias'], 'ssm_a_log': out['ssm_a_log'], 'ssm_d': out['ssm_d'], 'ssm_norm_g': out['ssm_norm_g'], 'ssm_out_proj': out['ssm_out_proj'], 'kv_norm_g': out['kv_norm_g'], 'w_k': out['w_k'], 'k_norm_g': out['k_norm_g'], 'w_v': out['w_v'], 'sb_w_q': out['sb_w_q'], 'sb_q_norm_g': out['sb_q_norm_g'], 'sb_w_o': out['sb_w_o'], 'loss_target': out['loss_target'], 'm_meta_tokens': out['m_meta_tokens'], 'm_norm_g': out['m_norm_g'], 'm_ffn_w1': out['m_ffn_w1'], 'm_ffn_w3': out['m_ffn_w3'], 'm_ffn_w2': out['m_ffn_w2'], 'm_ssm_in_proj': out['m_ssm_in_proj'], 'm_ssm_conv_w': out['m_ssm_conv_w'], 'm_ssm_conv_b': out['m_ssm_conv_b'], 'm_ssm_dt_bias': out['m_ssm_dt_bias'], 'm_ssm_a_log': out['m_ssm_a_log'], 'm_ssm_d': out['m_ssm_d'], 'm_ssm_norm_g': out['m_ssm_norm_g'], 'm_ssm_out_proj': out['m_ssm_out_proj'], 'm_kv_norm_g': out['m_kv_norm_g'], 'm_w_k': out['m_w_k'], 'm_k_norm_g': out['m_k_norm_g'], 'm_w_v': out['m_w_v'], 'm_sb_w_q': out['m_sb_w_q'], 'm_sb_q_norm_g': out['m_sb_q_norm_g'], 'm_sb_w_o': out['m_sb_w_o'], 'v_meta_tokens': out['v_meta_tokens'], 'v_norm_g': out['v_norm_g'], 'v_ffn_w1': out['v_ffn_w1'], 'v_ffn_w3': out['v_ffn_w3'], 'v_ffn_w2': out['v_ffn_w2'], 'v_ssm_in_proj': out['v_ssm_in_proj'], 'v_ssm_conv_w': out['v_ssm_conv_w'], 'v_ssm_conv_b': out['v_ssm_conv_b'], 'v_ssm_dt_bias': out['v_ssm_dt_bias'], 'v_ssm_a_log': out['v_ssm_a_log'], 'v_ssm_d': out['v_ssm_d'], 'v_ssm_norm_g': out['v_ssm_norm_g'], 'v_ssm_out_proj': out['v_ssm_out_proj'], 'v_kv_norm_g': out['v_kv_norm_g'], 'v_w_k': out['v_w_k'], 'v_k_norm_g': out['v_k_norm_g'], 'v_w_v': out['v_w_v'], 'v_sb_w_q': out['v_sb_w_q'], 'v_sb_q_norm_g': out['v_sb_q_norm_g'], 'v_sb_w_o': out['v_sb_w_o']}


def _loss(weights, diff, rest, loss_target):
    with _jax.named_scope("forward"):
        args = {**rest, TWIN_DIFF_INPUT: diff, **{k: w.astype(_WEIGHT_DTYPES[k]) for k, w in weights.items()}}
        y = _forward(args)
    with _jax.named_scope("loss_head"):
        err = _jnp.square(y.astype(_jnp.float32) - loss_target)
        return 0.5 * _jnp.sum(_jnp.mean(err, axis=-1)) if err.ndim else 0.5 * err


def _adamw(w, g, m, v):
    m = ADAM_B1 * m + (1.0 - ADAM_B1) * g
    v = ADAM_B2 * v + (1.0 - ADAM_B2) * _jnp.square(g)
    m_hat = m / (1.0 - ADAM_B1 ** ADAM_STEP)
    v_hat = v / (1.0 - ADAM_B2 ** ADAM_STEP)
    delta = -ADAM_LR * (m_hat / (_jnp.sqrt(v_hat) + ADAM_EPS) + ADAM_WD * w)
    return delta, m, v


def reference(x, meta_tokens, norm_g, ffn_w1, ffn_w3, ffn_w2, ssm_in_proj, ssm_conv_w, ssm_conv_b, ssm_dt_bias, ssm_a_log, ssm_d, ssm_norm_g, ssm_out_proj, kv_norm_g, w_k, k_norm_g, w_v, sb_w_q, sb_q_norm_g, sb_w_o, loss_target, m_meta_tokens, m_norm_g, m_ffn_w1, m_ffn_w3, m_ffn_w2, m_ssm_in_proj, m_ssm_conv_w, m_ssm_conv_b, m_ssm_dt_bias, m_ssm_a_log, m_ssm_d, m_ssm_norm_g, m_ssm_out_proj, m_kv_norm_g, m_w_k, m_k_norm_g, m_w_v, m_sb_w_q, m_sb_q_norm_g, m_sb_w_o, v_meta_tokens, v_norm_g, v_ffn_w1, v_ffn_w3, v_ffn_w2, v_ssm_in_proj, v_ssm_conv_w, v_ssm_conv_b, v_ssm_dt_bias, v_ssm_a_log, v_ssm_d, v_ssm_norm_g, v_ssm_out_proj, v_kv_norm_g, v_w_k, v_k_norm_g, v_w_v, v_sb_w_q, v_sb_q_norm_g, v_sb_w_o):
    given = dict(x=x, meta_tokens=meta_tokens, norm_g=norm_g, ffn_w1=ffn_w1, ffn_w3=ffn_w3, ffn_w2=ffn_w2, ssm_in_proj=ssm_in_proj, ssm_conv_w=ssm_conv_w, ssm_conv_b=ssm_conv_b, ssm_dt_bias=ssm_dt_bias, ssm_a_log=ssm_a_log, ssm_d=ssm_d, ssm_norm_g=ssm_norm_g, ssm_out_proj=ssm_out_proj, kv_norm_g=kv_norm_g, w_k=w_k, k_norm_g=k_norm_g, w_v=w_v, sb_w_q=sb_w_q, sb_q_norm_g=sb_q_norm_g, sb_w_o=sb_w_o, loss_target=loss_target, m_meta_tokens=m_meta_tokens, m_norm_g=m_norm_g, m_ffn_w1=m_ffn_w1, m_ffn_w3=m_ffn_w3, m_ffn_w2=m_ffn_w2, m_ssm_in_proj=m_ssm_in_proj, m_ssm_conv_w=m_ssm_conv_w, m_ssm_conv_b=m_ssm_conv_b, m_ssm_dt_bias=m_ssm_dt_bias, m_ssm_a_log=m_ssm_a_log, m_ssm_d=m_ssm_d, m_ssm_norm_g=m_ssm_norm_g, m_ssm_out_proj=m_ssm_out_proj, m_kv_norm_g=m_kv_norm_g, m_w_k=m_w_k, m_k_norm_g=m_k_norm_g, m_w_v=m_w_v, m_sb_w_q=m_sb_w_q, m_sb_q_norm_g=m_sb_q_norm_g, m_sb_w_o=m_sb_w_o, v_meta_tokens=v_meta_tokens, v_norm_g=v_norm_g, v_ffn_w1=v_ffn_w1, v_ffn_w3=v_ffn_w3, v_ffn_w2=v_ffn_w2, v_ssm_in_proj=v_ssm_in_proj, v_ssm_conv_w=v_ssm_conv_w, v_ssm_conv_b=v_ssm_conv_b, v_ssm_dt_bias=v_ssm_dt_bias, v_ssm_a_log=v_ssm_a_log, v_ssm_d=v_ssm_d, v_ssm_norm_g=v_ssm_norm_g, v_ssm_out_proj=v_ssm_out_proj, v_kv_norm_g=v_kv_norm_g, v_w_k=v_w_k, v_k_norm_g=v_k_norm_g, v_w_v=v_w_v, v_sb_w_q=v_sb_w_q, v_sb_q_norm_g=v_sb_q_norm_g, v_sb_w_o=v_sb_w_o)
    weights = {n: given[n] for n in TWIN_WEIGHTS}
    shared = {n: given[n] for n in SHARED_INPUTS}
    per_example = {n: given[n] for n in ['x']}
    grad_fn = _jax.value_and_grad(_loss, argnums=(0, 1))

    def one_microbatch(ex, loss_target):
        ex = dict(ex)
        diff = ex.pop(TWIN_DIFF_INPUT)
        return grad_fn(weights, diff, {**shared, **ex}, loss_target)

    if N_MICROBATCH == 1:
        loss, (grad_w, grad_x) = one_microbatch(per_example, given["loss_target"])
    else:
        def body(carry, xs):
            loss_sum, grad_sum = carry
            l_k, (gw_k, gx_k) = one_microbatch(xs[0], xs[1])
            with _jax.named_scope("update"):
                return (loss_sum + l_k, _jax.tree.map(_jnp.add, grad_sum, gw_k)), gx_k

        init = (_jnp.zeros((), _jnp.float32), _jax.tree.map(_jnp.zeros_like, weights))
        (loss, grad_w), grad_x = _jax.lax.scan(body, init, (per_example, given["loss_target"]))
    with _jax.named_scope("update"):
        delta_w, new_m, new_v = {}, {}, {}
        for n in TWIN_WEIGHTS:
            delta_w[n], new_m[n], new_v[n] = _adamw(weights[n], grad_w[n], given["m_" + n], given["v_" + n])
    return (loss, grad_x, *[grad_w[n] for n in TWIN_WEIGHTS], *[delta_w[n] for n in TWIN_WEIGHTS],
            *[new_m[n] for n in TWIN_WEIGHTS], *[new_v[n] for n in TWIN_WEIGHTS])
```

```python
import functools
import math

import jax
import jax.numpy as jnp
from jax import lax
from jax.experimental import pallas as pl
from jax.experimental.pallas import tpu as pltpu

F32, BF16 = jnp.float32, jnp.bfloat16
RMS_EPS = 1e-6
LANES = 128
HEAD = 64
D_STATE = 128
BLK = 128
FFN_RES = 0.5
VMEM_LIMIT = 56 * 2 ** 20
ADAM_LR, ADAM_B1, ADAM_B2, ADAM_EPS, ADAM_WD, ADAM_STEP = 0.001, 0.9, 0.999, 1e-08, 0.01, 10
MESH = pl.DeviceIdType.MESH

NN = (((1,), (0,)), ((), ()))
NT = (((1,), (1,)), ((), ()))
TN = (((0,), (0,)), ((), ()))


def _dot(a, b, dn=NN):
    return lax.dot_general(a, b, dn, preferred_element_type=F32)


def _split(x, parts):
    out = []
    for _ in range(parts):
        p = x.astype(BF16)
        out.append(p)
        x = x - p.astype(F32)
    return out


def _dotx(a, b, dn=NN, parts=3, split="a"):
    if split == "a":
        return sum(_dot(p, b, dn) for p in _split(a, parts))
    return sum(_dot(a, p, dn) for p in _split(b, parts))


def _tile(n, target, mult):
    best = None
    for d in range(mult, min(n, target) + 1, mult):
        if n % d == 0:
            best = d
    return n if best is None else best


def _params(*sem):
    return pltpu.CompilerParams(dimension_semantics=tuple(sem) if sem else None, vmem_limit_bytes=VMEM_LIMIT)


def _iota(shape, axis):
    return lax.broadcasted_iota(jnp.int32, shape, axis)


def _sigmoid(x):
    return 1.0 / (1.0 + jnp.exp(-x))


def _matmul(name, pairs, mode, tm, tn, tk, out_dtypes, epilogue=None, extras=(), separate=False):
    a0, b0 = pairs[0]
    if mode == "nn":
        (M, K), N = a0.shape, b0.shape[1]
    elif mode == "nt":
        (M, K), N = a0.shape, b0.shape[0]
    else:
        (K, M), N = a0.shape, b0.shape[1]
    assert M % tm == 0 and N % tn == 0 and K % tk == 0, (name, M, N, K, tm, tn, tk)
    nM, nN, nK = M // tm, N // tn, K // tk
    np_, ne, no = len(pairs), len(extras), len(out_dtypes)
    n_acc = np_ if separate else 1
    dn = {"nn": NN, "nt": NT, "tn": TN}[mode]

    def body(*refs):
        ab, ex = refs[:2 * np_], refs[2 * np_:2 * np_ + ne]
        outs, accs = refs[2 * np_ + ne:2 * np_ + ne + no], refs[2 * np_ + ne + no:]
        k = pl.program_id(2)

        def prod(i):
            return _dot(ab[2 * i][...].astype(BF16), ab[2 * i + 1][...].astype(BF16), dn)

        ps = [prod(i) for i in range(np_)]
        if not separate:
            ps = [functools.reduce(lambda u, v: u + v, ps)]

        def finish(vals):
            res = epilogue(vals, [e[...] for e in ex]) if epilogue is not None else vals
            for o, r in zip(outs, res):
                o[...] = r.astype(o.dtype)

        if nK == 1:
            finish(ps)
        else:
            @pl.when(k == 0)
            def _():
                for acc, p in zip(accs, ps):
                    acc[...] = p

            @pl.when(k > 0)
            def _():
                for acc, p in zip(accs, ps):
                    acc[...] += p

            @pl.when(k == nK - 1)
            def _():
                finish([acc[...] for acc in accs])

    if mode == "tn":
        a_spec = pl.BlockSpec((tk, tm), lambda n, m, k: (k, m))
    else:
        a_spec = pl.BlockSpec((tm, tk), lambda n, m, k: (m, k))
    if mode == "nt":
        b_spec = pl.BlockSpec((tn, tk), lambda n, m, k: (n, k))
    else:
        b_spec = pl.BlockSpec((tk, tn), lambda n, m, k: (k, n))
    in_specs, args = [], []
    for a, b in pairs:
        in_specs += [a_spec, b_spec]
        args += [a, b]
    for arr, kind in extras:
        if kind == "mn":
            in_specs.append(pl.BlockSpec((tm, tn), lambda n, m, k: (m, n)))
        else:
            in_specs.append(pl.BlockSpec((1, tn), lambda n, m, k: (0, n)))
        args.append(arr)
    out_specs = [pl.BlockSpec((tm, tn), lambda n, m, k: (m, n)) for _ in out_dtypes]
    res = pl.pallas_call(
        body, name=name, grid=(nN, nM, nK), in_specs=in_specs, out_specs=out_specs,
        out_shape=[jax.ShapeDtypeStruct((M, N), d) for d in out_dtypes],
        scratch_shapes=[pltpu.VMEM((tm, tn), F32) for _ in range(n_acc)] if nK > 1 else [],
        compiler_params=_params("parallel", "parallel", "arbitrary"),
    )(*args)
    return res


def _rms_fwd(name, h, g):
    L, D = h.shape
    tr = _tile(L, 1024, 16)

    def body(h_ref, g_ref, o_ref):
        x = h_ref[...]
        r = lax.rsqrt(jnp.mean(x * x, axis=-1, keepdims=True) + RMS_EPS)
        o_ref[...] = (x * r * g_ref[...]).astype(BF16)

    return pl.pallas_call(
        body, name=name, grid=(L // tr,),
        in_specs=[pl.BlockSpec((tr, D), lambda i: (i, 0)), pl.BlockSpec((1, D), lambda i: (0, 0))],
        out_specs=pl.BlockSpec((tr, D), lambda i: (i, 0)),
        out_shape=jax.ShapeDtypeStruct((L, D), BF16), compiler_params=_params("parallel"),
    )(h, g.reshape(1, D))


def _rms_bwd(name, dxn, h, g, dres):
    L, D = h.shape
    tr = _tile(L, 512, 8)

    def body(dxn_ref, h_ref, g_ref, dres_ref, dh_ref, dg_ref):
        x = h_ref[...]
        r = lax.rsqrt(jnp.mean(x * x, axis=-1, keepdims=True) + RMS_EPS)
        xh = x * r
        dxn = dxn_ref[...]
        dxh = dxn * g_ref[...]
        dh_ref[...] = dres_ref[...] + r * (dxh - xh * jnp.mean(dxh * xh, axis=-1, keepdims=True))

        @pl.when(pl.program_id(0) == 0)
        def _():
            dg_ref[...] = jnp.zeros_like(dg_ref)

        dg_ref[...] += jnp.sum(dxn * xh, axis=0, keepdims=True)

    row = pl.BlockSpec((tr, D), lambda i: (i, 0))
    vec = pl.BlockSpec((1, D), lambda i: (0, 0))
    return pl.pallas_call(
        body, name=name, grid=(L // tr,), in_specs=[row, row, vec, row], out_specs=[row, vec],
        out_shape=[jax.ShapeDtypeStruct((L, D), F32), jax.ShapeDtypeStruct((1, D), F32)],
        compiler_params=_params("arbitrary"),
    )(dxn, h, g.reshape(1, D), dres)


def _head_sums(x2):
    blockdiag = (_iota((LANES, LANES), 0) // HEAD == _iota((LANES, LANES), 1) // HEAD).astype(BF16)
    cols = [_dotx(x2[:, j:j + LANES], blockdiag) for j in range(0, x2.shape[1], LANES)]
    return jnp.concatenate(cols, axis=1) if len(cols) > 1 else cols[0]


def _headrms_fwd(name, raw, g):
    L, D = raw.shape
    tr = _tile(L, 512, 16)

    def body(x_ref, g_ref, o_ref):
        x = x_ref[...]
        r = lax.rsqrt(_head_sums(x * x) * (1.0 / HEAD) + RMS_EPS)
        o_ref[...] = (x * r * g_ref[...]).astype(BF16)

    return pl.pallas_call(
        body, name=name, grid=(L // tr,),
        in_specs=[pl.BlockSpec((tr, D), lambda i: (i, 0)), pl.BlockSpec((1, D), lambda i: (0, 0))],
        out_specs=pl.BlockSpec((tr, D), lambda i: (i, 0)),
        out_shape=jax.ShapeDtypeStruct((L, D), BF16), compiler_params=_params("parallel"),
    )(raw, g)


def _headrms_bwd(name, dy, raw, g):
    L, D = raw.shape
    tr = _tile(L, 512, 16)

    def body(dy_ref, x_ref, g_ref, dx_ref, dg_ref):
        x = x_ref[...]
        dy = dy_ref[...]
        r = lax.rsqrt(_head_sums(x * x) * (1.0 / HEAD) + RMS_EPS)
        xh = x * r
        dxh = dy * g_ref[...]
        dx_ref[...] = (r * (dxh - xh * (_head_sums(dxh * xh) * (1.0 / HEAD)))).astype(BF16)

        @pl.when(pl.program_id(0) == 0)
        def _():
            dg_ref[...] = jnp.zeros_like(dg_ref)

        dg_ref[...] += jnp.sum(dy * xh, axis=0, keepdims=True)

    row = pl.BlockSpec((tr, D), lambda i: (i, 0))
    vec = pl.BlockSpec((1, D), lambda i: (0, 0))
    return pl.pallas_call(
        body, name=name, grid=(L // tr,), in_specs=[row, row, vec], out_specs=[row, vec],
        out_shape=[jax.ShapeDtypeStruct((L, D), BF16), jax.ShapeDtypeStruct((1, D), F32)],
        compiler_params=_params("arbitrary"),
    )(dy, raw, g)


def _loss(name, h, tgt, pad_rows):
    L, D = h.shape
    nb = L // BLK
    assert pad_rows == BLK

    def body(h_ref, t_ref, dh_ref, s_ref):
        i = pl.program_id(0)

        @pl.when(i == 0)
        def _():
            s_ref[...] = jnp.zeros_like(s_ref)
            dh_ref[...] = jnp.zeros_like(dh_ref)

        @pl.when(i > 0)
        def _():
            e = h_ref[...] - t_ref[...]
            dh_ref[...] = e * (1.0 / D)
            s_ref[...] += jnp.sum(e * e, axis=0, keepdims=True)

    return pl.pallas_call(
        body, name=name, grid=(nb,),
        in_specs=[pl.BlockSpec((BLK, D), lambda i: (i, 0)), pl.BlockSpec((BLK, D), lambda i: (jnp.maximum(i - 1, 0), 0))],
        out_specs=[pl.BlockSpec((BLK, D), lambda i: (i, 0)), pl.BlockSpec((1, D), lambda i: (0, 0))],
        out_shape=[jax.ShapeDtypeStruct((L, D), F32), jax.ShapeDtypeStruct((1, D), F32)],
        compiler_params=_params("arbitrary"),
    )(h, tgt)


def _swiglu_up(name, xn, w1, w3):
    L, D = xn.shape
    Fd = w1.shape[1]
    tm, tn = _tile(L, 704, 16), _tile(Fd, 1408, LANES)

    def epi(accs, _):
        a, b = accs
        return [a, b, a * _sigmoid(a) * b]

    return _matmul(name, [(xn, w1), (xn, w3)], "nn", tm, tn, D, [BF16, BF16, BF16], epilogue=epi, separate=True)


def _swiglu_bwd(name, dh, w2, a, b):
    L, D = dh.shape
    Fd = w2.shape[0]
    tm, tn = _tile(L, 704, 16), _tile(Fd, 1408, LANES)

    def epi(accs, ex):
        dact = accs[0] * FFN_RES
        av, bv = ex[0].astype(F32), ex[1].astype(F32)
        s = _sigmoid(av)
        return [dact * bv * (s * (1.0 + av * (1.0 - s))), dact * av * s]

    return _matmul(name, [(dh, w2)], "nt", tm, tn, D, [BF16, BF16], epilogue=epi, extras=[(a, "mn"), (b, "mn")])


def _ffn_fwd(tag, h, g, w1, w3, w2):
    L, D = h.shape
    xn = _rms_fwd(f"{tag}_norm", h, g)
    a, b, act = _swiglu_up(f"{tag}_up", xn, w1, w3)
    tm = _tile(L, 704, 8)
    (h_out,) = _matmul(f"{tag}_down", [(act, w2)], "nn", tm, D, w2.shape[0], [F32],
                       epilogue=lambda accs, ex: [ex[0] + FFN_RES * accs[0]], extras=[(h, "mn")])
    return h_out, (xn, a, b, act)


def _ffn_bwd(tag, dh_out, h, g, w1, w3, w2, saved):
    xn, a, b, act = saved
    L, D = h.shape
    Fd = w2.shape[0]
    tl = _tile(L, 704, 16)
    da, db = _swiglu_bwd(f"{tag}_dact", dh_out, w2, a, b)
    (dw2,) = _matmul(f"{tag}_dw2", [(act, dh_out)], "tn", _tile(Fd, 1408, LANES), D, tl, [F32],
                     epilogue=lambda accs, ex: [FFN_RES * accs[0]])
    (dw1,) = _matmul(f"{tag}_dw1", [(xn, da)], "tn", D, _tile(Fd, 1408, LANES), tl, [F32])
    (dw3,) = _matmul(f"{tag}_dw3", [(xn, db)], "tn", D, _tile(Fd, 1408, LANES), tl, [F32])
    (dxn,) = _matmul(f"{tag}_dxn", [(da, w1), (db, w3)], "nt", _tile(L, 704, 8), _tile(D, 512, LANES), Fd, [F32])
    dh, dg = _rms_bwd(f"{tag}_dnorm", dxn, h, g, dh_out)
    return dh, dg.reshape(-1), dw1, dw3, dw2


def _conv_taps(ext, k):
    return ext if k == 0 else pltpu.roll(ext, k, axis=0)


def _conv_fwd(name, zx, col0, w, b, pad):
    L = zx.shape[0]
    C = w.shape[1]
    tr, tc = _tile(L, 704, 8), _tile(C, 512, LANES)
    cb = col0 // tc
    assert col0 % tc == 0

    def body(u_ref, halo_ref, w_ref, b_ref, o_ref):
        ext = jnp.concatenate([halo_ref[...], u_ref[...]], axis=0)
        pre = b_ref[...] + sum(_conv_taps(ext, 3 - k)[8:] * w_ref[k:k + 1, :] for k in range(4))
        rows = _iota(pre.shape, 0) + pl.program_id(1) * tr
        o_ref[...] = jnp.where(rows >= pad, pre * _sigmoid(pre), 0.0)

    return pl.pallas_call(
        body, name=name, grid=(C // tc, L // tr),
        in_specs=[pl.BlockSpec((tr, tc), lambda j, i: (i, cb + j)),
                  pl.BlockSpec((8, tc), lambda j, i: (jnp.maximum(i * (tr // 8) - 1, 0), cb + j)),
                  pl.BlockSpec((4, tc), lambda j, i: (0, j)), pl.BlockSpec((1, tc), lambda j, i: (0, j))],
        out_specs=pl.BlockSpec((tr, tc), lambda j, i: (i, j)),
        out_shape=jax.ShapeDtypeStruct((L, C), F32), compiler_params=_params("parallel", "parallel"),
    )(zx, zx, w, b)


def _conv_bwd_pre(name, dact, zx, col0, w, b, pad):
    L = zx.shape[0]
    C = w.shape[1]
    tr, tc = _tile(L, 704, 8), _tile(C, 512, LANES)
    cb = col0 // tc

    def body(d_ref, u_ref, halo_ref, w_ref, b_ref, dp_ref, dw_ref, db_ref):
        ext = jnp.concatenate([halo_ref[...], u_ref[...]], axis=0)
        taps = [_conv_taps(ext, 3 - k)[8:] for k in range(4)]
        pre = b_ref[...] + sum(taps[k] * w_ref[k:k + 1, :] for k in range(4))
        s = _sigmoid(pre)
        rows = _iota(pre.shape, 0) + pl.program_id(1) * tr
        dpre = jnp.where(rows >= pad, d_ref[...] * (s * (1.0 + pre * (1.0 - s))), 0.0)
        dp_ref[...] = dpre

        @pl.when(pl.program_id(1) == 0)
        def _():
            dw_ref[...] = jnp.zeros_like(dw_ref)
            db_ref[...] = jnp.zeros_like(db_ref)

        db_ref[...] += jnp.sum(dpre, axis=0, keepdims=True)
        dw_ref[...] += jnp.concatenate([jnp.sum(dpre * taps[k], axis=0, keepdims=True) for k in range(4)], axis=0)

    return pl.pallas_call(
        body, name=name, grid=(C // tc, L // tr),
        in_specs=[pl.BlockSpec((tr, tc), lambda j, i: (i, j)),
                  pl.BlockSpec((tr, tc), lambda j, i: (i, cb + j)),
                  pl.BlockSpec((8, tc), lambda j, i: (jnp.maximum(i * (tr // 8) - 1, 0), cb + j)),
                  pl.BlockSpec((4, tc), lambda j, i: (0, j)), pl.BlockSpec((1, tc), lambda j, i: (0, j))],
        out_specs=[pl.BlockSpec((tr, tc), lambda j, i: (i, j)), pl.BlockSpec((4, tc), lambda j, i: (0, j)),
                   pl.BlockSpec((1, tc), lambda j, i: (0, j))],
        out_shape=[jax.ShapeDtypeStruct((L, C), F32), jax.ShapeDtypeStruct((4, C), F32), jax.ShapeDtypeStruct((1, C), F32)],
        compiler_params=_params("parallel", "arbitrary"),
    )(dact, zx, zx, w, b)


def _conv_bwd_in(name, dpre, w):
    L, C = dpre.shape
    tr, tc = _tile(L, 704, 16), _tile(C, 512, LANES)
    nr = L // tr

    def body(d_ref, halo_ref, w_ref, o_ref):
        halo = jnp.where(pl.program_id(1) == nr - 1, 0.0, halo_ref[...])
        ext = jnp.concatenate([d_ref[...], halo], axis=0)
        acc = ext[:tr] * w_ref[3:4, :]
        for k in range(3):
            acc = acc + pltpu.roll(ext, tr + 8 - (3 - k), axis=0)[:tr] * w_ref[k:k + 1, :]
        o_ref[...] = acc.astype(BF16)

    return pl.pallas_call(
        body, name=name, grid=(C // tc, nr),
        in_specs=[pl.BlockSpec((tr, tc), lambda j, i: (i, j)),
                  pl.BlockSpec((8, tc), lambda j, i: (jnp.minimum((i + 1) * (tr // 8), L // 8 - 1), j)),
                  pl.BlockSpec((4, tc), lambda j, i: (0, j))],
        out_specs=pl.BlockSpec((tr, tc), lambda j, i: (i, j)),
        out_shape=jax.ShapeDtypeStruct((L, C), BF16), compiler_params=_params("parallel", "parallel"),
    )(dpre, dpre, w)


def _dt_fwd(name, zx, col0, bias_row, nheads, pad):
    L = zx.shape[0]
    cb = col0 // LANES

    def body(x_ref, b_ref, dt_ref, dtt_ref):
        v = x_ref[...] + b_ref[...]
        sp = jnp.maximum(v, 0.0) + jnp.log(1.0 + jnp.exp(-jnp.abs(v)))
        rows = _iota(v.shape, 0) + pl.program_id(0) * BLK
        dt = jnp.where((rows >= pad) & (_iota(v.shape, 1) < nheads), sp, 0.0)
        dt_ref[...] = dt
        dtt_ref[...] = dt.T

    return pl.pallas_call(
        body, name=name, grid=(L // BLK,),
        in_specs=[pl.BlockSpec((BLK, LANES), lambda i: (i, cb)), pl.BlockSpec((1, LANES), lambda i: (0, 0))],
        out_specs=[pl.BlockSpec((BLK, LANES), lambda i: (i, 0)), pl.BlockSpec((LANES, BLK), lambda i: (0, i))],
        out_shape=[jax.ShapeDtypeStruct((L, LANES), F32), jax.ShapeDtypeStruct((LANES, L), F32)],
        compiler_params=_params("parallel"),
    )(zx, bias_row)


def _dt_bwd(name, ddt, zx, col0, bias_row, nheads, pad):
    L = zx.shape[0]
    cb = col0 // LANES

    def body(d_ref, x_ref, b_ref, o_ref, db_ref):
        v = x_ref[...] + b_ref[...]
        rows = _iota(v.shape, 0) + pl.program_id(0) * BLK
        g = jnp.where((rows >= pad) & (_iota(v.shape, 1) < nheads), d_ref[...] * _sigmoid(v), 0.0)
        o_ref[...] = g.astype(BF16)

        @pl.when(pl.program_id(0) == 0)
        def _():
            db_ref[...] = jnp.zeros_like(db_ref)

        db_ref[...] += jnp.sum(g, axis=0, keepdims=True)

    return pl.pallas_call(
        body, name=name, grid=(L // BLK,),
        in_specs=[pl.BlockSpec((BLK, LANES), lambda i: (i, 0)), pl.BlockSpec((BLK, LANES), lambda i: (i, cb)),
                  pl.BlockSpec((1, LANES), lambda i: (0, 0))],
        out_specs=[pl.BlockSpec((BLK, LANES), lambda i: (i, 0)), pl.BlockSpec((1, LANES), lambda i: (0, 0))],
        out_shape=[jax.ShapeDtypeStruct((L, LANES), BF16), jax.ShapeDtypeStruct((1, LANES), F32)],
        compiler_params=_params("arbitrary"),
    )(ddt, zx, bias_row)


def _ssd_common(dt, dtt, a_row, a_col):
    tril = (_iota((BLK, BLK), 0) >= _iota((BLK, BLK), 1)).astype(BF16)
    cum = _dotx(tril, dt * a_row, split="b")
    cumt = _dotx(dtt * a_col, tril, NT)
    return cum, cumt


def _ssd_fwd(name, xbc, dt, dtt, a_row, a_col, expand, di, ng):
    L = xbc.shape[0]
    nc = L // BLK
    hpg = di // HEAD // ng
    gw = hpg * HEAD
    assert gw % LANES == 0

    def body(x_ref, dt_ref, dtt_ref, ar_ref, ac_ref, ex_ref, y_ref, st_ref, h_ref):
        @pl.when(pl.program_id(0) == 0)
        def _():
            h_ref[...] = jnp.zeros_like(h_ref)

        st_ref[0] = h_ref[...]
        dt, dtt = dt_ref[...], dtt_ref[...]
        cum, cumt = _ssd_common(dt, dtt, ar_ref[...], ac_ref[...])
        ex = ex_ref[...]
        ecum_x = _dotx(jnp.exp(cum), ex)
        wend_x = _dotx(jnp.exp(cum[BLK - 1:BLK, :] - cum) * dt, ex)
        ecl = jnp.broadcast_to(jnp.exp(cumt[:, BLK - 1:BLK]), (LANES, LANES))
        decay_h = _dotx(ex, ecl, TN, split="b")
        causal = _iota((BLK, BLK), 0) >= _iota((BLK, BLK), 1)
        low = _iota((BLK, LANES), 1) < HEAD
        for g in range(ng):
            xg = x_ref[:, g * gw:(g + 1) * gw]
            bg = x_ref[:, di + g * D_STATE:di + (g + 1) * D_STATE].astype(BF16)
            cg = x_ref[:, di + (ng + g) * D_STATE:di + (ng + g + 1) * D_STATE].astype(BF16)
            hg = h_ref[g * gw:(g + 1) * gw, :]
            gram = _dot(cg, bg, NT)
            yoff = _dot(cg, hg.astype(BF16), NT) * ecum_x[:, g * gw:(g + 1) * gw]
            parts = []
            for j in range(gw // LANES):
                xp = xg[:, j * LANES:(j + 1) * LANES].astype(BF16)
                yd = []
                for hh in range(2):
                    h = g * hpg + 2 * j + hh
                    seg = cum[:, h:h + 1] - cumt[h:h + 1, :]
                    m = gram * jnp.where(causal, jnp.exp(jnp.minimum(seg, 0.0)), 0.0) * dtt[h:h + 1, :]
                    yd.append(_dot(m.astype(BF16), xp))
                parts.append(jnp.where(low, yd[0], yd[1]))
            y_ref[:, g * gw:(g + 1) * gw] = jnp.concatenate(parts, axis=1) + yoff
            xw = (xg * wend_x[:, g * gw:(g + 1) * gw]).astype(BF16)
            h_ref[g * gw:(g + 1) * gw, :] = hg * decay_h[g * gw:(g + 1) * gw, :] + _dot(xw, bg, TN)

    W = xbc.shape[1]
    full = lambda r, c: pl.BlockSpec((r, c), lambda i: (0, 0))
    return pl.pallas_call(
        body, name=name, grid=(nc,),
        in_specs=[pl.BlockSpec((BLK, W), lambda i: (i, 0)), pl.BlockSpec((BLK, LANES), lambda i: (i, 0)),
                  pl.BlockSpec((LANES, BLK), lambda i: (0, i)), full(1, LANES), full(LANES, LANES), full(LANES, di)],
        out_specs=[pl.BlockSpec((BLK, di), lambda i: (i, 0)), pl.BlockSpec((1, di, D_STATE), lambda i: (i, 0, 0))],
        out_shape=[jax.ShapeDtypeStruct((L, di), F32), jax.ShapeDtypeStruct((nc, di, D_STATE), F32)],
        scratch_shapes=[pltpu.VMEM((di, D_STATE), F32)], compiler_params=_params("arbitrary"),
    )(xbc, dt, dtt, a_row, a_col, expand)


def _ssd_bwd(name, xbc, dt, dtt, a_row, a_col, expand, states, dy, d_x, di, ng):
    L, W = xbc.shape
    nc = L // BLK
    hpg = di // HEAD // ng
    gw = hpg * HEAD

    def body(x_ref, dt_ref, dtt_ref, ar_ref, ac_ref, ex_ref, st_ref, dy_ref, dx_ref_in, dxo_ref, ddt_ref, da_ref, dh_ref):
        @pl.when(pl.program_id(0) == 0)
        def _():
            dh_ref[...] = jnp.zeros_like(dh_ref)
            da_ref[...] = jnp.zeros_like(da_ref)

        dt, dtt, a_row = dt_ref[...], dtt_ref[...], ar_ref[...]
        cum, cumt = _ssd_common(dt, dtt, a_row, ac_ref[...])
        ex = ex_ref[...]
        ecum = jnp.exp(cum)
        ecum_x = _dotx(ecum, ex)
        e_s = jnp.exp(cum[BLK - 1:BLK, :] - cum)
        wend_x = _dotx(e_s * dt, ex)
        ecl_col = jnp.exp(cumt[:, BLK - 1:BLK])
        decay_h = _dotx(ex, jnp.broadcast_to(ecl_col, (LANES, LANES)), TN, split="b")
        causal = _iota((BLK, BLK), 0) >= _iota((BLK, BLK), 1)
        low = _iota((BLK, LANES), 1) < HEAD
        lane = _iota((1, LANES), 1)
        sub = _iota((LANES, 1), 0)
        dcum_c = jnp.zeros((BLK, LANES), F32)
        dcum_r = jnp.zeros((LANES, BLK), F32)
        ddt_r = jnp.zeros((LANES, BLK), F32)
        zoff = []
        dwend_src = []
        for g in range(ng):
            gs = slice(g * gw, (g + 1) * gw)
            xg = x_ref[:, gs]
            bg = x_ref[:, di + g * D_STATE:di + (g + 1) * D_STATE].astype(BF16)
            cg = x_ref[:, di + (ng + g) * D_STATE:di + (ng + g + 1) * D_STATE].astype(BF16)
            hprev = st_ref[0, gs, :]
            dhn = dh_ref[gs, :]
            dyg = dy_ref[:, gs]
            gram = _dot(cg, bg, NT)
            dgram = jnp.zeros((BLK, BLK), F32)
            dxg = []
            for j in range(gw // LANES):
                xp = xg[:, j * LANES:(j + 1) * LANES].astype(BF16)
                dyp = dyg[:, j * LANES:(j + 1) * LANES]
                dxh = []
                for hh in range(2):
                    h = g * hpg + 2 * j + hh
                    seg = cum[:, h:h + 1] - cumt[h:h + 1, :]
                    lm = jnp.where(causal, jnp.exp(jnp.minimum(seg, 0.0)), 0.0)
                    dtr = dtt[h:h + 1, :]
                    m = gram * lm * dtr
                    dym = jnp.where(low if hh == 0 else ~low, dyp, 0.0).astype(BF16)
                    dxh.append(_dot(m.astype(BF16), dym, TN))
                    dm = _dot(dym, xp, NT)
                    dgram = dgram + dm * lm * dtr
                    v = dm * gram * lm
                    wv = v * dtr
                    ddt_r = ddt_r + jnp.where(sub == h, jnp.sum(v, axis=0, keepdims=True), 0.0)
                    dcum_r = dcum_r - jnp.where(sub == h, jnp.sum(wv, axis=0, keepdims=True), 0.0)
                    dcum_c = dcum_c + jnp.where(lane == h, jnp.sum(wv, axis=1, keepdims=True), 0.0)
                dxg.append(jnp.where(low, dxh[0], dxh[1]))
            dx_diag = jnp.concatenate(dxg, axis=1)
            hb = hprev.astype(BF16)
            yoff = _dot(cg, hb, NT) * ecum_x[:, gs]
            dye = (dyg * ecum_x[:, gs]).astype(BF16)
            dcg = _dot(dye, hb) + _dot(dgram.astype(BF16), bg)
            dbg = _dot(dgram.astype(BF16), cg, TN)
            dh_prev = _dot(dye, cg, TN)
            zoff.append(dyg * yoff)
            dhb = dhn.astype(BF16)
            dxw = _dot(bg, dhb, NT)
            xw = (xg * wend_x[:, gs]).astype(BF16)
            dbg = dbg + _dot(xw, dhb)
            dwend_src.append(dxw * xg)
            dxo_ref[:, gs] = dx_diag + dxw * wend_x[:, gs] + dyg * dx_ref_in[:, gs]
            dxo_ref[:, di + g * D_STATE:di + (g + 1) * D_STATE] = dbg
            dxo_ref[:, di + (ng + g) * D_STATE:di + (ng + g + 1) * D_STATE] = dcg
            prod = dhn * hprev
            dd = jnp.sum(_dotx(ex[:, gs], prod, split="b"), axis=1, keepdims=True)
            dcum_r = dcum_r + jnp.where(_iota((1, BLK), 1) == BLK - 1, dd * ecl_col, 0.0)
            dh_ref[gs, :] = dhn * decay_h[gs, :] + dh_prev
        dcum_c = dcum_c + _dotx(jnp.concatenate(zoff, axis=1), ex, NT)
        dwend = _dotx(jnp.concatenate(dwend_src, axis=1), ex, NT)
        ddt_c = dwend * e_s
        de = dwend * dt * e_s
        dcum_c = dcum_c - de + jnp.where(_iota((BLK, 1), 0) == BLK - 1, jnp.sum(de, axis=0, keepdims=True), 0.0)
        dcum = dcum_c + dcum_r.T
        triu = (_iota((BLK, BLK), 0) <= _iota((BLK, BLK), 1)).astype(BF16)
        da = _dotx(triu, dcum, split="b")
        ddt_ref[...] = ddt_c + ddt_r.T + da * a_row
        da_ref[...] += jnp.sum(da * dt, axis=0, keepdims=True)

    rev = lambda i: nc - 1 - i
    full = lambda r, c: pl.BlockSpec((r, c), lambda i: (0, 0))
    return pl.pallas_call(
        body, name=name, grid=(nc,),
        in_specs=[pl.BlockSpec((BLK, W), lambda i: (rev(i), 0)), pl.BlockSpec((BLK, LANES), lambda i: (rev(i), 0)),
                  pl.BlockSpec((LANES, BLK), lambda i: (0, rev(i))), full(1, LANES), full(LANES, LANES), full(LANES, di),
                  pl.BlockSpec((1, di, D_STATE), lambda i: (rev(i), 0, 0)), pl.BlockSpec((BLK, di), lambda i: (rev(i), 0)),
                  full(1, di)],
        out_specs=[pl.BlockSpec((BLK, W), lambda i: (rev(i), 0)), pl.BlockSpec((BLK, LANES), lambda i: (rev(i), 0)),
                   full(1, LANES)],
        out_shape=[jax.ShapeDtypeStruct((L, W), F32), jax.ShapeDtypeStruct((L, LANES), F32),
                   jax.ShapeDtypeStruct((1, LANES), F32)],
        scratch_shapes=[pltpu.VMEM((di, D_STATE), F32)], compiler_params=_params("arbitrary"),
    )(xbc, dt, dtt, a_row, a_col, expand, states, dy, d_x)


def _group_sums(v, gsz):
    cols = []
    for j in range(0, v.shape[1], gsz):
        s = jnp.sum(v[:, j:j + gsz], axis=1, keepdims=True)
        cols.append(jnp.broadcast_to(s, (v.shape[0], gsz)))
    return jnp.concatenate(cols, axis=1)


def _gate_fwd(name, y, xbc, zx, d_x, ng_row, gsz):
    L, di = y.shape
    tr = _tile(L, 512, 16)

    def body(y_ref, x_ref, z_ref, d_ref, g_ref, o_ref):
        z = z_ref[...]
        y2 = (y_ref[...] + d_ref[...] * x_ref[...]) * (z * _sigmoid(z))
        r = lax.rsqrt(_group_sums(y2 * y2, gsz) * (1.0 / gsz) + RMS_EPS)
        o_ref[...] = (y2 * r * g_ref[...]).astype(BF16)

    row = pl.BlockSpec((tr, di), lambda i: (i, 0))
    vec = pl.BlockSpec((1, di), lambda i: (0, 0))
    return pl.pallas_call(
        body, name=name, grid=(L // tr,), in_specs=[row, row, row, vec, vec], out_specs=row,
        out_shape=jax.ShapeDtypeStruct((L, di), BF16), compiler_params=_params("parallel"),
    )(y, xbc, zx, d_x, ng_row)


def _gate_bwd(name, dy3, y, xbc, zx, d_x, ng_row, gsz):
    L, di = y.shape
    tr = _tile(L, 256, 16)

    def body(dy_ref, y_ref, x_ref, z_ref, d_ref, g_ref, dz_ref, dy1_ref, dg_ref, dd_ref):
        z, x = z_ref[...], x_ref[...]
        s = _sigmoid(z)
        sz = z * s
        y1 = y_ref[...] + d_ref[...] * x
        y2 = y1 * sz
        r = lax.rsqrt(_group_sums(y2 * y2, gsz) * (1.0 / gsz) + RMS_EPS)
        yg = y2 * r
        dy3 = dy_ref[...]
        dyg = dy3 * g_ref[...]
        dy2 = r * (dyg - yg * (_group_sums(dyg * yg, gsz) * (1.0 / gsz)))
        dz_ref[...] = (dy2 * y1 * (s * (1.0 + z * (1.0 - s)))).astype(BF16)
        dy1 = dy2 * sz
        dy1_ref[...] = dy1

        @pl.when(pl.program_id(0) == 0)
        def _():
            dg_ref[...] = jnp.zeros_like(dg_ref)
            dd_ref[...] = jnp.zeros_like(dd_ref)

        dg_ref[...] += jnp.sum(dy3 * yg, axis=0, keepdims=True)
        dd_ref[...] += jnp.sum(dy1 * x, axis=0, keepdims=True)

    row = pl.BlockSpec((tr, di), lambda i: (i, 0))
    vec = pl.BlockSpec((1, di), lambda i: (0, 0))
    return pl.pallas_call(
        body, name=name, grid=(L // tr,), in_specs=[row, row, row, row, vec, vec], out_specs=[row, row, vec, vec],
        out_shape=[jax.ShapeDtypeStruct((L, di), BF16), jax.ShapeDtypeStruct((L, di), F32),
                   jax.ShapeDtypeStruct((1, di), F32), jax.ShapeDtypeStruct((1, di), F32)],
        compiler_params=_params("arbitrary"),
    )(dy3, y, xbc, zx, d_x, ng_row)


def _sb_tile(qh, kblk, i, kb, pad):
    z = _dot(qh, kblk, NT)
    e = jnp.exp(-jnp.abs(z))
    l1 = jnp.log(1.0 + e)
    lsz = jnp.minimum(z, 0.0) - l1
    t = _iota((BLK, BLK), 0) + i * BLK
    s = _iota((BLK, BLK), 1) + kb * BLK
    valid = (s < t) & (s >= pad)
    lkm = jnp.where(valid, -jnp.maximum(z, 0.0) - l1, 0.0)
    return z, e, lsz, lkm, valid


def _sb_fwd(name, q, k, v, pad):
    L, D = q.shape
    nb = L // BLK

    def body(q_ref, k_ref, v_ref, o_ref):
        i = pl.program_id(1)
        after = (_iota((BLK, BLK), 0) > _iota((BLK, BLK), 1)).astype(BF16)
        lane = _iota((BLK, LANES), 1)
        qs = q_ref[...] * 0.125
        out = jnp.zeros((BLK, LANES), F32)
        for hh in range(2):
            mine = (lane < HEAD) if hh == 0 else (lane >= HEAD)
            qh = jnp.where(mine, qs, 0).astype(BF16)

            def step(j, carry):
                c, acc = carry
                kb = i - j
                rows = pl.ds(pl.multiple_of(kb * BLK, BLK), BLK)
                _, _, lsz, lkm, valid = _sb_tile(qh, k_ref[rows, :], i, kb, pad)
                sfx = c + _dotx(lkm, after, parts=2)
                a = jnp.where(valid, jnp.exp(lsz + sfx), 0.0)
                vm = jnp.where(mine, v_ref[rows, :], 0).astype(BF16)
                return c + jnp.sum(lkm, axis=1, keepdims=True), acc + _dot(a.astype(BF16), vm)

            _, acc = lax.fori_loop(0, i + 1, step, (jnp.zeros((BLK, 1), F32), jnp.zeros((BLK, LANES), F32)))
            out = out + acc
        o_ref[...] = out.astype(BF16)

    return pl.pallas_call(
        body, name=name, grid=(D // LANES, nb),
        in_specs=[pl.BlockSpec((BLK, LANES), lambda p, i: (i, p)), pl.BlockSpec((L, LANES), lambda p, i: (0, p)),
                  pl.BlockSpec((L, LANES), lambda p, i: (0, p))],
        out_specs=pl.BlockSpec((BLK, LANES), lambda p, i: (i, p)),
        out_shape=jax.ShapeDtypeStruct((L, D), BF16), compiler_params=_params("parallel", "arbitrary"),
    )(q, k, v)


def _sb_bwd(name, q, k, v, do, pad):
    L, D = q.shape
    nb = L // BLK

    def body(q_ref, k_ref, v_ref, do_ref, dq_ref, dk_ref, dv_ref, cs_ref):
        i = pl.program_id(1)

        @pl.when(i == 0)
        def _():
            dk_ref[...] = jnp.zeros_like(dk_ref)
            dv_ref[...] = jnp.zeros_like(dv_ref)

        after = (_iota((BLK, BLK), 0) > _iota((BLK, BLK), 1)).astype(BF16)
        before = (_iota((BLK, BLK), 0) < _iota((BLK, BLK), 1)).astype(BF16)
        lane = _iota((BLK, LANES), 1)
        qs = q_ref[...] * 0.125
        dov = do_ref[...]
        dq = jnp.zeros((BLK, LANES), F32)
        for hh in range(2):
            mine = (lane < HEAD) if hh == 0 else (lane >= HEAD)
            qh = jnp.where(mine, qs, 0).astype(BF16)
            doh = jnp.where(mine, dov, 0).astype(BF16)

            def near_to_far(j, c):
                kb = i - j
                rows = pl.ds(pl.multiple_of(kb * BLK, BLK), BLK)
                _, _, _, lkm, _ = _sb_tile(qh, k_ref[rows, :], i, kb, pad)
                cs_ref[rows, :] = jnp.broadcast_to(c, (BLK, LANES))
                return c + jnp.sum(lkm, axis=1, keepdims=True)

            lax.fori_loop(0, i + 1, near_to_far, jnp.zeros((BLK, 1), F32))

            def far_to_near(kb, carry):
                p, acc = carry
                rows = pl.ds(pl.multiple_of(kb * BLK, BLK), BLK)
                kblk = k_ref[rows, :]
                z, e, lsz, lkm, valid = _sb_tile(qh, kblk, i, kb, pad)
                sfx = cs_ref[rows, :] + _dotx(lkm, after, parts=2)
                a = jnp.where(valid, jnp.exp(lsz + sfx), 0.0)
                dlog = _dot(doh, v_ref[rows, :], NT) * a
                pfx = p + _dotx(dlog, before, parts=2)
                inv = 1.0 / (1.0 + e)
                sig = jnp.where(z >= 0.0, inv, e * inv)
                dz = (dlog * (1.0 - sig) - jnp.where(valid, pfx * sig, 0.0)).astype(BF16)
                dk_ref[rows, :] += _dot(dz, qh, TN)
                dv_ref[rows, :] += _dot(a.astype(BF16), doh, TN)
                km = jnp.where(mine, kblk, 0).astype(BF16)
                return p + jnp.sum(dlog, axis=1, keepdims=True), acc + _dot(dz, km)

            _, acc = lax.fori_loop(0, i + 1, far_to_near, (jnp.zeros((BLK, 1), F32), jnp.zeros((BLK, LANES), F32)))
            dq = dq + acc
        dq_ref[...] = dq * 0.125

    blk = pl.BlockSpec((BLK, LANES), lambda p, i: (i, p))
    col = pl.BlockSpec((L, LANES), lambda p, i: (0, p))
    return pl.pallas_call(
        body, name=name, grid=(D // LANES, nb), in_specs=[blk, col, col, blk], out_specs=[blk, col, col],
        out_shape=[jax.ShapeDtypeStruct((L, D), F32)] * 3,
        scratch_shapes=[pltpu.VMEM((L, LANES), F32)], compiler_params=_params("parallel", "arbitrary"),
    )(q, k, v, do)


def _local_step(x, tgt, w):
    S, D = x.shape
    nm = w["meta_tokens"].shape[0]
    pad = BLK - nm
    L = pad + nm + S
    assert L % BLK == 0 and 0 < nm <= BLK
    di = w["ssm_out_proj"].shape[0]
    nh = w["ssm_dt_bias"].shape[0]
    assert di == nh * HEAD and nh <= LANES
    conv_dim = w["ssm_conv_w"].shape[1]
    ng = (conv_dim - di) // (2 * D_STATE)
    zp = di + conv_dim + LANES
    g = w["norm_g"]
    grads = {}

    h0 = jnp.concatenate([jnp.zeros((pad, D), F32), w["meta_tokens"], x], axis=0)

    h1, s1 = _ffn_fwd("f00", h0, g[0, 0], w["ffn_w1"][0, 0], w["ffn_w3"][0, 0], w["ffn_w2"][0, 0])
    u0 = _rms_fwd("m_norm", h1, g[0, 1])
    w_in = jnp.concatenate([w["ssm_in_proj"], jnp.zeros((D, zp - w["ssm_in_proj"].shape[1]), BF16)], axis=1)
    tl = _tile(L, 704, 16)
    (zx,) = _matmul("m_inproj", [(u0, w_in)], "nn", tl, _tile(zp, 1024, LANES), D, [F32])
    conv_b = w["ssm_conv_b"].reshape(1, conv_dim)
    xbc = _conv_fwd("m_conv", zx, di, w["ssm_conv_w"], conv_b, pad)
    bias_row = jnp.zeros((1, LANES), F32).at[0, :nh].set(w["ssm_dt_bias"])
    dt, dtt = _dt_fwd("m_dt", zx, di + conv_dim, bias_row, nh, pad)
    a_neg = -jnp.exp(w["ssm_a_log"])
    a_row = jnp.zeros((1, LANES), F32).at[0, :nh].set(a_neg)
    a_col = jnp.broadcast_to(a_row.reshape(LANES, 1), (LANES, LANES))
    expand = (jnp.arange(LANES)[:, None] == (jnp.arange(di) // HEAD)[None, :]).astype(BF16)
    y_ssd, states = _ssd_fwd("m_ssd", xbc, dt, dtt, a_row, a_col, expand, di, ng)
    d_x = jnp.repeat(w["ssm_d"], HEAD).reshape(1, di)
    ssm_g = w["ssm_norm_g"].reshape(1, di)
    gsz = di // ng
    y3 = _gate_fwd("m_gate", y_ssd, xbc, zx, d_x, ssm_g, gsz)
    (h2,) = _matmul("m_outproj", [(y3, w["ssm_out_proj"])], "nn", tl, D, di, [F32],
                    epilogue=lambda accs, ex: [ex[0] + accs[0]], extras=[(h1, "mn")])
    h3, s2 = _ffn_fwd("f01", h2, g[0, 2], w["ffn_w1"][0, 1], w["ffn_w3"][0, 1], w["ffn_w2"][0, 1])

    kv_in = _rms_fwd("kv_norm", h3, w["kv_norm_g"])
    (k_raw,) = _matmul("kv_k", [(kv_in, w["w_k"])], "nn", tl, D, D, [F32])
    (v_sh,) = _matmul("kv_v", [(kv_in, w["w_v"])], "nn", tl, D, D, [BF16])
    kg = jnp.tile(w["k_norm_g"], D // HEAD).reshape(1, D)
    k_sh = _headrms_fwd("kv_knorm", k_raw, kg)

    h4, s3 = _ffn_fwd("f10", h3, g[1, 0], w["ffn_w1"][1, 0], w["ffn_w3"][1, 0], w["ffn_w2"][1, 0])
    u1 = _rms_fwd("a_norm", h4, g[1, 1])
    (q_raw,) = _matmul("a_q", [(u1, w["sb_w_q"])], "nn", tl, D, D, [F32])
    qg = jnp.tile(w["sb_q_norm_g"], D // HEAD).reshape(1, D)
    q = _headrms_fwd("a_qnorm", q_raw, qg)
    o = _sb_fwd("a_attn", q, k_sh, v_sh, pad)
    (h5,) = _matmul("a_o", [(o, w["sb_w_o"])], "nn", tl, D, D, [F32],
                    epilogue=lambda accs, ex: [ex[0] + accs[0]], extras=[(h4, "mn")])
    h6, s4 = _ffn_fwd("f11", h5, g[1, 2], w["ffn_w1"][1, 1], w["ffn_w3"][1, 1], w["ffn_w2"][1, 1])

    dh6, sq = _loss("loss", h6, tgt, pad + nm)

    dg = jnp.zeros_like(g)
    dw1 = [[None, None], [None, None]]
    dw3 = [[None, None], [None, None]]
    dw2 = [[None, None], [None, None]]
    dh5, dgv, dw1[1][1], dw3[1][1], dw2[1][1] = _ffn_bwd("b11", dh6, h5, g[1, 2], w["ffn_w1"][1, 1], w["ffn_w3"][1, 1],
                                                           w["ffn_w2"][1, 1], s4)
    dg = dg.at[1, 2].set(dgv)
    td = _tile(D, 512, LANES)
    (do,) = _matmul("b_do", [(dh5, w["sb_w_o"])], "nt", tl, D, D, [BF16])
    (grads["sb_w_o"],) = _matmul("b_dwo", [(o, dh5)], "tn", D, td, tl, [F32])
    dq, dk, dv = _sb_bwd("b_attn", q, k_sh, v_sh, do, pad)
    dq_raw, dqg = _headrms_bwd("b_qnorm", dq, q_raw, qg)
    grads["sb_q_norm_g"] = dqg.reshape(D // HEAD, HEAD).sum(0)
    (grads["sb_w_q"],) = _matmul("b_dwq", [(u1, dq_raw)], "tn", D, td, tl, [F32])
    (du1,) = _matmul("b_du1", [(dq_raw, w["sb_w_q"])], "nt", tl, D, D, [F32])
    dh4, dgv = _rms_bwd("b_anorm", du1, h4, g[1, 1], dh5)
    dg = dg.at[1, 1].set(dgv.reshape(-1))
    dh3, dgv, dw1[1][0], dw3[1][0], dw2[1][0] = _ffn_bwd("b10", dh4, h3, g[1, 0], w["ffn_w1"][1, 0], w["ffn_w3"][1, 0],
                                                           w["ffn_w2"][1, 0], s3)
    dg = dg.at[1, 0].set(dgv)

    dk_raw, dkg = _headrms_bwd("b_knorm", dk, k_raw, kg)
    grads["k_norm_g"] = dkg.reshape(D // HEAD, HEAD).sum(0)
    (grads["w_k"],) = _matmul("b_dwk", [(kv_in, dk_raw)], "tn", D, td, tl, [F32])
    (grads["w_v"],) = _matmul("b_dwv", [(kv_in, dv)], "tn", D, td, tl, [F32])
    (dkv_in,) = _matmul("b_dkvin", [(dk_raw, w["w_k"]), (dv, w["w_v"])], "nt", tl, D, D, [F32])
    dh3, dgv = _rms_bwd("b_kvnorm", dkv_in, h3, w["kv_norm_g"], dh3)
    grads["kv_norm_g"] = dgv.reshape(-1)

    dh2, dgv, dw1[0][1], dw3[0][1], dw2[0][1] = _ffn_bwd("b01", dh3, h2, g[0, 2], w["ffn_w1"][0, 1], w["ffn_w3"][0, 1],
                                                           w["ffn_w2"][0, 1], s2)
    dg = dg.at[0, 2].set(dgv)
    (dy3,) = _matmul("b_dy3", [(dh2, w["ssm_out_proj"])], "nt", tl, _tile(di, 1024, LANES), D, [F32])
    (grads["ssm_out_proj"],) = _matmul("b_dwout", [(y3, dh2)], "tn", _tile(di, 1024, LANES), D, tl, [F32])
    dz, dy1, dssm_g, dd_x = _gate_bwd("b_gate", dy3, y_ssd, xbc, zx, d_x, ssm_g, gsz)
    grads["ssm_norm_g"] = dssm_g.reshape(-1)
    grads["ssm_d"] = dd_x.reshape(nh, HEAD).sum(1)
    dxbc, ddt, da = _ssd_bwd("b_ssd", xbc, dt, dtt, a_row, a_col, expand, states, dy1, d_x, di, ng)
    grads["ssm_a_log"] = da[0, :nh] * a_neg
    ddt_raw, dbias = _dt_bwd("b_dt", ddt, zx, di + conv_dim, bias_row, nh, pad)
    grads["ssm_dt_bias"] = dbias[0, :nh]
    dpre, grads["ssm_conv_w"], dconv_b = _conv_bwd_pre("b_convpre", dxbc, zx, di, w["ssm_conv_w"], conv_b, pad)
    grads["ssm_conv_b"] = dconv_b.reshape(-1)
    dxbc_raw = _conv_bwd_in("b_convin", dpre, w["ssm_conv_w"])
    dzx = jnp.concatenate([dz, dxbc_raw, ddt_raw], axis=1)
    (dw_in,) = _matmul("b_dwin", [(u0, dzx)], "tn", D, _tile(zp, 1024, LANES), tl, [F32])
    grads["ssm_in_proj"] = dw_in[:, :w["ssm_in_proj"].shape[1]]
    (du0,) = _matmul("b_du0", [(dzx, w_in)], "nt", _tile(L, 352, 16), td, zp, [F32])
    dh1, dgv = _rms_bwd("b_mnorm", du0, h1, g[0, 1], dh2)
    dg = dg.at[0, 1].set(dgv.reshape(-1))
    dh0, dgv, dw1[0][0], dw3[0][0], dw2[0][0] = _ffn_bwd("b00", dh1, h0, g[0, 0], w["ffn_w1"][0, 0], w["ffn_w3"][0, 0],
                                                           w["ffn_w2"][0, 0], s1)
    dg = dg.at[0, 0].set(dgv)

    grads["norm_g"] = dg
    grads["ffn_w1"] = jnp.stack([jnp.stack(r) for r in dw1])
    grads["ffn_w3"] = jnp.stack([jnp.stack(r) for r in dw3])
    grads["ffn_w2"] = jnp.stack([jnp.stack(r) for r in dw2])
    grads["meta_tokens"] = dh0[pad:pad + nm]
    return sq, dh0[pad + nm:], grads


HBM_SPEC = pl.BlockSpec(memory_space=pltpu.HBM)


def _place():
    return lax.axis_index("x"), lax.axis_index("y"), lax.axis_index("c")


def _allgather8(name, blk):
    m, n = blk.shape

    def body(x_ref, out_ref, send_sems, recv_sems, local_sem):
        x, y, c = _place()
        me, sibling = (x, y, c), (x, y, 1 - c)
        chips = [(1 - x, y), (x, 1 - y), (1 - x, 1 - y)]

        def rows(px, py, pc):
            return out_ref.at[pl.ds((4 * px + 2 * py + pc) * m, m), :]

        def copy(k, block, to, src=None):
            return pltpu.make_async_remote_copy(
                src_ref=rows(*block) if src is None else src, dst_ref=rows(*block),
                send_sem=send_sems.at[k], recv_sem=recv_sems.at[k], device_id=to, device_id_type=MESH)

        mine = pltpu.make_async_copy(x_ref, rows(*me), local_sem)
        mine.start()
        first = [copy(0, me, sibling, src=x_ref)]
        first += [copy(1 + j, me, (*chip, c), src=x_ref) for j, chip in enumerate(chips)]
        for cp in first:
            cp.start()
        passed = [copy(4 + j, (*chip, c), sibling) for j, chip in enumerate(chips)]
        for j, chip in enumerate(chips):
            copy(1 + j, (*chip, c), me).wait_recv()
            passed[j].start()
        copy(0, sibling, me).wait_recv()
        for j, chip in enumerate(chips):
            copy(4 + j, (*chip, 1 - c), me).wait_recv()
        for cp in first + passed:
            cp.wait_send()
        mine.wait()

    return pl.pallas_call(
        body, name=name, out_shape=jax.ShapeDtypeStruct((8 * m, n), blk.dtype),
        in_specs=[HBM_SPEC], out_specs=HBM_SPEC,
        scratch_shapes=[pltpu.SemaphoreType.DMA((7,)), pltpu.SemaphoreType.DMA((7,)), pltpu.SemaphoreType.DMA],
    )(blk)


def _exchange8(name, g):
    _, m, n = g.shape

    def body(g_ref, out_ref, send_sems, recv_sems, local_sem):
        x, y, c = _place()
        me_id = 4 * x + 2 * y + c
        mine = pltpu.make_async_copy(g_ref.at[me_id], out_ref.at[me_id], local_sem)
        mine.start()
        sends, recvs = [], []
        for k in range(1, 8):
            px = 1 - x if k & 4 else x
            py = 1 - y if k & 2 else y
            pc = 1 - c if k & 1 else c
            pid = 4 * px + 2 * py + pc
            sends.append(pltpu.make_async_remote_copy(
                src_ref=g_ref.at[pid], dst_ref=out_ref.at[me_id], send_sem=send_sems.at[k - 1],
                recv_sem=recv_sems.at[k - 1], device_id=(px, py, pc), device_id_type=MESH))
            recvs.append(pltpu.make_async_remote_copy(
                src_ref=g_ref.at[me_id], dst_ref=out_ref.at[pid], send_sem=send_sems.at[k - 1],
                recv_sem=recv_sems.at[k - 1], device_id=(px, py, pc), device_id_type=MESH))
        for cp in sends:
            cp.start()
        for cp in recvs:
            cp.wait_recv()
        for cp in sends:
            cp.wait_send()
        mine.wait()

    return pl.pallas_call(
        body, name=name, out_shape=jax.ShapeDtypeStruct(g.shape, g.dtype), in_specs=[HBM_SPEC], out_specs=HBM_SPEC,
        scratch_shapes=[pltpu.SemaphoreType.DMA((7,)), pltpu.SemaphoreType.DMA((7,)), pltpu.SemaphoreType.DMA],
    )(g)


def _pairshare(name, half):
    m, n = half.shape

    def body(x_ref, out_ref, send_sem, recv_sem, local_sem):
        x, y, c = _place()
        mine = pltpu.make_async_copy(x_ref, out_ref.at[pl.ds(c * m, m), :], local_sem)
        mine.start()
        send = pltpu.make_async_remote_copy(
            src_ref=x_ref, dst_ref=out_ref.at[pl.ds(c * m, m), :], send_sem=send_sem, recv_sem=recv_sem,
            device_id=(x, y, 1 - c), device_id_type=MESH)
        send.start()
        pltpu.make_async_remote_copy(
            src_ref=x_ref, dst_ref=out_ref.at[pl.ds((1 - c) * m, m), :], send_sem=send_sem, recv_sem=recv_sem,
            device_id=(x, y, 1 - c), device_id_type=MESH).wait_recv()
        send.wait_send()
        mine.wait()

    return pl.pallas_call(
        body, name=name, out_shape=jax.ShapeDtypeStruct((2 * m, n), half.dtype), in_specs=[HBM_SPEC], out_specs=HBM_SPEC,
        scratch_shapes=[pltpu.SemaphoreType.DMA, pltpu.SemaphoreType.DMA, pltpu.SemaphoreType.DMA],
    )(half)


def _sum8(name, parts):
    _, m, n = parts.shape
    tr = _tile(m, 256, 8)

    def body(p_ref, o_ref):
        acc = p_ref[0]
        for s in range(1, 8):
            acc = acc + p_ref[s]
        o_ref[...] = acc

    return pl.pallas_call(
        body, name=name, grid=(m // tr,), in_specs=[pl.BlockSpec((8, tr, n), lambda i: (0, i, 0))],
        out_specs=pl.BlockSpec((tr, n), lambda i: (i, 0)), out_shape=jax.ShapeDtypeStruct((m, n), F32),
        compiler_params=_params("parallel"),
    )(parts)


def _adamw(name, w, g, m, v):
    shape = w.shape
    cols = shape[-1]
    rows = math.prod(shape[:-1])
    tr = _tile(rows, 512, 8) if rows * cols > 2 ** 19 else rows

    def body(w_ref, g_ref, m_ref, v_ref, d_ref, mo_ref, vo_ref):
        gv = g_ref[...]
        m2 = ADAM_B1 * m_ref[...] + (1.0 - ADAM_B1) * gv
        v2 = ADAM_B2 * v_ref[...] + (1.0 - ADAM_B2) * (gv * gv)
        m_hat = m2 / (1.0 - ADAM_B1 ** ADAM_STEP)
        v_hat = v2 / (1.0 - ADAM_B2 ** ADAM_STEP)
        d_ref[...] = -ADAM_LR * (m_hat / (jnp.sqrt(v_hat) + ADAM_EPS) + ADAM_WD * w_ref[...])
        mo_ref[...] = m2
        vo_ref[...] = v2

    spec = pl.BlockSpec((tr, cols), lambda i: (i, 0))
    outs = pl.pallas_call(
        body, name=name, grid=(rows // tr,), in_specs=[spec] * 4, out_specs=[spec] * 3,
        out_shape=[jax.ShapeDtypeStruct((rows, cols), F32)] * 3, compiler_params=_params("parallel"),
    )(*(a.reshape(rows, cols) for a in (w, g, m, v)))
    return tuple(o.reshape(shape) for o in outs)


WEIGHTS = ["meta_tokens", "norm_g", "ffn_w1", "ffn_w3", "ffn_w2", "ssm_in_proj", "ssm_conv_w", "ssm_conv_b", "ssm_dt_bias",
           "ssm_a_log", "ssm_d", "ssm_norm_g", "ssm_out_proj", "kv_norm_g", "w_k", "k_norm_g", "w_v", "sb_w_q",
           "sb_q_norm_g", "sb_w_o"]
SHARD_AXIS = {"meta_tokens": 1, "norm_g": 2, "ffn_w1": 3, "ffn_w3": 3, "ffn_w2": 2, "ssm_in_proj": 2, "ssm_conv_w": 2,
              "ssm_conv_b": 1, "ssm_norm_g": 1, "ssm_out_proj": 1, "w_k": 0, "w_v": 0, "sb_w_q": 1, "sb_w_o": 1}
MATRICES = ["ffn_w1", "ffn_w3", "ffn_w2", "ssm_in_proj", "ssm_out_proj", "w_k", "w_v", "sb_w_q", "sb_w_o"]
VECTORS = [n for n in WEIGHTS if n in SHARD_AXIS and n not in MATRICES]
REPLICATED = [n for n in WEIGHTS if n not in SHARD_AXIS]
LAYER_AXIS = ("ssm_", "sb_")
PACK_COLS = 1024
N_CHIPS = 4


def _pack(arrays, row_mult, dtype):
    flat = jnp.concatenate([a.reshape(-1).astype(dtype) for a in arrays])
    rows = -(-flat.shape[0] // (PACK_COLS * row_mult)) * row_mult
    return jnp.pad(flat, (0, rows * PACK_COLS - flat.shape[0])).reshape(rows, PACK_COLS)


def _unpack(packed, shapes):
    lead = packed.shape[:-2]
    flat = packed.reshape(*lead, -1)
    out, off = [], 0
    for shp in shapes:
        n = math.prod(shp)
        out.append(flat[..., off:off + n].reshape(*lead, *shp))
        off += n
    return out


def _join(stack, axis):
    return jnp.concatenate([stack[s] for s in range(N_CHIPS)], axis=axis)


def _shards(full, axis):
    n = full.shape[axis] // N_CHIPS
    return [lax.slice_in_dim(full, s * n, (s + 1) * n, axis=axis) for s in range(N_CHIPS)]


def _drop_layer(name, a):
    return a[0] if name.startswith(LAYER_AXIS) else a


def kernel(x, meta_tokens, norm_g, ffn_w1, ffn_w3, ffn_w2, ssm_in_proj, ssm_conv_w, ssm_conv_b, ssm_dt_bias, ssm_a_log, ssm_d, ssm_norm_g, ssm_out_proj, kv_norm_g, w_k, k_norm_g, w_v, sb_w_q, sb_q_norm_g, sb_w_o, loss_target, m_meta_tokens, m_norm_g, m_ffn_w1, m_ffn_w3, m_ffn_w2, m_ssm_in_proj, m_ssm_conv_w, m_ssm_conv_b, m_ssm_dt_bias, m_ssm_a_log, m_ssm_d, m_ssm_norm_g, m_ssm_out_proj, m_kv_norm_g, m_w_k, m_k_norm_g, m_w_v, m_sb_w_q, m_sb_q_norm_g, m_sb_w_o, v_meta_tokens, v_norm_g, v_ffn_w1, v_ffn_w3, v_ffn_w2, v_ssm_in_proj, v_ssm_conv_w, v_ssm_conv_b, v_ssm_dt_bias, v_ssm_a_log, v_ssm_d, v_ssm_norm_g, v_ssm_out_proj, v_kv_norm_g, v_w_k, v_k_norm_g, v_w_v, v_sb_w_q, v_sb_q_norm_g, v_sb_w_o):
    args = locals()
    w_in = {n: args[n] for n in WEIGHTS}
    m_in = {n: args["m_" + n] for n in WEIGHTS}
    v_in = {n: args["v_" + n] for n in WEIGHTS}
    c = lax.axis_index("c")

    def gather(names, dtype, row_mult):
        packed = _pack([w_in[n] for n in names], 2 * row_mult, dtype)
        half = packed.shape[0] // 2
        mine = lax.dynamic_slice_in_dim(packed, c * half, half, axis=0)
        allp = _allgather8(f"gather_{jnp.dtype(dtype).name}", mine).reshape(N_CHIPS, 2 * half, PACK_COLS)
        stacks = _unpack(allp, [w_in[n].shape for n in names])
        return {n: _join(s, SHARD_AXIS[n]) for n, s in zip(names, stacks)}

    full = {**gather(MATRICES, BF16, 16), **gather(VECTORS, F32, 8)}
    full.update({n: w_in[n] for n in REPLICATED})
    full = {n: _drop_layer(n, a) for n, a in full.items()}

    sq, grad_x, grads = _local_step(x[0], loss_target[0], full)
    loss = lax.psum(0.5 / x.shape[-1] * jnp.sum(sq), ("x", "y", "c"))
    grads = {n: (g[None] if n.startswith(LAYER_AXIS) else g) for n, g in grads.items()}

    sharded = MATRICES + VECTORS
    per_chip = [[] for _ in range(N_CHIPS)]
    for n in sharded:
        for p, s in enumerate(_shards(grads[n], SHARD_AXIS[n])):
            per_chip[p].append(s)
    contrib = jnp.stack([_pack(parts, 16, F32) for parts in per_chip])
    rows = contrib.shape[1]
    got = _exchange8("grad_exchange", contrib.reshape(2 * N_CHIPS, rows // 2, PACK_COLS))
    reduced = _pairshare("grad_share", _sum8("grad_sum", got))
    g_out = dict(zip(sharded, _unpack(reduced, [w_in[n].shape for n in sharded])))

    rep = _pack([grads[n] for n in REPLICATED], 8, F32)
    rep_sum = _sum8("rep_sum", _allgather8("rep_gather", rep).reshape(8, rep.shape[0], PACK_COLS))
    g_out.update(zip(REPLICATED, _unpack(rep_sum, [w_in[n].shape for n in REPLICATED])))

    delta, new_m, new_v = {}, {}, {}
    for n in WEIGHTS:
        delta[n], new_m[n], new_v[n] = _adamw(f"adamw_{n}", w_in[n], g_out[n], m_in[n], v_in[n])
    return (loss, grad_x[None], *[g_out[n] for n in WEIGHTS], *[delta[n] for n in WEIGHTS],
            *[new_m[n] for n in WEIGHTS], *[new_v[n] for n in WEIGHTS])
```

```python
import functools
import math

import jax
import jax.numpy as jnp
from jax import lax
from jax.experimental import pallas as pl
from jax.experimental.pallas import tpu as pltpu

F32, BF16 = jnp.float32, jnp.bfloat16
RMS_EPS = 1e-6
LANES = 128
HEAD = 64
D_STATE = 128
BLK = 128
FFN_RES = 0.5
VMEM_LIMIT = 56 * 2 ** 20
ADAM_LR, ADAM_B1, ADAM_B2, ADAM_EPS, ADAM_WD, ADAM_STEP = 0.001, 0.9, 0.999, 1e-08, 0.01, 10
MESH = pl.DeviceIdType.MESH

NN = (((1,), (0,)), ((), ()))
NT = (((1,), (1,)), ((), ()))
TN = (((0,), (0,)), ((), ()))


def _dot(a, b, dn=NN):
    return lax.dot_general(a, b, dn, preferred_element_type=F32)


def _split(x, parts):
    out = []
    for _ in range(parts):
        p = x.astype(BF16)
        out.append(p)
        x = x - p.astype(F32)
    return out


def _dotx(a, b, dn=NN, parts=3, split="a"):
    if split == "a":
        return sum(_dot(p, b, dn) for p in _split(a, parts))
    return sum(_dot(a, p, dn) for p in _split(b, parts))


def _tile(n, target, mult):
    best = None
    for d in range(mult, min(n, target) + 1, mult):
        if n % d == 0:
            best = d
    return n if best is None else best


def _params(*sem):
    return pltpu.CompilerParams(dimension_semantics=tuple(sem) if sem else None, vmem_limit_bytes=VMEM_LIMIT)


def _iota(shape, axis):
    return lax.broadcasted_iota(jnp.int32, shape, axis)


def _sigmoid(x):
    return 1.0 / (1.0 + jnp.exp(-x))


def _matmul(name, pairs, mode, tm, tn, tk, out_dtypes, epilogue=None, extras=(), separate=False):
    a0, b0 = pairs[0]
    if mode == "nn":
        (M, K), N = a0.shape, b0.shape[1]
    elif mode == "nt":
        (M, K), N = a0.shape, b0.shape[0]
    else:
        (K, M), N = a0.shape, b0.shape[1]
    assert M % tm == 0 and N % tn == 0 and K % tk == 0, (name, M, N, K, tm, tn, tk)
    nM, nN, nK = M // tm, N // tn, K // tk
    np_, ne, no = len(pairs), len(extras), len(out_dtypes)
    n_acc = np_ if separate else 1
    dn = {"nn": NN, "nt": NT, "tn": TN}[mode]

    def body(*refs):
        ab, ex = refs[:2 * np_], refs[2 * np_:2 * np_ + ne]
        outs, accs = refs[2 * np_ + ne:2 * np_ + ne + no], refs[2 * np_ + ne + no:]
        k = pl.program_id(2)

        def prod(i):
            return _dot(ab[2 * i][...].astype(BF16), ab[2 * i + 1][...].astype(BF16), dn)

        ps = [prod(i) for i in range(np_)]
        if not separate:
            ps = [functools.reduce(lambda u, v: u + v, ps)]

        def finish(vals):
            res = epilogue(vals, [e[...] for e in ex]) if epilogue is not None else vals
            for o, r in zip(outs, res):
                o[...] = r.astype(o.dtype)

        if nK == 1:
            finish(ps)
        else:
            @pl.when(k == 0)
            def _():
                for acc, p in zip(accs, ps):
                    acc[...] = p

            @pl.when(k > 0)
            def _():
                for acc, p in zip(accs, ps):
                    acc[...] += p

            @pl.when(k == nK - 1)
            def _():
                finish([acc[...] for acc in accs])

    if mode == "tn":
        a_spec = pl.BlockSpec((tk, tm), lambda n, m, k: (k, m))
    else:
        a_spec = pl.BlockSpec((tm, tk), lambda n, m, k: (m, k))
    if mode == "nt":
        b_spec = pl.BlockSpec((tn, tk), lambda n, m, k: (n, k))
    else:
        b_spec = pl.BlockSpec((tk, tn), lambda n, m, k: (k, n))
    in_specs, args = [], []
    for a, b in pairs:
        in_specs += [a_spec, b_spec]
        args += [a, b]
    for arr, kind in extras:
        if kind == "mn":
            in_specs.append(pl.BlockSpec((tm, tn), lambda n, m, k: (m, n)))
        else:
            in_specs.append(pl.BlockSpec((1, tn), lambda n, m, k: (0, n)))
        args.append(arr)
    out_specs = [pl.BlockSpec((tm, tn), lambda n, m, k: (m, n)) for _ in out_dtypes]
    res = pl.pallas_call(
        body, name=name, grid=(nN, nM, nK), in_specs=in_specs, out_specs=out_specs,
        out_shape=[jax.ShapeDtypeStruct((M, N), d) for d in out_dtypes],
        scratch_shapes=[pltpu.VMEM((tm, tn), F32) for _ in range(n_acc)] if nK > 1 else [],
        compiler_params=_params("parallel", "parallel", "arbitrary"),
    )(*args)
    return res


def _rms_fwd(name, h, g):
    L, D = h.shape
    tr = _tile(L, 1024, 16)

    def body(h_ref, g_ref, o_ref):
        x = h_ref[...]
        r = lax.rsqrt(jnp.mean(x * x, axis=-1, keepdims=True) + RMS_EPS)
        o_ref[...] = (x * r * g_ref[...]).astype(BF16)

    return pl.pallas_call(
        body, name=name, grid=(L // tr,),
        in_specs=[pl.BlockSpec((tr, D), lambda i: (i, 0)), pl.BlockSpec((1, D), lambda i: (0, 0))],
        out_specs=pl.BlockSpec((tr, D), lambda i: (i, 0)),
        out_shape=jax.ShapeDtypeStruct((L, D), BF16), compiler_params=_params("parallel"),
    )(h, g.reshape(1, D))


def _rms_bwd(name, dxn, h, g, dres):
    L, D = h.shape
    tr = _tile(L, 512, 8)

    def body(dxn_ref, h_ref, g_ref, dres_ref, dh_ref, dg_ref):
        x = h_ref[...]
        r = lax.rsqrt(jnp.mean(x * x, axis=-1, keepdims=True) + RMS_EPS)
        xh = x * r
        dxn = dxn_ref[...]
        dxh = dxn * g_ref[...]
        dh_ref[...] = dres_ref[...] + r * (dxh - xh * jnp.mean(dxh * xh, axis=-1, keepdims=True))

        @pl.when(pl.program_id(0) == 0)
        def _():
            dg_ref[...] = jnp.zeros_like(dg_ref)

        dg_ref[...] += jnp.sum(dxn * xh, axis=0, keepdims=True)

    row = pl.BlockSpec((tr, D), lambda i: (i, 0))
    vec = pl.BlockSpec((1, D), lambda i: (0, 0))
    return pl.pallas_call(
        body, name=name, grid=(L // tr,), in_specs=[row, row, vec, row], out_specs=[row, vec],
        out_shape=[jax.ShapeDtypeStruct((L, D), F32), jax.ShapeDtypeStruct((1, D), F32)],
        compiler_params=_params("arbitrary"),
    )(dxn, h, g.reshape(1, D), dres)


def _head_sums(x2):
    blockdiag = (_iota((LANES, LANES), 0) // HEAD == _iota((LANES, LANES), 1) // HEAD).astype(BF16)
    cols = [_dotx(x2[:, j:j + LANES], blockdiag) for j in range(0, x2.shape[1], LANES)]
    return jnp.concatenate(cols, axis=1) if len(cols) > 1 else cols[0]


def _headrms_fwd(name, raw, g):
    L, D = raw.shape
    tr = _tile(L, 512, 16)

    def body(x_ref, g_ref, o_ref):
        x = x_ref[...]
        r = lax.rsqrt(_head_sums(x * x) * (1.0 / HEAD) + RMS_EPS)
        o_ref[...] = (x * r * g_ref[...]).astype(BF16)

    return pl.pallas_call(
        body, name=name, grid=(L // tr,),
        in_specs=[pl.BlockSpec((tr, D), lambda i: (i, 0)), pl.BlockSpec((1, D), lambda i: (0, 0))],
        out_specs=pl.BlockSpec((tr, D), lambda i: (i, 0)),
        out_shape=jax.ShapeDtypeStruct((L, D), BF16), compiler_params=_params("parallel"),
    )(raw, g)


def _headrms_bwd(name, dy, raw, g):
    L, D = raw.shape
    tr = _tile(L, 512, 16)

    def body(dy_ref, x_ref, g_ref, dx_ref, dg_ref):
        x = x_ref[...]
        dy = dy_ref[...]
        r = lax.rsqrt(_head_sums(x * x) * (1.0 / HEAD) + RMS_EPS)
        xh = x * r
        dxh = dy * g_ref[...]
        dx_ref[...] = (r * (dxh - xh * (_head_sums(dxh * xh) * (1.0 / HEAD)))).astype(BF16)

        @pl.when(pl.program_id(0) == 0)
        def _():
            dg_ref[...] = jnp.zeros_like(dg_ref)

        dg_ref[...] += jnp.sum(dy * xh, axis=0, keepdims=True)

    row = pl.BlockSpec((tr, D), lambda i: (i, 0))
    vec = pl.BlockSpec((1, D), lambda i: (0, 0))
    return pl.pallas_call(
        body, name=name, grid=(L // tr,), in_specs=[row, row, vec], out_specs=[row, vec],
        out_shape=[jax.ShapeDtypeStruct((L, D), BF16), jax.ShapeDtypeStruct((1, D), F32)],
        compiler_params=_params("arbitrary"),
    )(dy, raw, g)


def _loss(name, h, tgt, pad_rows):
    L, D = h.shape
    nb = L // BLK
    assert pad_rows == BLK

    def body(h_ref, t_ref, dh_ref, s_ref):
        i = pl.program_id(0)

        @pl.when(i == 0)
        def _():
            s_ref[...] = jnp.zeros_like(s_ref)
            dh_ref[...] = jnp.zeros_like(dh_ref)

        @pl.when(i > 0)
        def _():
            e = h_ref[...] - t_ref[...]
            dh_ref[...] = e * (1.0 / D)
            s_ref[...] += jnp.sum(e * e, axis=0, keepdims=True)

    return pl.pallas_call(
        body, name=name, grid=(nb,),
        in_specs=[pl.BlockSpec((BLK, D), lambda i: (i, 0)), pl.BlockSpec((BLK, D), lambda i: (jnp.maximum(i - 1, 0), 0))],
        out_specs=[pl.BlockSpec((BLK, D), lambda i: (i, 0)), pl.BlockSpec((1, D), lambda i: (0, 0))],
        out_shape=[jax.ShapeDtypeStruct((L, D), F32), jax.ShapeDtypeStruct((1, D), F32)],
        compiler_params=_params("arbitrary"),
    )(h, tgt)


def _swiglu_up(name, xn, w1, w3):
    L, D = xn.shape
    Fd = w1.shape[1]
    tm, tn = _tile(L, 704, 16), _tile(Fd, 1408, LANES)

    def epi(accs, _):
        a, b = accs
        return [a, b, a * _sigmoid(a) * b]

    return _matmul(name, [(xn, w1), (xn, w3)], "nn", tm, tn, D, [BF16, BF16, BF16], epilogue=epi, separate=True)


def _swiglu_bwd(name, dh, w2, a, b):
    L, D = dh.shape
    Fd = w2.shape[0]
    tm, tn = _tile(L, 704, 16), _tile(Fd, 1408, LANES)

    def epi(accs, ex):
        dact = accs[0] * FFN_RES
        av, bv = ex[0].astype(F32), ex[1].astype(F32)
        s = _sigmoid(av)
        return [dact * bv * (s * (1.0 + av * (1.0 - s))), dact * av * s]

    return _matmul(name, [(dh, w2)], "nt", tm, tn, D, [BF16, BF16], epilogue=epi, extras=[(a, "mn"), (b, "mn")])


def _ffn_fwd(tag, h, g, w1, w3, w2):
    L, D = h.shape
    xn = _rms_fwd(f"{tag}_norm", h, g)
    a, b, act = _swiglu_up(f"{tag}_up", xn, w1, w3)
    tm = _tile(L, 704, 8)
    (h_out,) = _matmul(f"{tag}_down", [(act, w2)], "nn", tm, D, w2.shape[0], [F32],
                       epilogue=lambda accs, ex: [ex[0] + FFN_RES * accs[0]], extras=[(h, "mn")])
    return h_out, (xn, a, b, act)


def _ffn_bwd(tag, dh_out, h, g, w1, w3, w2, saved):
    xn, a, b, act = saved
    L, D = h.shape
    Fd = w2.shape[0]
    tl = _tile(L, 704, 16)
    da, db = _swiglu_bwd(f"{tag}_dact", dh_out, w2, a, b)
    (dw2,) = _matmul(f"{tag}_dw2", [(act, dh_out)], "tn", _tile(Fd, 1408, LANES), D, tl, [F32],
                     epilogue=lambda accs, ex: [FFN_RES * accs[0]])
    (dw1,) = _matmul(f"{tag}_dw1", [(xn, da)], "tn", D, _tile(Fd, 1408, LANES), tl, [F32])
    (dw3,) = _matmul(f"{tag}_dw3", [(xn, db)], "tn", D, _tile(Fd, 1408, LANES), tl, [F32])
    (dxn,) = _matmul(f"{tag}_dxn", [(da, w1), (db, w3)], "nt", _tile(L, 704, 8), _tile(D, 512, LANES), Fd, [F32])
    dh, dg = _rms_bwd(f"{tag}_dnorm", dxn, h, g, dh_out)
    return dh, dg.reshape(-1), dw1, dw3, dw2


def _conv_taps(ext, k):
    return ext if k == 0 else pltpu.roll(ext, k, axis=0)


def _conv_fwd(name, zx, col0, w, b, pad):
    L = zx.shape[0]
    C = w.shape[1]
    tr, tc = _tile(L, 704, 8), _tile(C, 512, LANES)
    cb = col0 // tc
    assert col0 % tc == 0

    def body(u_ref, halo_ref, w_ref, b_ref, o_ref):
        ext = jnp.concatenate([halo_ref[...], u_ref[...]], axis=0)
        pre = b_ref[...] + sum(_conv_taps(ext, 3 - k)[8:] * w_ref[k:k + 1, :] for k in range(4))
        rows = _iota(pre.shape, 0) + pl.program_id(1) * tr
        o_ref[...] = jnp.where(rows >= pad, pre * _sigmoid(pre), 0.0)

    return pl.pallas_call(
        body, name=name, grid=(C // tc, L // tr),
        in_specs=[pl.BlockSpec((tr, tc), lambda j, i: (i, cb + j)),
                  pl.BlockSpec((8, tc), lambda j, i: (jnp.maximum(i * (tr // 8) - 1, 0), cb + j)),
                  pl.BlockSpec((4, tc), lambda j, i: (0, j)), pl.BlockSpec((1, tc), lambda j, i: (0, j))],
        out_specs=pl.BlockSpec((tr, tc), lambda j, i: (i, j)),
        out_shape=jax.ShapeDtypeStruct((L, C), F32), compiler_params=_params("parallel", "parallel"),
    )(zx, zx, w, b)


def _conv_bwd_pre(name, dact, zx, col0, w, b, pad):
    L = zx.shape[0]
    C = w.shape[1]
    tr, tc = _tile(L, 704, 8), _tile(C, 512, LANES)
    cb = col0 // tc

    def body(d_ref, u_ref, halo_ref, w_ref, b_ref, dp_ref, dw_ref, db_ref):
        ext = jnp.concatenate([halo_ref[...], u_ref[...]], axis=0)
        taps = [_conv_taps(ext, 3 - k)[8:] for k in range(4)]
        pre = b_ref[...] + sum(taps[k] * w_ref[k:k + 1, :] for k in range(4))
        s = _sigmoid(pre)
        rows = _iota(pre.shape, 0) + pl.program_id(1) * tr
        dpre = jnp.where(rows >= pad, d_ref[...] * (s * (1.0 + pre * (1.0 - s))), 0.0)
        dp_ref[...] = dpre

        @pl.when(pl.program_id(1) == 0)
        def _():
            dw_ref[...] = jnp.zeros_like(dw_ref)
            db_ref[...] = jnp.zeros_like(db_ref)

        db_ref[...] += jnp.sum(dpre, axis=0, keepdims=True)
        dw_ref[...] += jnp.concatenate([jnp.sum(dpre * taps[k], axis=0, keepdims=True) for k in range(4)], axis=0)

    return pl.pallas_call(
        body, name=name, grid=(C // tc, L // tr),
        in_specs=[pl.BlockSpec((tr, tc), lambda j, i: (i, j)),
                  pl.BlockSpec((tr, tc), lambda j, i: (i, cb + j)),
                  pl.BlockSpec((8, tc), lambda j, i: (jnp.maximum(i * (tr // 8) - 1, 0), cb + j)),
                  pl.BlockSpec((4, tc), lambda j, i: (0, j)), pl.BlockSpec((1, tc), lambda j, i: (0, j))],
        out_specs=[pl.BlockSpec((tr, tc), lambda j, i: (i, j)), pl.BlockSpec((4, tc), lambda j, i: (0, j)),
                   pl.BlockSpec((1, tc), lambda j, i: (0, j))],
        out_shape=[jax.ShapeDtypeStruct((L, C), F32), jax.ShapeDtypeStruct((4, C), F32), jax.ShapeDtypeStruct((1, C), F32)],
        compiler_params=_params("parallel", "arbitrary"),
    )(dact, zx, zx, w, b)


def _conv_bwd_in(name, dpre, w):
    L, C = dpre.shape
    tr, tc = _tile(L, 704, 16), _tile(C, 512, LANES)
    nr = L // tr

    def body(d_ref, halo_ref, w_ref, o_ref):
        halo = jnp.where(pl.program_id(1) == nr - 1, 0.0, halo_ref[...])
        ext = jnp.concatenate([d_ref[...], halo], axis=0)
        acc = ext[:tr] * w_ref[3:4, :]
        for k in range(3):
            acc = acc + pltpu.roll(ext, tr + 8 - (3 - k), axis=0)[:tr] * w_ref[k:k + 1, :]
        o_ref[...] = acc.astype(BF16)

    return pl.pallas_call(
        body, name=name, grid=(C // tc, nr),
        in_specs=[pl.BlockSpec((tr, tc), lambda j, i: (i, j)),
                  pl.BlockSpec((8, tc), lambda j, i: (jnp.minimum((i + 1) * (tr // 8), L // 8 - 1), j)),
                  pl.BlockSpec((4, tc), lambda j, i: (0, j))],
        out_specs=pl.BlockSpec((tr, tc), lambda j, i: (i, j)),
        out_shape=jax.ShapeDtypeStruct((L, C), BF16), compiler_params=_params("parallel", "parallel"),
    )(dpre, dpre, w)


def _dt_fwd(name, zx, col0, bias_row, nheads, pad):
    L = zx.shape[0]
    cb = col0 // LANES

    def body(x_ref, b_ref, dt_ref, dtt_ref):
        v = x_ref[...] + b_ref[...]
        sp = jnp.maximum(v, 0.0) + jnp.log(1.0 + jnp.exp(-jnp.abs(v)))
        rows = _iota(v.shape, 0) + pl.program_id(0) * BLK
        dt = jnp.where((rows >= pad) & (_iota(v.shape, 1) < nheads), sp, 0.0)
        dt_ref[...] = dt
        dtt_ref[...] = dt.T

    return pl.pallas_call(
        body, name=name, grid=(L // BLK,),
        in_specs=[pl.BlockSpec((BLK, LANES), lambda i: (i, cb)), pl.BlockSpec((1, LANES), lambda i: (0, 0))],
        out_specs=[pl.BlockSpec((BLK, LANES), lambda i: (i, 0)), pl.BlockSpec((LANES, BLK), lambda i: (0, i))],
        out_shape=[jax.ShapeDtypeStruct((L, LANES), F32), jax.ShapeDtypeStruct((LANES, L), F32)],
        compiler_params=_params("parallel"),
    )(zx, bias_row)


def _dt_bwd(name, ddt, zx, col0, bias_row, nheads, pad):
    L = zx.shape[0]
    cb = col0 // LANES

    def body(d_ref, x_ref, b_ref, o_ref, db_ref):
        v = x_ref[...] + b_ref[...]
        rows = _iota(v.shape, 0) + pl.program_id(0) * BLK
        g = jnp.where((rows >= pad) & (_iota(v.shape, 1) < nheads), d_ref[...] * _sigmoid(v), 0.0)
        o_ref[...] = g.astype(BF16)

        @pl.when(pl.program_id(0) == 0)
        def _():
            db_ref[...] = jnp.zeros_like(db_ref)

        db_ref[...] += jnp.sum(g, axis=0, keepdims=True)

    return pl.pallas_call(
        body, name=name, grid=(L // BLK,),
        in_specs=[pl.BlockSpec((BLK, LANES), lambda i: (i, 0)), pl.BlockSpec((BLK, LANES), lambda i: (i, cb)),
                  pl.BlockSpec((1, LANES), lambda i: (0, 0))],
        out_specs=[pl.BlockSpec((BLK, LANES), lambda i: (i, 0)), pl.BlockSpec((1, LANES), lambda i: (0, 0))],
        out_shape=[jax.ShapeDtypeStruct((L, LANES), BF16), jax.ShapeDtypeStruct((1, LANES), F32)],
        compiler_params=_params("arbitrary"),
    )(ddt, zx, bias_row)


def _ssd_common(dt, dtt, a_row, a_col):
    tril = (_iota((BLK, BLK), 0) >= _iota((BLK, BLK), 1)).astype(BF16)
    cum = _dotx(tril, dt * a_row, split="b")
    cumt = _dotx(dtt * a_col, tril, NT)
    return cum, cumt


def _ssd_fwd(name, xbc, dt, dtt, a_row, a_col, expand, di, ng):
    L = xbc.shape[0]
    nc = L // BLK
    hpg = di // HEAD // ng
    gw = hpg * HEAD
    assert gw % LANES == 0

    def body(x_ref, dt_ref, dtt_ref, ar_ref, ac_ref, ex_ref, y_ref, st_ref, h_ref):
        @pl.when(pl.program_id(0) == 0)
        def _():
            h_ref[...] = jnp.zeros_like(h_ref)

        st_ref[0] = h_ref[...]
        dt, dtt = dt_ref[...], dtt_ref[...]
        cum, cumt = _ssd_common(dt, dtt, ar_ref[...], ac_ref[...])
        ex = ex_ref[...]
        ecum_x = _dotx(jnp.exp(cum), ex)
        wend_x = _dotx(jnp.exp(cum[BLK - 1:BLK, :] - cum) * dt, ex)
        ecl = jnp.broadcast_to(jnp.exp(cumt[:, BLK - 1:BLK]), (LANES, LANES))
        decay_h = _dotx(ex, ecl, TN, split="b")
        causal = _iota((BLK, BLK), 0) >= _iota((BLK, BLK), 1)
        low = _iota((BLK, LANES), 1) < HEAD
        for g in range(ng):
            xg = x_ref[:, g * gw:(g + 1) * gw]
            bg = x_ref[:, di + g * D_STATE:di + (g + 1) * D_STATE].astype(BF16)
            cg = x_ref[:, di + (ng + g) * D_STATE:di + (ng + g + 1) * D_STATE].astype(BF16)
            hg = h_ref[g * gw:(g + 1) * gw, :]
            gram = _dot(cg, bg, NT)
            yoff = _dot(cg, hg.astype(BF16), NT) * ecum_x[:, g * gw:(g + 1) * gw]
            parts = []
            for j in range(gw // LANES):
                xp = xg[:, j * LANES:(j + 1) * LANES].astype(BF16)
                yd = []
                for hh in range(2):
                    h = g * hpg + 2 * j + hh
                    seg = cum[:, h:h + 1] - cumt[h:h + 1, :]
                    m = gram * jnp.where(causal, jnp.exp(jnp.minimum(seg, 0.0)), 0.0) * dtt[h:h + 1, :]
                    yd.append(_dot(m.astype(BF16), xp))
                parts.append(jnp.where(low, yd[0], yd[1]))
            y_ref[:, g * gw:(g + 1) * gw] = jnp.concatenate(parts, axis=1) + yoff
            xw = (xg * wend_x[:, g * gw:(g + 1) * gw]).astype(BF16)
            h_ref[g * gw:(g + 1) * gw, :] = hg * decay_h[g * gw:(g + 1) * gw, :] + _dot(xw, bg, TN)

    W = xbc.shape[1]
    full = lambda r, c: pl.BlockSpec((r, c), lambda i: (0, 0))
    return pl.pallas_call(
        body, name=name, grid=(nc,),
        in_specs=[pl.BlockSpec((BLK, W), lambda i: (i, 0)), pl.BlockSpec((BLK, LANES), lambda i: (i, 0)),
                  pl.BlockSpec((LANES, BLK), lambda i: (0, i)), full(1, LANES), full(LANES, LANES), full(LANES, di)],
        out_specs=[pl.BlockSpec((BLK, di), lambda i: (i, 0)), pl.BlockSpec((1, di, D_STATE), lambda i: (i, 0, 0))],
        out_shape=[jax.ShapeDtypeStruct((L, di), F32), jax.ShapeDtypeStruct((nc, di, D_STATE), F32)],
        scratch_shapes=[pltpu.VMEM((di, D_STATE), F32)], compiler_params=_params("arbitrary"),
    )(xbc, dt, dtt, a_row, a_col, expand)


def _ssd_bwd(name, xbc, dt, dtt, a_row, a_col, expand, states, dy, d_x, di, ng):
    L, W = xbc.shape
    nc = L // BLK
    hpg = di // HEAD // ng
    gw = hpg * HEAD

    def body(x_ref, dt_ref, dtt_ref, ar_ref, ac_ref, ex_ref, st_ref, dy_ref, dx_ref_in, dxo_ref, ddt_ref, da_ref, dh_ref):
        @pl.when(pl.program_id(0) == 0)
        def _():
            dh_ref[...] = jnp.zeros_like(dh_ref)
            da_ref[...] = jnp.zeros_like(da_ref)

        dt, dtt, a_row = dt_ref[...], dtt_ref[...], ar_ref[...]
        cum, cumt = _ssd_common(dt, dtt, a_row, ac_ref[...])
        ex = ex_ref[...]
        ecum = jnp.exp(cum)
        ecum_x = _dotx(ecum, ex)
        e_s = jnp.exp(cum[BLK - 1:BLK, :] - cum)
        wend_x = _dotx(e_s * dt, ex)
        ecl_col = jnp.exp(cumt[:, BLK - 1:BLK])
        decay_h = _dotx(ex, jnp.broadcast_to(ecl_col, (LANES, LANES)), TN, split="b")
        causal = _iota((BLK, BLK), 0) >= _iota((BLK, BLK), 1)
        low = _iota((BLK, LANES), 1) < HEAD
        lane = _iota((1, LANES), 1)
        sub = _iota((LANES, 1), 0)
        dcum_c = jnp.zeros((BLK, LANES), F32)
        dcum_r = jnp.zeros((LANES, BLK), F32)
        ddt_r = jnp.zeros((LANES, BLK), F32)
        zoff = []
        dwend_src = []
        for g in range(ng):
            gs = slice(g * gw, (g + 1) * gw)
            xg = x_ref[:, gs]
            bg = x_ref[:, di + g * D_STATE:di + (g + 1) * D_STATE].astype(BF16)
            cg = x_ref[:, di + (ng + g) * D_STATE:di + (ng + g + 1) * D_STATE].astype(BF16)
            hprev = st_ref[0, gs, :]
            dhn = dh_ref[gs, :]
            dyg = dy_ref[:, gs]
            gram = _dot(cg, bg, NT)
            dgram = jnp.zeros((BLK, BLK), F32)
            dxg = []
            for j in range(gw // LANES):
                xp = xg[:, j * LANES:(j + 1) * LANES].astype(BF16)
                dyp = dyg[:, j * LANES:(j + 1) * LANES]
                dxh = []
                for hh in range(2):
                    h = g * hpg + 2 * j + hh
                    seg = cum[:, h:h + 1] - cumt[h:h + 1, :]
                    lm = jnp.where(causal, jnp.exp(jnp.minimum(seg, 0.0)), 0.0)
                    dtr = dtt[h:h + 1, :]
                    m = gram * lm * dtr
                    dym = jnp.where(low if hh == 0 else ~low, dyp, 0.0).astype(BF16)
                    dxh.append(_dot(m.astype(BF16), dym, TN))
                    dm = _dot(dym, xp, NT)
                    dgram = dgram + dm * lm * dtr
                    v = dm * gram * lm
                    wv = v * dtr
                    ddt_r = ddt_r + jnp.where(sub == h, jnp.sum(v, axis=0, keepdims=True), 0.0)
                    dcum_r = dcum_r - jnp.where(sub == h, jnp.sum(wv, axis=0, keepdims=True), 0.0)
                    dcum_c = dcum_c + jnp.where(lane == h, jnp.sum(wv, axis=1, keepdims=True), 0.0)
                dxg.append(jnp.where(low, dxh[0], dxh[1]))
            dx_diag = jnp.concatenate(dxg, axis=1)
            hb = hprev.astype(BF16)
            yoff = _dot(cg, hb, NT) * ecum_x[:, gs]
            dye = (dyg * ecum_x[:, gs]).astype(BF16)
            dcg = _dot(dye, hb) + _dot(dgram.astype(BF16), bg)
            dbg = _dot(dgram.astype(BF16), cg, TN)
            dh_prev = _dot(dye, cg, TN)
            zoff.append(dyg * yoff)
            dhb = dhn.astype(BF16)
            dxw = _dot(bg, dhb, NT)
            xw = (xg * wend_x[:, gs]).astype(BF16)
            dbg = dbg + _dot(xw, dhb)
            dwend_src.append(dxw * xg)
            dxo_ref[:, gs] = dx_diag + dxw * wend_x[:, gs] + dyg * dx_ref_in[:, gs]
            dxo_ref[:, di + g * D_STATE:di + (g + 1) * D_STATE] = dbg
            dxo_ref[:, di + (ng + g) * D_STATE:di + (ng + g + 1) * D_STATE] = dcg
            prod = dhn * hprev
            dd = jnp.sum(_dotx(ex[:, gs], prod, split="b"), axis=1, keepdims=True)
            dcum_r = dcum_r + jnp.where(_iota((1, BLK), 1) == BLK - 1, dd * ecl_col, 0.0)
            dh_ref[gs, :] = dhn * decay_h[gs, :] + dh_prev
        dcum_c = dcum_c + _dotx(jnp.concatenate(zoff, axis=1), ex, NT)
        dwend = _dotx(jnp.concatenate(dwend_src, axis=1), ex, NT)
        ddt_c = dwend * e_s
        de = dwend * dt * e_s
        dcum_c = dcum_c - de + jnp.where(_iota((BLK, 1), 0) == BLK - 1, jnp.sum(de, axis=0, keepdims=True), 0.0)
        dcum = dcum_c + dcum_r.T
        triu = (_iota((BLK, BLK), 0) <= _iota((BLK, BLK), 1)).astype(BF16)
        da = _dotx(triu, dcum, split="b")
        ddt_ref[...] = ddt_c + ddt_r.T + da * a_row
        da_ref[...] += jnp.sum(da * dt, axis=0, keepdims=True)

    rev = lambda i: nc - 1 - i
    full = lambda r, c: pl.BlockSpec((r, c), lambda i: (0, 0))
    return pl.pallas_call(
        body, name=name, grid=(nc,),
        in_specs=[pl.BlockSpec((BLK, W), lambda i: (rev(i), 0)), pl.BlockSpec((BLK, LANES), lambda i: (rev(i), 0)),
                  pl.BlockSpec((LANES, BLK), lambda i: (0, rev(i))), full(1, LANES), full(LANES, LANES), full(LANES, di),
                  pl.BlockSpec((1, di, D_STATE), lambda i: (rev(i), 0, 0)), pl.BlockSpec((BLK, di), lambda i: (rev(i), 0)),
                  full(1, di)],
        out_specs=[pl.BlockSpec((BLK, W), lambda i: (rev(i), 0)), pl.BlockSpec((BLK, LANES), lambda i: (rev(i), 0)),
                   full(1, LANES)],
        out_shape=[jax.ShapeDtypeStruct((L, W), F32), jax.ShapeDtypeStruct((L, LANES), F32),
                   jax.ShapeDtypeStruct((1, LANES), F32)],
        scratch_shapes=[pltpu.VMEM((di, D_STATE), F32)], compiler_params=_params("arbitrary"),
    )(xbc, dt, dtt, a_row, a_col, expand, states, dy, d_x)


def _group_sums(v, gsz):
    cols = []
    for j in range(0, v.shape[1], gsz):
        s = jnp.sum(v[:, j:j + gsz], axis=1, keepdims=True)
        cols.append(jnp.broadcast_to(s, (v.shape[0], gsz)))
    return jnp.concatenate(cols, axis=1)


def _gate_fwd(name, y, xbc, zx, d_x, ng_row, gsz):
    L, di = y.shape
    tr = _tile(L, 512, 16)

    def body(y_ref, x_ref, z_ref, d_ref, g_ref, o_ref):
        z = z_ref[...]
        y2 = (y_ref[...] + d_ref[...] * x_ref[...]) * (z * _sigmoid(z))
        r = lax.rsqrt(_group_sums(y2 * y2, gsz) * (1.0 / gsz) + RMS_EPS)
        o_ref[...] = (y2 * r * g_ref[...]).astype(BF16)

    row = pl.BlockSpec((tr, di), lambda i: (i, 0))
    vec = pl.BlockSpec((1, di), lambda i: (0, 0))
    return pl.pallas_call(
        body, name=name, grid=(L // tr,), in_specs=[row, row, row, vec, vec], out_specs=row,
        out_shape=jax.ShapeDtypeStruct((L, di), BF16), compiler_params=_params("parallel"),
    )(y, xbc, zx, d_x, ng_row)


def _gate_bwd(name, dy3, y, xbc, zx, d_x, ng_row, gsz):
    L, di = y.shape
    tr = _tile(L, 256, 16)

    def body(dy_ref, y_ref, x_ref, z_ref, d_ref, g_ref, dz_ref, dy1_ref, dg_ref, dd_ref):
        z, x = z_ref[...], x_ref[...]
        s = _sigmoid(z)
        sz = z * s
        y1 = y_ref[...] + d_ref[...] * x
        y2 = y1 * sz
        r = lax.rsqrt(_group_sums(y2 * y2, gsz) * (1.0 / gsz) + RMS_EPS)
        yg = y2 * r
        dy3 = dy_ref[...]
        dyg = dy3 * g_ref[...]
        dy2 = r * (dyg - yg * (_group_sums(dyg * yg, gsz) * (1.0 / gsz)))
        dz_ref[...] = (dy2 * y1 * (s * (1.0 + z * (1.0 - s)))).astype(BF16)
        dy1 = dy2 * sz
        dy1_ref[...] = dy1

        @pl.when(pl.program_id(0) == 0)
        def _():
            dg_ref[...] = jnp.zeros_like(dg_ref)
            dd_ref[...] = jnp.zeros_like(dd_ref)

        dg_ref[...] += jnp.sum(dy3 * yg, axis=0, keepdims=True)
        dd_ref[...] += jnp.sum(dy1 * x, axis=0, keepdims=True)

    row = pl.BlockSpec((tr, di), lambda i: (i, 0))
    vec = pl.BlockSpec((1, di), lambda i: (0, 0))
    return pl.pallas_call(
        body, name=name, grid=(L // tr,), in_specs=[row, row, row, row, vec, vec], out_specs=[row, row, vec, vec],
        out_shape=[jax.ShapeDtypeStruct((L, di), BF16), jax.ShapeDtypeStruct((L, di), F32),
                   jax.ShapeDtypeStruct((1, di), F32), jax.ShapeDtypeStruct((1, di), F32)],
        compiler_params=_params("arbitrary"),
    )(dy3, y, xbc, zx, d_x, ng_row)


SB_UNROLL_FWD = 4
SB_UNROLL_BWD = 4


def _sb_heads(x):
    low = _iota(x.shape, 1) < HEAD
    zero = jnp.zeros_like(x)
    return jnp.concatenate([jnp.where(low, x, zero), jnp.where(low, zero, x)], axis=0)


def _sb_unheads(x2):
    return jnp.where(_iota((BLK, LANES), 1) < HEAD, x2[:BLK], x2[BLK:])


def _sb_tiles(qq, kblks, dmat, scol, thrs, s0s):
    n = range(len(kblks))
    z = [_dot(qq, kblks[u], NT) for u in n]
    e = [jnp.exp(-jnp.abs(z[u])) for u in n]
    l1 = [jnp.log(1.0 + e[u]) for u in n]
    valid = [(dmat > thrs[u]) & (scol >= s0s[u]) for u in n]
    lsz = [jnp.minimum(z[u], 0.0) - l1[u] for u in n]
    lkm = [jnp.where(valid[u], -jnp.maximum(z[u], 0.0) - l1[u], 0.0) for u in n]
    return z, e, lsz, lkm, valid


def _sb_fwd(name, q, k, v, pad):
    L, D = q.shape
    nb = L // BLK
    U = SB_UNROLL_FWD

    def body(q_ref, k_ref, v_ref, o_ref):
        i = pl.program_id(1)
        after = (_iota((BLK, BLK), 0) > _iota((BLK, BLK), 1)).astype(BF16)
        qq = _sb_heads(q_ref[...] * 0.125)
        dmat = (_iota((2 * BLK, BLK), 0) & (BLK - 1)) - _iota((2 * BLK, BLK), 1)
        scol = _iota((2 * BLK, BLK), 1)

        def group(j, carry):
            c, acc = carry
            rng = range(U)
            kraw = [i - (j * U + u) for u in rng]
            kb = [jnp.maximum(kraw[u], 0) for u in rng]
            rows = [pl.ds(pl.multiple_of(kb[u] * BLK, BLK), BLK) for u in rng]
            thr = [jnp.where(kraw[u] >= 0, (kb[u] - i) * BLK, BLK) for u in rng]
            _, _, lsz, lkm, valid = _sb_tiles(qq, [k_ref[rows[u], :] for u in rng], dmat, scol, thr,
                                              [pad - kb[u] * BLK for u in rng])
            cum = [_dotx(lkm[u], after, parts=2) for u in rng]
            a = []
            for u in rng:
                a.append(jnp.where(valid[u], jnp.exp(lsz[u] + (c + cum[u])), 0.0).astype(BF16))
                c = c + jnp.sum(lkm[u], axis=1, keepdims=True)
            for u in rng:
                acc = acc + _dot(a[u], v_ref[rows[u], :])
            return c, acc

        _, acc = lax.fori_loop(0, (i + U) // U, group,
                               (jnp.zeros((2 * BLK, 1), F32), jnp.zeros((2 * BLK, LANES), F32)))
        o_ref[...] = _sb_unheads(acc).astype(BF16)

    return pl.pallas_call(
        body, name=name, grid=(D // LANES, nb),
        in_specs=[pl.BlockSpec((BLK, LANES), lambda p, i: (i, p)), pl.BlockSpec((L, LANES), lambda p, i: (0, p)),
                  pl.BlockSpec((L, LANES), lambda p, i: (0, p))],
        out_specs=pl.BlockSpec((BLK, LANES), lambda p, i: (i, p)),
        out_shape=jax.ShapeDtypeStruct((L, D), BF16), compiler_params=_params("parallel", "arbitrary"),
    )(q, k, v)


def _sb_bwd(name, q, k, v, do, pad):
    L, D = q.shape
    nb = L // BLK
    U, UF = SB_UNROLL_BWD, SB_UNROLL_FWD

    def body(q_ref, k_ref, v_ref, do_ref, dq_ref, dk_ref, dv_ref, cs_ref):
        i = pl.program_id(1)

        @pl.when(i == 0)
        def _():
            dk_ref[...] = jnp.zeros_like(dk_ref)
            dv_ref[...] = jnp.zeros_like(dv_ref)

        after = (_iota((BLK, BLK), 0) > _iota((BLK, BLK), 1)).astype(BF16)
        before = (_iota((BLK, BLK), 0) < _iota((BLK, BLK), 1)).astype(BF16)
        qq = _sb_heads(q_ref[...] * 0.125)
        dd = _sb_heads(do_ref[...])
        dmat = (_iota((2 * BLK, BLK), 0) & (BLK - 1)) - _iota((2 * BLK, BLK), 1)
        scol = _iota((2 * BLK, BLK), 1)

        def slot(kb):
            return pl.ds(pl.multiple_of(kb * 2 * BLK, 2 * BLK), 2 * BLK)

        def near_to_far(j, c):
            rng = range(UF)
            kraw = [i - (j * UF + u) for u in rng]
            kb = [jnp.maximum(kraw[u], 0) for u in rng]
            rows = [pl.ds(pl.multiple_of(kb[u] * BLK, BLK), BLK) for u in rng]
            thr = [jnp.where(kraw[u] >= 0, (kb[u] - i) * BLK, BLK) for u in rng]
            lkm = _sb_tiles(qq, [k_ref[rows[u], :] for u in rng], dmat, scol, thr, [pad - kb[u] * BLK for u in rng])[3]
            for u in rng:
                cs_ref[slot(jnp.where(kraw[u] >= 0, kb[u], nb)), :] = jnp.broadcast_to(c, (2 * BLK, LANES))
                c = c + jnp.sum(lkm[u], axis=1, keepdims=True)
            return c

        lax.fori_loop(0, (i + UF) // UF, near_to_far, jnp.zeros((2 * BLK, 1), F32))

        def far_to_near(j, carry):
            p, acc = carry
            rng = range(U)
            kraw = [j * U + u for u in rng]
            kb = [jnp.minimum(kraw[u], i) for u in rng]
            rows = [pl.ds(pl.multiple_of(kb[u] * BLK, BLK), BLK) for u in rng]
            thr = [jnp.where(kraw[u] <= i, (kb[u] - i) * BLK, BLK) for u in rng]
            kblk = [k_ref[rows[u], :] for u in rng]
            z, e, lsz, lkm, valid = _sb_tiles(qq, kblk, dmat, scol, thr, [pad - kb[u] * BLK for u in rng])
            da = [_dot(dd, v_ref[rows[u], :], NT) for u in rng]
            sfx = [cs_ref[slot(kb[u]), :] + _dotx(lkm[u], after, parts=2) for u in rng]
            a = [jnp.where(valid[u], jnp.exp(lsz[u] + sfx[u]), 0.0) for u in rng]
            dlog = [da[u] * a[u] for u in rng]
            cum = [_dotx(dlog[u], before, parts=2) for u in rng]
            dz = []
            for u in rng:
                inv = 1.0 / (1.0 + e[u])
                sig = jnp.where(z[u] >= 0.0, inv, e[u] * inv)
                dz.append((dlog[u] * (1.0 - sig) - jnp.where(valid[u], (p + cum[u]) * sig, 0.0)).astype(BF16))
                p = p + jnp.sum(dlog[u], axis=1, keepdims=True)
            for u in rng:
                dk_ref[rows[u], :] += _dot(dz[u], qq, TN)
                dv_ref[rows[u], :] += _dot(a[u].astype(BF16), dd, TN)
                acc = acc + _dot(dz[u], kblk[u])
            return p, acc

        _, acc = lax.fori_loop(0, (i + U) // U, far_to_near,
                               (jnp.zeros((2 * BLK, 1), F32), jnp.zeros((2 * BLK, LANES), F32)))
        dq_ref[...] = _sb_unheads(acc) * 0.125

    blk = pl.BlockSpec((BLK, LANES), lambda p, i: (i, p))
    col = pl.BlockSpec((L, LANES), lambda p, i: (0, p))
    return pl.pallas_call(
        body, name=name, grid=(D // LANES, nb), in_specs=[blk, col, col, blk], out_specs=[blk, col, col],
        out_shape=[jax.ShapeDtypeStruct((L, D), F32)] * 3,
        scratch_shapes=[pltpu.VMEM((2 * (L + BLK), LANES), F32)], compiler_params=_params("parallel", "arbitrary"),
    )(q, k, v, do)


def _local_step(x, tgt, w):
    S, D = x.shape
    nm = w["meta_tokens"].shape[0]
    pad = BLK - nm
    L = pad + nm + S
    assert L % BLK == 0 and 0 < nm <= BLK
    di = w["ssm_out_proj"].shape[0]
    nh = w["ssm_dt_bias"].shape[0]
    assert di == nh * HEAD and nh <= LANES
    conv_dim = w["ssm_conv_w"].shape[1]
    ng = (conv_dim - di) // (2 * D_STATE)
    zp = di + conv_dim + LANES
    g = w["norm_g"]
    grads = {}

    h0 = jnp.concatenate([jnp.zeros((pad, D), F32), w["meta_tokens"], x], axis=0)

    h1, s1 = _ffn_fwd("f00", h0, g[0, 0], w["ffn_w1"][0, 0], w["ffn_w3"][0, 0], w["ffn_w2"][0, 0])
    u0 = _rms_fwd("m_norm", h1, g[0, 1])
    w_in = jnp.concatenate([w["ssm_in_proj"], jnp.zeros((D, zp - w["ssm_in_proj"].shape[1]), BF16)], axis=1)
    tl = _tile(L, 704, 16)
    (zx,) = _matmul("m_inproj", [(u0, w_in)], "nn", tl, _tile(zp, 1024, LANES), D, [F32])
    conv_b = w["ssm_conv_b"].reshape(1, conv_dim)
    xbc = _conv_fwd("m_conv", zx, di, w["ssm_conv_w"], conv_b, pad)
    bias_row = jnp.zeros((1, LANES), F32).at[0, :nh].set(w["ssm_dt_bias"])
    dt, dtt = _dt_fwd("m_dt", zx, di + conv_dim, bias_row, nh, pad)
    a_neg = -jnp.exp(w["ssm_a_log"])
    a_row = jnp.zeros((1, LANES), F32).at[0, :nh].set(a_neg)
    a_col = jnp.broadcast_to(a_row.reshape(LANES, 1), (LANES, LANES))
    expand = (jnp.arange(LANES)[:, None] == (jnp.arange(di) // HEAD)[None, :]).astype(BF16)
    y_ssd, states = _ssd_fwd("m_ssd", xbc, dt, dtt, a_row, a_col, expand, di, ng)
    d_x = jnp.repeat(w["ssm_d"], HEAD).reshape(1, di)
    ssm_g = w["ssm_norm_g"].reshape(1, di)
    gsz = di // ng
    y3 = _gate_fwd("m_gate", y_ssd, xbc, zx, d_x, ssm_g, gsz)
    (h2,) = _matmul("m_outproj", [(y3, w["ssm_out_proj"])], "nn", tl, D, di, [F32],
                    epilogue=lambda accs, ex: [ex[0] + accs[0]], extras=[(h1, "mn")])
    h3, s2 = _ffn_fwd("f01", h2, g[0, 2], w["ffn_w1"][0, 1], w["ffn_w3"][0, 1], w["ffn_w2"][0, 1])

    kv_in = _rms_fwd("kv_norm", h3, w["kv_norm_g"])
    (k_raw,) = _matmul("kv_k", [(kv_in, w["w_k"])], "nn", tl, D, D, [F32])
    (v_sh,) = _matmul("kv_v", [(kv_in, w["w_v"])], "nn", tl, D, D, [BF16])
    kg = jnp.tile(w["k_norm_g"], D // HEAD).reshape(1, D)
    k_sh = _headrms_fwd("kv_knorm", k_raw, kg)

    h4, s3 = _ffn_fwd("f10", h3, g[1, 0], w["ffn_w1"][1, 0], w["ffn_w3"][1, 0], w["ffn_w2"][1, 0])
    u1 = _rms_fwd("a_norm", h4, g[1, 1])
    (q_raw,) = _matmul("a_q", [(u1, w["sb_w_q"])], "nn", tl, D, D, [F32])
    qg = jnp.tile(w["sb_q_norm_g"], D // HEAD).reshape(1, D)
    q = _headrms_fwd("a_qnorm", q_raw, qg)
    o = _sb_fwd("a_attn", q, k_sh, v_sh, pad)
    (h5,) = _matmul("a_o", [(o, w["sb_w_o"])], "nn", tl, D, D, [F32],
                    epilogue=lambda accs, ex: [ex[0] + accs[0]], extras=[(h4, "mn")])
    h6, s4 = _ffn_fwd("f11", h5, g[1, 2], w["ffn_w1"][1, 1], w["ffn_w3"][1, 1], w["ffn_w2"][1, 1])

    dh6, sq = _loss("loss", h6, tgt, pad + nm)

    dg = jnp.zeros_like(g)
    dw1 = [[None, None], [None, None]]
    dw3 = [[None, None], [None, None]]
    dw2 = [[None, None], [None, None]]
    dh5, dgv, dw1[1][1], dw3[1][1], dw2[1][1] = _ffn_bwd("b11", dh6, h5, g[1, 2], w["ffn_w1"][1, 1], w["ffn_w3"][1, 1],
                                                           w["ffn_w2"][1, 1], s4)
    dg = dg.at[1, 2].set(dgv)
    td = _tile(D, 512, LANES)
    (do,) = _matmul("b_do", [(dh5, w["sb_w_o"])], "nt", tl, D, D, [BF16])
    (grads["sb_w_o"],) = _matmul("b_dwo", [(o, dh5)], "tn", D, td, tl, [F32])
    dq, dk, dv = _sb_bwd("b_attn", q, k_sh, v_sh, do, pad)
    dq_raw, dqg = _headrms_bwd("b_qnorm", dq, q_raw, qg)
    grads["sb_q_norm_g"] = dqg.reshape(D // HEAD, HEAD).sum(0)
    (grads["sb_w_q"],) = _matmul("b_dwq", [(u1, dq_raw)], "tn", D, td, tl, [F32])
    (du1,) = _matmul("b_du1", [(dq_raw, w["sb_w_q"])], "nt", tl, D, D, [F32])
    dh4, dgv = _rms_bwd("b_anorm", du1, h4, g[1, 1], dh5)
    dg = dg.at[1, 1].set(dgv.reshape(-1))
    dh3, dgv, dw1[1][0], dw3[1][0], dw2[1][0] = _ffn_bwd("b10", dh4, h3, g[1, 0], w["ffn_w1"][1, 0], w["ffn_w3"][1, 0],
                                                           w["ffn_w2"][1, 0], s3)
    dg = dg.at[1, 0].set(dgv)

    dk_raw, dkg = _headrms_bwd("b_knorm", dk, k_raw, kg)
    grads["k_norm_g"] = dkg.reshape(D // HEAD, HEAD).sum(0)
    (grads["w_k"],) = _matmul("b_dwk", [(kv_in, dk_raw)], "tn", D, td, tl, [F32])
    (grads["w_v"],) = _matmul("b_dwv", [(kv_in, dv)], "tn", D, td, tl, [F32])
    (dkv_in,) = _matmul("b_dkvin", [(dk_raw, w["w_k"]), (dv, w["w_v"])], "nt", tl, D, D, [F32])
    dh3, dgv = _rms_bwd("b_kvnorm", dkv_in, h3, w["kv_norm_g"], dh3)
    grads["kv_norm_g"] = dgv.reshape(-1)

    dh2, dgv, dw1[0][1], dw3[0][1], dw2[0][1] = _ffn_bwd("b01", dh3, h2, g[0, 2], w["ffn_w1"][0, 1], w["ffn_w3"][0, 1],
                                                           w["ffn_w2"][0, 1], s2)
    dg = dg.at[0, 2].set(dgv)
    (dy3,) = _matmul("b_dy3", [(dh2, w["ssm_out_proj"])], "nt", tl, _tile(di, 1024, LANES), D, [F32])
    (grads["ssm_out_proj"],) = _matmul("b_dwout", [(y3, dh2)], "tn", _tile(di, 1024, LANES), D, tl, [F32])
    dz, dy1, dssm_g, dd_x = _gate_bwd("b_gate", dy3, y_ssd, xbc, zx, d_x, ssm_g, gsz)
    grads["ssm_norm_g"] = dssm_g.reshape(-1)
    grads["ssm_d"] = dd_x.reshape(nh, HEAD).sum(1)
    dxbc, ddt, da = _ssd_bwd("b_ssd", xbc, dt, dtt, a_row, a_col, expand, states, dy1, d_x, di, ng)
    grads["ssm_a_log"] = da[0, :nh] * a_neg
    ddt_raw, dbias = _dt_bwd("b_dt", ddt, zx, di + conv_dim, bias_row, nh, pad)
    grads["ssm_dt_bias"] = dbias[0, :nh]
    dpre, grads["ssm_conv_w"], dconv_b = _conv_bwd_pre("b_convpre", dxbc, zx, di, w["ssm_conv_w"], conv_b, pad)
    grads["ssm_conv_b"] = dconv_b.reshape(-1)
    dxbc_raw = _conv_bwd_in("b_convin", dpre, w["ssm_conv_w"])
    dzx = jnp.concatenate([dz, dxbc_raw, ddt_raw], axis=1)
    (dw_in,) = _matmul("b_dwin", [(u0, dzx)], "tn", D, _tile(zp, 1024, LANES), tl, [F32])
    grads["ssm_in_proj"] = dw_in[:, :w["ssm_in_proj"].shape[1]]
    (du0,) = _matmul("b_du0", [(dzx, w_in)], "nt", _tile(L, 352, 16), td, zp, [F32])
    dh1, dgv = _rms_bwd("b_mnorm", du0, h1, g[0, 1], dh2)
    dg = dg.at[0, 1].set(dgv.reshape(-1))
    dh0, dgv, dw1[0][0], dw3[0][0], dw2[0][0] = _ffn_bwd("b00", dh1, h0, g[0, 0], w["ffn_w1"][0, 0], w["ffn_w3"][0, 0],
                                                           w["ffn_w2"][0, 0], s1)
    dg = dg.at[0, 0].set(dgv)

    grads["norm_g"] = dg
    grads["ffn_w1"] = jnp.stack([jnp.stack(r) for r in dw1])
    grads["ffn_w3"] = jnp.stack([jnp.stack(r) for r in dw3])
    grads["ffn_w2"] = jnp.stack([jnp.stack(r) for r in dw2])
    grads["meta_tokens"] = dh0[pad:pad + nm]
    return sq, dh0[pad + nm:], grads


HBM_SPEC = pl.BlockSpec(memory_space=pltpu.HBM)


def _place():
    return lax.axis_index("x"), lax.axis_index("y"), lax.axis_index("c")


def _allgather8(name, blk):
    m, n = blk.shape

    def body(x_ref, out_ref, send_sems, recv_sems, local_sem):
        x, y, c = _place()
        me, sibling = (x, y, c), (x, y, 1 - c)
        chips = [(1 - x, y), (x, 1 - y), (1 - x, 1 - y)]

        def rows(px, py, pc):
            return out_ref.at[pl.ds((4 * px + 2 * py + pc) * m, m), :]

        def copy(k, block, to, src=None):
            return pltpu.make_async_remote_copy(
                src_ref=rows(*block) if src is None else src, dst_ref=rows(*block),
                send_sem=send_sems.at[k], recv_sem=recv_sems.at[k], device_id=to, device_id_type=MESH)

        mine = pltpu.make_async_copy(x_ref, rows(*me), local_sem)
        mine.start()
        first = [copy(0, me, sibling, src=x_ref)]
        first += [copy(1 + j, me, (*chip, c), src=x_ref) for j, chip in enumerate(chips)]
        for cp in first:
            cp.start()
        passed = [copy(4 + j, (*chip, c), sibling) for j, chip in enumerate(chips)]
        for j, chip in enumerate(chips):
            copy(1 + j, (*chip, c), me).wait_recv()
            passed[j].start()
        copy(0, sibling, me).wait_recv()
        for j, chip in enumerate(chips):
            copy(4 + j, (*chip, 1 - c), me).wait_recv()
        for cp in first + passed:
            cp.wait_send()
        mine.wait()

    return pl.pallas_call(
        body, name=name, out_shape=jax.ShapeDtypeStruct((8 * m, n), blk.dtype),
        in_specs=[HBM_SPEC], out_specs=HBM_SPEC,
        scratch_shapes=[pltpu.SemaphoreType.DMA((7,)), pltpu.SemaphoreType.DMA((7,)), pltpu.SemaphoreType.DMA],
    )(blk)


def _exchange8(name, g):
    _, m, n = g.shape

    def body(g_ref, out_ref, send_sems, recv_sems, local_sem):
        x, y, c = _place()
        me_id = 4 * x + 2 * y + c
        mine = pltpu.make_async_copy(g_ref.at[me_id], out_ref.at[me_id], local_sem)
        mine.start()
        sends, recvs = [], []
        for k in range(1, 8):
            px = 1 - x if k & 4 else x
            py = 1 - y if k & 2 else y
            pc = 1 - c if k & 1 else c
            pid = 4 * px + 2 * py + pc
            sends.append(pltpu.make_async_remote_copy(
                src_ref=g_ref.at[pid], dst_ref=out_ref.at[me_id], send_sem=send_sems.at[k - 1],
                recv_sem=recv_sems.at[k - 1], device_id=(px, py, pc), device_id_type=MESH))
            recvs.append(pltpu.make_async_remote_copy(
                src_ref=g_ref.at[me_id], dst_ref=out_ref.at[pid], send_sem=send_sems.at[k - 1],
                recv_sem=recv_sems.at[k - 1], device_id=(px, py, pc), device_id_type=MESH))
        for cp in sends:
            cp.start()
        for cp in recvs:
            cp.wait_recv()
        for cp in sends:
            cp.wait_send()
        mine.wait()

    return pl.pallas_call(
        body, name=name, out_shape=jax.ShapeDtypeStruct(g.shape, g.dtype), in_specs=[HBM_SPEC], out_specs=HBM_SPEC,
        scratch_shapes=[pltpu.SemaphoreType.DMA((7,)), pltpu.SemaphoreType.DMA((7,)), pltpu.SemaphoreType.DMA],
    )(g)


def _pairshare(name, half):
    m, n = half.shape

    def body(x_ref, out_ref, send_sem, recv_sem, local_sem):
        x, y, c = _place()
        mine = pltpu.make_async_copy(x_ref, out_ref.at[pl.ds(c * m, m), :], local_sem)
        mine.start()
        send = pltpu.make_async_remote_copy(
            src_ref=x_ref, dst_ref=out_ref.at[pl.ds(c * m, m), :], send_sem=send_sem, recv_sem=recv_sem,
            device_id=(x, y, 1 - c), device_id_type=MESH)
        send.start()
        pltpu.make_async_remote_copy(
            src_ref=x_ref, dst_ref=out_ref.at[pl.ds((1 - c) * m, m), :], send_sem=send_sem, recv_sem=recv_sem,
            device_id=(x, y, 1 - c), device_id_type=MESH).wait_recv()
        send.wait_send()
        mine.wait()

    return pl.pallas_call(
        body, name=name, out_shape=jax.ShapeDtypeStruct((2 * m, n), half.dtype), in_specs=[HBM_SPEC], out_specs=HBM_SPEC,
        scratch_shapes=[pltpu.SemaphoreType.DMA, pltpu.SemaphoreType.DMA, pltpu.SemaphoreType.DMA],
    )(half)


def _sum8(name, parts):
    _, m, n = parts.shape
    tr = _tile(m, 256, 8)

    def body(p_ref, o_ref):
        acc = p_ref[0]
        for s in range(1, 8):
            acc = acc + p_ref[s]
        o_ref[...] = acc

    return pl.pallas_call(
        body, name=name, grid=(m // tr,), in_specs=[pl.BlockSpec((8, tr, n), lambda i: (0, i, 0))],
        out_specs=pl.BlockSpec((tr, n), lambda i: (i, 0)), out_shape=jax.ShapeDtypeStruct((m, n), F32),
        compiler_params=_params("parallel"),
    )(parts)


def _adamw(name, w, g, m, v):
    shape = w.shape
    cols = shape[-1]
    rows = math.prod(shape[:-1])
    tr = _tile(rows, 512, 8) if rows * cols > 2 ** 19 else rows

    def body(w_ref, g_ref, m_ref, v_ref, d_ref, mo_ref, vo_ref):
        gv = g_ref[...]
        m2 = ADAM_B1 * m_ref[...] + (1.0 - ADAM_B1) * gv
        v2 = ADAM_B2 * v_ref[...] + (1.0 - ADAM_B2) * (gv * gv)
        m_hat = m2 / (1.0 - ADAM_B1 ** ADAM_STEP)
        v_hat = v2 / (1.0 - ADAM_B2 ** ADAM_STEP)
        d_ref[...] = -ADAM_LR * (m_hat / (jnp.sqrt(v_hat) + ADAM_EPS) + ADAM_WD * w_ref[...])
        mo_ref[...] = m2
        vo_ref[...] = v2

    spec = pl.BlockSpec((tr, cols), lambda i: (i, 0))
    outs = pl.pallas_call(
        body, name=name, grid=(rows // tr,), in_specs=[spec] * 4, out_specs=[spec] * 3,
        out_shape=[jax.ShapeDtypeStruct((rows, cols), F32)] * 3, compiler_params=_params("parallel"),
    )(*(a.reshape(rows, cols) for a in (w, g, m, v)))
    return tuple(o.reshape(shape) for o in outs)


WEIGHTS = ["meta_tokens", "norm_g", "ffn_w1", "ffn_w3", "ffn_w2", "ssm_in_proj", "ssm_conv_w", "ssm_conv_b", "ssm_dt_bias",
           "ssm_a_log", "ssm_d", "ssm_norm_g", "ssm_out_proj", "kv_norm_g", "w_k", "k_norm_g", "w_v", "sb_w_q",
           "sb_q_norm_g", "sb_w_o"]
SHARD_AXIS = {"meta_tokens": 1, "norm_g": 2, "ffn_w1": 3, "ffn_w3": 3, "ffn_w2": 2, "ssm_in_proj": 2, "ssm_conv_w": 2,
              "ssm_conv_b": 1, "ssm_norm_g": 1, "ssm_out_proj": 1, "w_k": 0, "w_v": 0, "sb_w_q": 1, "sb_w_o": 1}
MATRICES = ["ffn_w1", "ffn_w3", "ffn_w2", "ssm_in_proj", "ssm_out_proj", "w_k", "w_v", "sb_w_q", "sb_w_o"]
VECTORS = [n for n in WEIGHTS if n in SHARD_AXIS and n not in MATRICES]
REPLICATED = [n for n in WEIGHTS if n not in SHARD_AXIS]
LAYER_AXIS = ("ssm_", "sb_")
PACK_COLS = 1024
N_CHIPS = 4


def _pack(arrays, row_mult, dtype):
    segs = []
    for a in arrays:
        n = math.prod(a.shape)
        r = -(-n // PACK_COLS)
        flat = a.reshape(-1).astype(dtype)
        if r * PACK_COLS != n:
            flat = jnp.pad(flat, (0, r * PACK_COLS - n))
        segs.append(flat.reshape(r, PACK_COLS))
    rows = sum(s.shape[0] for s in segs)
    extra = -rows % row_mult
    if extra:
        segs.append(jnp.zeros((extra, PACK_COLS), dtype))
    return jnp.concatenate(segs, axis=0)


def _unpack(packed, shapes):
    lead = packed.shape[:-2]
    out, r0 = [], 0
    for shp in shapes:
        n = math.prod(shp)
        r = -(-n // PACK_COLS)
        seg = packed[..., r0:r0 + r, :]
        if r * PACK_COLS != n:
            seg = seg.reshape(*lead, r * PACK_COLS)[..., :n]
        out.append(seg.reshape(*lead, *shp))
        r0 += r
    return out


def _join(stack, axis):
    return jnp.concatenate([stack[s] for s in range(N_CHIPS)], axis=axis)


def _shards(full, axis):
    n = full.shape[axis] // N_CHIPS
    return [lax.slice_in_dim(full, s * n, (s + 1) * n, axis=axis) for s in range(N_CHIPS)]


def _drop_layer(name, a):
    return a[0] if name.startswith(LAYER_AXIS) else a


def kernel(x, meta_tokens, norm_g, ffn_w1, ffn_w3, ffn_w2, ssm_in_proj, ssm_conv_w, ssm_conv_b, ssm_dt_bias, ssm_a_log, ssm_d, ssm_norm_g, ssm_out_proj, kv_norm_g, w_k, k_norm_g, w_v, sb_w_q, sb_q_norm_g, sb_w_o, loss_target, m_meta_tokens, m_norm_g, m_ffn_w1, m_ffn_w3, m_ffn_w2, m_ssm_in_proj, m_ssm_conv_w, m_ssm_conv_b, m_ssm_dt_bias, m_ssm_a_log, m_ssm_d, m_ssm_norm_g, m_ssm_out_proj, m_kv_norm_g, m_w_k, m_k_norm_g, m_w_v, m_sb_w_q, m_sb_q_norm_g, m_sb_w_o, v_meta_tokens, v_norm_g, v_ffn_w1, v_ffn_w3, v_ffn_w2, v_ssm_in_proj, v_ssm_conv_w, v_ssm_conv_b, v_ssm_dt_bias, v_ssm_a_log, v_ssm_d, v_ssm_norm_g, v_ssm_out_proj, v_kv_norm_g, v_w_k, v_k_norm_g, v_w_v, v_sb_w_q, v_sb_q_norm_g, v_sb_w_o):
    args = locals()
    w_in = {n: args[n] for n in WEIGHTS}
    m_in = {n: args["m_" + n] for n in WEIGHTS}
    v_in = {n: args["v_" + n] for n in WEIGHTS}
    c = lax.axis_index("c")

    def gather(names, dtype, row_mult):
        packed = _pack([w_in[n] for n in names], 2 * row_mult, dtype)
        half = packed.shape[0] // 2
        mine = lax.dynamic_slice_in_dim(packed, c * half, half, axis=0)
        allp = _allgather8(f"gather_{jnp.dtype(dtype).name}", mine).reshape(N_CHIPS, 2 * half, PACK_COLS)
        stacks = _unpack(allp, [w_in[n].shape for n in names])
        return {n: _join(s, SHARD_AXIS[n]) for n, s in zip(names, stacks)}

    full = {**gather(MATRICES, BF16, 16), **gather(VECTORS, F32, 8)}
    full.update({n: w_in[n] for n in REPLICATED})
    full = {n: _drop_layer(n, a) for n, a in full.items()}

    sq, grad_x, grads = _local_step(x[0], loss_target[0], full)
    loss = lax.psum(0.5 / x.shape[-1] * jnp.sum(sq), ("x", "y", "c"))
    grads = {n: (g[None] if n.startswith(LAYER_AXIS) else g) for n, g in grads.items()}

    sharded = MATRICES + VECTORS
    per_chip = [[] for _ in range(N_CHIPS)]
    for n in sharded:
        for p, s in enumerate(_shards(grads[n], SHARD_AXIS[n])):
            per_chip[p].append(s)
    contrib = jnp.stack([_pack(parts, 16, F32) for parts in per_chip])
    rows = contrib.shape[1]
    got = _exchange8("grad_exchange", contrib.reshape(2 * N_CHIPS, rows // 2, PACK_COLS))
    reduced = _pairshare("grad_share", _sum8("grad_sum", got))
    g_out = dict(zip(sharded, _unpack(reduced, [w_in[n].shape for n in sharded])))

    rep = _pack([grads[n] for n in REPLICATED], 8, F32)
    rep_sum = _sum8("rep_sum", _allgather8("rep_gather", rep).reshape(8, rep.shape[0], PACK_COLS))
    g_out.update(zip(REPLICATED, _unpack(rep_sum, [w_in[n].shape for n in REPLICATED])))

    delta, new_m, new_v = {}, {}, {}
    for n in WEIGHTS:
        delta[n], new_m[n], new_v[n] = _adamw(f"adamw_{n}", w_in[n], g_out[n], m_in[n], v_in[n])
    return (loss, grad_x[None], *[g_out[n] for n in WEIGHTS], *[delta[n] for n in WEIGHTS],
            *[new_m[n] for n in WEIGHTS], *[new_v[n] for n in WEIGHTS])
```

```python
import functools
import math

import jax
import jax.numpy as jnp
from jax import lax
from jax.experimental import pallas as pl
from jax.experimental.pallas import tpu as pltpu

F32, BF16 = jnp.float32, jnp.bfloat16
RMS_EPS = 1e-6
LANES = 128
HEAD = 64
D_STATE = 128
BLK = 128
FFN_RES = 0.5
VMEM_LIMIT = 56 * 2 ** 20
ADAM_LR, ADAM_B1, ADAM_B2, ADAM_EPS, ADAM_WD, ADAM_STEP = 0.001, 0.9, 0.999, 1e-08, 0.01, 10
MESH = pl.DeviceIdType.MESH

NN = (((1,), (0,)), ((), ()))
NT = (((1,), (1,)), ((), ()))
TN = (((0,), (0,)), ((), ()))


def _dot(a, b, dn=NN):
    return lax.dot_general(a, b, dn, preferred_element_type=F32)


def _split(x, parts):
    out = []
    for _ in range(parts):
        p = x.astype(BF16)
        out.append(p)
        x = x - p.astype(F32)
    return out


def _dotx(a, b, dn=NN, parts=3, split="a"):
    if split == "a":
        return sum(_dot(p, b, dn) for p in _split(a, parts))
    return sum(_dot(a, p, dn) for p in _split(b, parts))


def _tile(n, target, mult):
    best = None
    for d in range(mult, min(n, target) + 1, mult):
        if n % d == 0:
            best = d
    return n if best is None else best


def _params(*sem):
    return pltpu.CompilerParams(dimension_semantics=tuple(sem) if sem else None, vmem_limit_bytes=VMEM_LIMIT)


def _iota(shape, axis):
    return lax.broadcasted_iota(jnp.int32, shape, axis)


def _sigmoid(x):
    return 1.0 / (1.0 + jnp.exp(-x))


def _matmul(name, pairs, mode, tm, tn, tk, out_dtypes, epilogue=None, extras=(), separate=False):
    a0, b0 = pairs[0]
    if mode == "nn":
        (M, K), N = a0.shape, b0.shape[1]
    elif mode == "nt":
        (M, K), N = a0.shape, b0.shape[0]
    else:
        (K, M), N = a0.shape, b0.shape[1]
    assert M % tm == 0 and N % tn == 0 and K % tk == 0, (name, M, N, K, tm, tn, tk)
    nM, nN, nK = M // tm, N // tn, K // tk
    np_, ne, no = len(pairs), len(extras), len(out_dtypes)
    n_acc = np_ if separate else 1
    dn = {"nn": NN, "nt": NT, "tn": TN}[mode]

    def body(*refs):
        ab, ex = refs[:2 * np_], refs[2 * np_:2 * np_ + ne]
        outs, accs = refs[2 * np_ + ne:2 * np_ + ne + no], refs[2 * np_ + ne + no:]
        k = pl.program_id(2)

        def prod(i):
            return _dot(ab[2 * i][...].astype(BF16), ab[2 * i + 1][...].astype(BF16), dn)

        ps = [prod(i) for i in range(np_)]
        if not separate:
            ps = [functools.reduce(lambda u, v: u + v, ps)]

        def finish(vals):
            res = epilogue(vals, [e[...] for e in ex]) if epilogue is not None else vals
            for o, r in zip(outs, res):
                o[...] = r.astype(o.dtype)

        if nK == 1:
            finish(ps)
        else:
            @pl.when(k == 0)
            def _():
                for acc, p in zip(accs, ps):
                    acc[...] = p

            @pl.when(k > 0)
            def _():
                for acc, p in zip(accs, ps):
                    acc[...] += p

            @pl.when(k == nK - 1)
            def _():
                finish([acc[...] for acc in accs])

    if mode == "tn":
        a_spec = pl.BlockSpec((tk, tm), lambda n, m, k: (k, m))
    else:
        a_spec = pl.BlockSpec((tm, tk), lambda n, m, k: (m, k))
    if mode == "nt":
        b_spec = pl.BlockSpec((tn, tk), lambda n, m, k: (n, k))
    else:
        b_spec = pl.BlockSpec((tk, tn), lambda n, m, k: (k, n))
    in_specs, args = [], []
    for a, b in pairs:
        in_specs += [a_spec, b_spec]
        args += [a, b]
    for arr, kind in extras:
        if kind == "mn":
            in_specs.append(pl.BlockSpec((tm, tn), lambda n, m, k: (m, n)))
        else:
            in_specs.append(pl.BlockSpec((1, tn), lambda n, m, k: (0, n)))
        args.append(arr)
    out_specs = [pl.BlockSpec((tm, tn), lambda n, m, k: (m, n)) for _ in out_dtypes]
    res = pl.pallas_call(
        body, name=name, grid=(nN, nM, nK), in_specs=in_specs, out_specs=out_specs,
        out_shape=[jax.ShapeDtypeStruct((M, N), d) for d in out_dtypes],
        scratch_shapes=[pltpu.VMEM((tm, tn), F32) for _ in range(n_acc)] if nK > 1 else [],
        compiler_params=_params("parallel", "parallel", "arbitrary"),
    )(*args)
    return res


def _rms_fwd(name, h, g):
    L, D = h.shape
    tr = _tile(L, 1024, 16)

    def body(h_ref, g_ref, o_ref):
        x = h_ref[...]
        r = lax.rsqrt(jnp.mean(x * x, axis=-1, keepdims=True) + RMS_EPS)
        o_ref[...] = (x * r * g_ref[...]).astype(BF16)

    return pl.pallas_call(
        body, name=name, grid=(L // tr,),
        in_specs=[pl.BlockSpec((tr, D), lambda i: (i, 0)), pl.BlockSpec((1, D), lambda i: (0, 0))],
        out_specs=pl.BlockSpec((tr, D), lambda i: (i, 0)),
        out_shape=jax.ShapeDtypeStruct((L, D), BF16), compiler_params=_params("parallel"),
    )(h, g.reshape(1, D))


def _rms_bwd(name, dxn, h, g, dres):
    L, D = h.shape
    tr = _tile(L, 512, 8)

    def body(dxn_ref, h_ref, g_ref, dres_ref, dh_ref, dg_ref):
        x = h_ref[...]
        r = lax.rsqrt(jnp.mean(x * x, axis=-1, keepdims=True) + RMS_EPS)
        xh = x * r
        dxn = dxn_ref[...]
        dxh = dxn * g_ref[...]
        dh_ref[...] = dres_ref[...] + r * (dxh - xh * jnp.mean(dxh * xh, axis=-1, keepdims=True))

        @pl.when(pl.program_id(0) == 0)
        def _():
            dg_ref[...] = jnp.zeros_like(dg_ref)

        dg_ref[...] += jnp.sum(dxn * xh, axis=0, keepdims=True)

    row = pl.BlockSpec((tr, D), lambda i: (i, 0))
    vec = pl.BlockSpec((1, D), lambda i: (0, 0))
    return pl.pallas_call(
        body, name=name, grid=(L // tr,), in_specs=[row, row, vec, row], out_specs=[row, vec],
        out_shape=[jax.ShapeDtypeStruct((L, D), F32), jax.ShapeDtypeStruct((1, D), F32)],
        compiler_params=_params("arbitrary"),
    )(dxn, h, g.reshape(1, D), dres)


def _head_sums(x2):
    blockdiag = (_iota((LANES, LANES), 0) // HEAD == _iota((LANES, LANES), 1) // HEAD).astype(BF16)
    cols = [_dotx(x2[:, j:j + LANES], blockdiag) for j in range(0, x2.shape[1], LANES)]
    return jnp.concatenate(cols, axis=1) if len(cols) > 1 else cols[0]


def _headrms_fwd(name, raw, g):
    L, D = raw.shape
    tr = _tile(L, 512, 16)

    def body(x_ref, g_ref, o_ref):
        x = x_ref[...]
        r = lax.rsqrt(_head_sums(x * x) * (1.0 / HEAD) + RMS_EPS)
        o_ref[...] = (x * r * g_ref[...]).astype(BF16)

    return pl.pallas_call(
        body, name=name, grid=(L // tr,),
        in_specs=[pl.BlockSpec((tr, D), lambda i: (i, 0)), pl.BlockSpec((1, D), lambda i: (0, 0))],
        out_specs=pl.BlockSpec((tr, D), lambda i: (i, 0)),
        out_shape=jax.ShapeDtypeStruct((L, D), BF16), compiler_params=_params("parallel"),
    )(raw, g)


def _headrms_bwd(name, dy, raw, g):
    L, D = raw.shape
    tr = _tile(L, 512, 16)

    def body(dy_ref, x_ref, g_ref, dx_ref, dg_ref):
        x = x_ref[...]
        dy = dy_ref[...]
        r = lax.rsqrt(_head_sums(x * x) * (1.0 / HEAD) + RMS_EPS)
        xh = x * r
        dxh = dy * g_ref[...]
        dx_ref[...] = (r * (dxh - xh * (_head_sums(dxh * xh) * (1.0 / HEAD)))).astype(BF16)

        @pl.when(pl.program_id(0) == 0)
        def _():
            dg_ref[...] = jnp.zeros_like(dg_ref)

        dg_ref[...] += jnp.sum(dy * xh, axis=0, keepdims=True)

    row = pl.BlockSpec((tr, D), lambda i: (i, 0))
    vec = pl.BlockSpec((1, D), lambda i: (0, 0))
    return pl.pallas_call(
        body, name=name, grid=(L // tr,), in_specs=[row, row, vec], out_specs=[row, vec],
        out_shape=[jax.ShapeDtypeStruct((L, D), BF16), jax.ShapeDtypeStruct((1, D), F32)],
        compiler_params=_params("arbitrary"),
    )(dy, raw, g)


def _loss(name, h, tgt, pad_rows):
    L, D = h.shape
    nb = L // BLK
    assert pad_rows == BLK

    def body(h_ref, t_ref, dh_ref, s_ref):
        i = pl.program_id(0)

        @pl.when(i == 0)
        def _():
            s_ref[...] = jnp.zeros_like(s_ref)
            dh_ref[...] = jnp.zeros_like(dh_ref)

        @pl.when(i > 0)
        def _():
            e = h_ref[...] - t_ref[...]
            dh_ref[...] = e * (1.0 / D)
            s_ref[...] += jnp.sum(e * e, axis=0, keepdims=True)

    return pl.pallas_call(
        body, name=name, grid=(nb,),
        in_specs=[pl.BlockSpec((BLK, D), lambda i: (i, 0)), pl.BlockSpec((BLK, D), lambda i: (jnp.maximum(i - 1, 0), 0))],
        out_specs=[pl.BlockSpec((BLK, D), lambda i: (i, 0)), pl.BlockSpec((1, D), lambda i: (0, 0))],
        out_shape=[jax.ShapeDtypeStruct((L, D), F32), jax.ShapeDtypeStruct((1, D), F32)],
        compiler_params=_params("arbitrary"),
    )(h, tgt)


def _swiglu_up(name, xn, w1, w3):
    L, D = xn.shape
    Fd = w1.shape[1]
    tm, tn = _tile(L, 704, 16), _tile(Fd, 1408, LANES)

    def epi(accs, _):
        a, b = accs
        return [a, b, a * _sigmoid(a) * b]

    return _matmul(name, [(xn, w1), (xn, w3)], "nn", tm, tn, D, [BF16, BF16, BF16], epilogue=epi, separate=True)


def _swiglu_bwd(name, dh, w2, a, b):
    L, D = dh.shape
    Fd = w2.shape[0]
    tm, tn = _tile(L, 704, 16), _tile(Fd, 1408, LANES)

    def epi(accs, ex):
        dact = accs[0] * FFN_RES
        av, bv = ex[0].astype(F32), ex[1].astype(F32)
        s = _sigmoid(av)
        return [dact * bv * (s * (1.0 + av * (1.0 - s))), dact * av * s]

    return _matmul(name, [(dh, w2)], "nt", tm, tn, D, [BF16, BF16], epilogue=epi, extras=[(a, "mn"), (b, "mn")])


def _ffn_fwd(tag, h, g, w1, w3, w2):
    L, D = h.shape
    xn = _rms_fwd(f"{tag}_norm", h, g)
    a, b, act = _swiglu_up(f"{tag}_up", xn, w1, w3)
    tm = _tile(L, 704, 8)
    (h_out,) = _matmul(f"{tag}_down", [(act, w2)], "nn", tm, D, w2.shape[0], [F32],
                       epilogue=lambda accs, ex: [ex[0] + FFN_RES * accs[0]], extras=[(h, "mn")])
    return h_out, (xn, a, b, act)


def _ffn_bwd(tag, dh_out, h, g, w1, w3, w2, saved):
    xn, a, b, act = saved
    L, D = h.shape
    Fd = w2.shape[0]
    tl = _tile(L, 704, 16)
    da, db = _swiglu_bwd(f"{tag}_dact", dh_out, w2, a, b)
    (dw2,) = _matmul(f"{tag}_dw2", [(act, dh_out)], "tn", _tile(Fd, 1408, LANES), D, tl, [F32],
                     epilogue=lambda accs, ex: [FFN_RES * accs[0]])
    (dw1,) = _matmul(f"{tag}_dw1", [(xn, da)], "tn", D, _tile(Fd, 1408, LANES), tl, [F32])
    (dw3,) = _matmul(f"{tag}_dw3", [(xn, db)], "tn", D, _tile(Fd, 1408, LANES), tl, [F32])
    (dxn,) = _matmul(f"{tag}_dxn", [(da, w1), (db, w3)], "nt", _tile(L, 704, 8), _tile(D, 512, LANES), Fd, [F32])
    dh, dg = _rms_bwd(f"{tag}_dnorm", dxn, h, g, dh_out)
    return dh, dg.reshape(-1), dw1, dw3, dw2


def _conv_taps(ext, k):
    return ext if k == 0 else pltpu.roll(ext, k, axis=0)


def _conv_fwd(name, zx, col0, w, b, pad):
    L = zx.shape[0]
    C = w.shape[1]
    tr, tc = _tile(L, 704, 8), _tile(C, 512, LANES)
    cb = col0 // tc
    assert col0 % tc == 0

    def body(u_ref, halo_ref, w_ref, b_ref, o_ref):
        ext = jnp.concatenate([halo_ref[...], u_ref[...]], axis=0)
        pre = b_ref[...] + sum(_conv_taps(ext, 3 - k)[8:] * w_ref[k:k + 1, :] for k in range(4))
        rows = _iota(pre.shape, 0) + pl.program_id(1) * tr
        o_ref[...] = jnp.where(rows >= pad, pre * _sigmoid(pre), 0.0)

    return pl.pallas_call(
        body, name=name, grid=(C // tc, L // tr),
        in_specs=[pl.BlockSpec((tr, tc), lambda j, i: (i, cb + j)),
                  pl.BlockSpec((8, tc), lambda j, i: (jnp.maximum(i * (tr // 8) - 1, 0), cb + j)),
                  pl.BlockSpec((4, tc), lambda j, i: (0, j)), pl.BlockSpec((1, tc), lambda j, i: (0, j))],
        out_specs=pl.BlockSpec((tr, tc), lambda j, i: (i, j)),
        out_shape=jax.ShapeDtypeStruct((L, C), F32), compiler_params=_params("parallel", "parallel"),
    )(zx, zx, w, b)


def _conv_bwd_pre(name, dact, zx, col0, w, b, pad):
    L = zx.shape[0]
    C = w.shape[1]
    tr, tc = _tile(L, 704, 8), _tile(C, 512, LANES)
    cb = col0 // tc

    def body(d_ref, u_ref, halo_ref, w_ref, b_ref, dp_ref, dw_ref, db_ref):
        ext = jnp.concatenate([halo_ref[...], u_ref[...]], axis=0)
        taps = [_conv_taps(ext, 3 - k)[8:] for k in range(4)]
        pre = b_ref[...] + sum(taps[k] * w_ref[k:k + 1, :] for k in range(4))
        s = _sigmoid(pre)
        rows = _iota(pre.shape, 0) + pl.program_id(1) * tr
        dpre = jnp.where(rows >= pad, d_ref[...] * (s * (1.0 + pre * (1.0 - s))), 0.0)
        dp_ref[...] = dpre

        @pl.when(pl.program_id(1) == 0)
        def _():
            dw_ref[...] = jnp.zeros_like(dw_ref)
            db_ref[...] = jnp.zeros_like(db_ref)

        db_ref[...] += jnp.sum(dpre, axis=0, keepdims=True)
        dw_ref[...] += jnp.concatenate([jnp.sum(dpre * taps[k], axis=0, keepdims=True) for k in range(4)], axis=0)

    return pl.pallas_call(
        body, name=name, grid=(C // tc, L // tr),
        in_specs=[pl.BlockSpec((tr, tc), lambda j, i: (i, j)),
                  pl.BlockSpec((tr, tc), lambda j, i: (i, cb + j)),
                  pl.BlockSpec((8, tc), lambda j, i: (jnp.maximum(i * (tr // 8) - 1, 0), cb + j)),
                  pl.BlockSpec((4, tc), lambda j, i: (0, j)), pl.BlockSpec((1, tc), lambda j, i: (0, j))],
        out_specs=[pl.BlockSpec((tr, tc), lambda j, i: (i, j)), pl.BlockSpec((4, tc), lambda j, i: (0, j)),
                   pl.BlockSpec((1, tc), lambda j, i: (0, j))],
        out_shape=[jax.ShapeDtypeStruct((L, C), F32), jax.ShapeDtypeStruct((4, C), F32), jax.ShapeDtypeStruct((1, C), F32)],
        compiler_params=_params("parallel", "arbitrary"),
    )(dact, zx, zx, w, b)


def _conv_bwd_in(name, dpre, w):
    L, C = dpre.shape
    tr, tc = _tile(L, 704, 16), _tile(C, 512, LANES)
    nr = L // tr

    def body(d_ref, halo_ref, w_ref, o_ref):
        halo = jnp.where(pl.program_id(1) == nr - 1, 0.0, halo_ref[...])
        ext = jnp.concatenate([d_ref[...], halo], axis=0)
        acc = ext[:tr] * w_ref[3:4, :]
        for k in range(3):
            acc = acc + pltpu.roll(ext, tr + 8 - (3 - k), axis=0)[:tr] * w_ref[k:k + 1, :]
        o_ref[...] = acc.astype(BF16)

    return pl.pallas_call(
        body, name=name, grid=(C // tc, nr),
        in_specs=[pl.BlockSpec((tr, tc), lambda j, i: (i, j)),
                  pl.BlockSpec((8, tc), lambda j, i: (jnp.minimum((i + 1) * (tr // 8), L // 8 - 1), j)),
                  pl.BlockSpec((4, tc), lambda j, i: (0, j))],
        out_specs=pl.BlockSpec((tr, tc), lambda j, i: (i, j)),
        out_shape=jax.ShapeDtypeStruct((L, C), BF16), compiler_params=_params("parallel", "parallel"),
    )(dpre, dpre, w)


def _dt_fwd(name, zx, col0, bias_row, nheads, pad):
    L = zx.shape[0]
    cb = col0 // LANES

    def body(x_ref, b_ref, dt_ref, dtt_ref):
        v = x_ref[...] + b_ref[...]
        sp = jnp.maximum(v, 0.0) + jnp.log(1.0 + jnp.exp(-jnp.abs(v)))
        rows = _iota(v.shape, 0) + pl.program_id(0) * BLK
        dt = jnp.where((rows >= pad) & (_iota(v.shape, 1) < nheads), sp, 0.0)
        dt_ref[...] = dt
        dtt_ref[...] = dt.T

    return pl.pallas_call(
        body, name=name, grid=(L // BLK,),
        in_specs=[pl.BlockSpec((BLK, LANES), lambda i: (i, cb)), pl.BlockSpec((1, LANES), lambda i: (0, 0))],
        out_specs=[pl.BlockSpec((BLK, LANES), lambda i: (i, 0)), pl.BlockSpec((LANES, BLK), lambda i: (0, i))],
        out_shape=[jax.ShapeDtypeStruct((L, LANES), F32), jax.ShapeDtypeStruct((LANES, L), F32)],
        compiler_params=_params("parallel"),
    )(zx, bias_row)


def _dt_bwd(name, ddt, zx, col0, bias_row, nheads, pad):
    L = zx.shape[0]
    cb = col0 // LANES

    def body(d_ref, x_ref, b_ref, o_ref, db_ref):
        v = x_ref[...] + b_ref[...]
        rows = _iota(v.shape, 0) + pl.program_id(0) * BLK
        g = jnp.where((rows >= pad) & (_iota(v.shape, 1) < nheads), d_ref[...] * _sigmoid(v), 0.0)
        o_ref[...] = g.astype(BF16)

        @pl.when(pl.program_id(0) == 0)
        def _():
            db_ref[...] = jnp.zeros_like(db_ref)

        db_ref[...] += jnp.sum(g, axis=0, keepdims=True)

    return pl.pallas_call(
        body, name=name, grid=(L // BLK,),
        in_specs=[pl.BlockSpec((BLK, LANES), lambda i: (i, 0)), pl.BlockSpec((BLK, LANES), lambda i: (i, cb)),
                  pl.BlockSpec((1, LANES), lambda i: (0, 0))],
        out_specs=[pl.BlockSpec((BLK, LANES), lambda i: (i, 0)), pl.BlockSpec((1, LANES), lambda i: (0, 0))],
        out_shape=[jax.ShapeDtypeStruct((L, LANES), BF16), jax.ShapeDtypeStruct((1, LANES), F32)],
        compiler_params=_params("arbitrary"),
    )(ddt, zx, bias_row)


def _ssd_common(dt, dtt, a_row, a_col):
    tril = (_iota((BLK, BLK), 0) >= _iota((BLK, BLK), 1)).astype(BF16)
    cum = _dotx(tril, dt * a_row, split="b")
    cumt = _dotx(dtt * a_col, tril, NT)
    return cum, cumt


def _ssd_fwd(name, xbc, dt, dtt, a_row, a_col, expand, di, ng):
    L = xbc.shape[0]
    nc = L // BLK
    hpg = di // HEAD // ng
    gw = hpg * HEAD
    assert gw % LANES == 0

    def body(x_ref, dt_ref, dtt_ref, ar_ref, ac_ref, ex_ref, y_ref, st_ref, h_ref):
        @pl.when(pl.program_id(0) == 0)
        def _():
            h_ref[...] = jnp.zeros_like(h_ref)

        st_ref[0] = h_ref[...]
        dt, dtt = dt_ref[...], dtt_ref[...]
        cum, cumt = _ssd_common(dt, dtt, ar_ref[...], ac_ref[...])
        ex = ex_ref[...]
        ecum_x = _dotx(jnp.exp(cum), ex)
        wend_x = _dotx(jnp.exp(cum[BLK - 1:BLK, :] - cum) * dt, ex)
        ecl = jnp.broadcast_to(jnp.exp(cumt[:, BLK - 1:BLK]), (LANES, LANES))
        decay_h = _dotx(ex, ecl, TN, split="b")
        causal = _iota((BLK, BLK), 0) >= _iota((BLK, BLK), 1)
        low = _iota((BLK, LANES), 1) < HEAD
        for g in range(ng):
            xg = x_ref[:, g * gw:(g + 1) * gw]
            bg = x_ref[:, di + g * D_STATE:di + (g + 1) * D_STATE].astype(BF16)
            cg = x_ref[:, di + (ng + g) * D_STATE:di + (ng + g + 1) * D_STATE].astype(BF16)
            hg = h_ref[g * gw:(g + 1) * gw, :]
            gram = _dot(cg, bg, NT)
            yoff = _dot(cg, hg.astype(BF16), NT) * ecum_x[:, g * gw:(g + 1) * gw]
            parts = []
            for j in range(gw // LANES):
                xp = xg[:, j * LANES:(j + 1) * LANES].astype(BF16)
                yd = []
                for hh in range(2):
                    h = g * hpg + 2 * j + hh
                    seg = cum[:, h:h + 1] - cumt[h:h + 1, :]
                    m = gram * jnp.where(causal, jnp.exp(jnp.minimum(seg, 0.0)), 0.0) * dtt[h:h + 1, :]
                    yd.append(_dot(m.astype(BF16), xp))
                parts.append(jnp.where(low, yd[0], yd[1]))
            y_ref[:, g * gw:(g + 1) * gw] = jnp.concatenate(parts, axis=1) + yoff
            xw = (xg * wend_x[:, g * gw:(g + 1) * gw]).astype(BF16)
            h_ref[g * gw:(g + 1) * gw, :] = hg * decay_h[g * gw:(g + 1) * gw, :] + _dot(xw, bg, TN)

    W = xbc.shape[1]
    full = lambda r, c: pl.BlockSpec((r, c), lambda i: (0, 0))
    return pl.pallas_call(
        body, name=name, grid=(nc,),
        in_specs=[pl.BlockSpec((BLK, W), lambda i: (i, 0)), pl.BlockSpec((BLK, LANES), lambda i: (i, 0)),
                  pl.BlockSpec((LANES, BLK), lambda i: (0, i)), full(1, LANES), full(LANES, LANES), full(LANES, di)],
        out_specs=[pl.BlockSpec((BLK, di), lambda i: (i, 0)), pl.BlockSpec((1, di, D_STATE), lambda i: (i, 0, 0))],
        out_shape=[jax.ShapeDtypeStruct((L, di), F32), jax.ShapeDtypeStruct((nc, di, D_STATE), F32)],
        scratch_shapes=[pltpu.VMEM((di, D_STATE), F32)], compiler_params=_params("arbitrary"),
    )(xbc, dt, dtt, a_row, a_col, expand)


def _ssd_bwd(name, xbc, dt, dtt, a_row, a_col, expand, states, dy, d_x, di, ng):
    L, W = xbc.shape
    nc = L // BLK
    hpg = di // HEAD // ng
    gw = hpg * HEAD

    def body(x_ref, dt_ref, dtt_ref, ar_ref, ac_ref, ex_ref, st_ref, dy_ref, dx_ref_in, dxo_ref, ddt_ref, da_ref, dh_ref):
        @pl.when(pl.program_id(0) == 0)
        def _():
            dh_ref[...] = jnp.zeros_like(dh_ref)
            da_ref[...] = jnp.zeros_like(da_ref)

        dt, dtt, a_row = dt_ref[...], dtt_ref[...], ar_ref[...]
        cum, cumt = _ssd_common(dt, dtt, a_row, ac_ref[...])
        ex = ex_ref[...]
        ecum = jnp.exp(cum)
        ecum_x = _dotx(ecum, ex)
        e_s = jnp.exp(cum[BLK - 1:BLK, :] - cum)
        wend_x = _dotx(e_s * dt, ex)
        ecl_col = jnp.exp(cumt[:, BLK - 1:BLK])
        decay_h = _dotx(ex, jnp.broadcast_to(ecl_col, (LANES, LANES)), TN, split="b")
        causal = _iota((BLK, BLK), 0) >= _iota((BLK, BLK), 1)
        low = _iota((BLK, LANES), 1) < HEAD
        lane = _iota((1, LANES), 1)
        sub = _iota((LANES, 1), 0)
        dcum_c = jnp.zeros((BLK, LANES), F32)
        dcum_r = jnp.zeros((LANES, BLK), F32)
        ddt_r = jnp.zeros((LANES, BLK), F32)
        zoff = []
        dwend_src = []
        for g in range(ng):
            gs = slice(g * gw, (g + 1) * gw)
            xg = x_ref[:, gs]
            bg = x_ref[:, di + g * D_STATE:di + (g + 1) * D_STATE].astype(BF16)
            cg = x_ref[:, di + (ng + g) * D_STATE:di + (ng + g + 1) * D_STATE].astype(BF16)
            hprev = st_ref[0, gs, :]
            dhn = dh_ref[gs, :]
            dyg = dy_ref[:, gs]
            gram = _dot(cg, bg, NT)
            dgram = jnp.zeros((BLK, BLK), F32)
            dxg = []
            for j in range(gw // LANES):
                xp = xg[:, j * LANES:(j + 1) * LANES].astype(BF16)
                dyp = dyg[:, j * LANES:(j + 1) * LANES]
                dxh = []
                for hh in range(2):
                    h = g * hpg + 2 * j + hh
                    seg = cum[:, h:h + 1] - cumt[h:h + 1, :]
                    lm = jnp.where(causal, jnp.exp(jnp.minimum(seg, 0.0)), 0.0)
                    dtr = dtt[h:h + 1, :]
                    m = gram * lm * dtr
                    dym = jnp.where(low if hh == 0 else ~low, dyp, 0.0).astype(BF16)
                    dxh.append(_dot(m.astype(BF16), dym, TN))
                    dm = _dot(dym, xp, NT)
                    dgram = dgram + dm * lm * dtr
                    v = dm * gram * lm
                    wv = v * dtr
                    ddt_r = ddt_r + jnp.where(sub == h, jnp.sum(v, axis=0, keepdims=True), 0.0)
                    dcum_r = dcum_r - jnp.where(sub == h, jnp.sum(wv, axis=0, keepdims=True), 0.0)
                    dcum_c = dcum_c + jnp.where(lane == h, jnp.sum(wv, axis=1, keepdims=True), 0.0)
                dxg.append(jnp.where(low, dxh[0], dxh[1]))
            dx_diag = jnp.concatenate(dxg, axis=1)
            hb = hprev.astype(BF16)
            yoff = _dot(cg, hb, NT) * ecum_x[:, gs]
            dye = (dyg * ecum_x[:, gs]).astype(BF16)
            dcg = _dot(dye, hb) + _dot(dgram.astype(BF16), bg)
            dbg = _dot(dgram.astype(BF16), cg, TN)
            dh_prev = _dot(dye, cg, TN)
            zoff.append(dyg * yoff)
            dhb = dhn.astype(BF16)
            dxw = _dot(bg, dhb, NT)
            xw = (xg * wend_x[:, gs]).astype(BF16)
            dbg = dbg + _dot(xw, dhb)
            dwend_src.append(dxw * xg)
            dxo_ref[:, gs] = dx_diag + dxw * wend_x[:, gs] + dyg * dx_ref_in[:, gs]
            dxo_ref[:, di + g * D_STATE:di + (g + 1) * D_STATE] = dbg
            dxo_ref[:, di + (ng + g) * D_STATE:di + (ng + g + 1) * D_STATE] = dcg
            prod = dhn * hprev
            dd = jnp.sum(_dotx(ex[:, gs], prod, split="b"), axis=1, keepdims=True)
            dcum_r = dcum_r + jnp.where(_iota((1, BLK), 1) == BLK - 1, dd * ecl_col, 0.0)
            dh_ref[gs, :] = dhn * decay_h[gs, :] + dh_prev
        dcum_c = dcum_c + _dotx(jnp.concatenate(zoff, axis=1), ex, NT)
        dwend = _dotx(jnp.concatenate(dwend_src, axis=1), ex, NT)
        ddt_c = dwend * e_s
        de = dwend * dt * e_s
        dcum_c = dcum_c - de + jnp.where(_iota((BLK, 1), 0) == BLK - 1, jnp.sum(de, axis=0, keepdims=True), 0.0)
        dcum = dcum_c + dcum_r.T
        triu = (_iota((BLK, BLK), 0) <= _iota((BLK, BLK), 1)).astype(BF16)
        da = _dotx(triu, dcum, split="b")
        ddt_ref[...] = ddt_c + ddt_r.T + da * a_row
        da_ref[...] += jnp.sum(da * dt, axis=0, keepdims=True)

    rev = lambda i: nc - 1 - i
    full = lambda r, c: pl.BlockSpec((r, c), lambda i: (0, 0))
    return pl.pallas_call(
        body, name=name, grid=(nc,),
        in_specs=[pl.BlockSpec((BLK, W), lambda i: (rev(i), 0)), pl.BlockSpec((BLK, LANES), lambda i: (rev(i), 0)),
                  pl.BlockSpec((LANES, BLK), lambda i: (0, rev(i))), full(1, LANES), full(LANES, LANES), full(LANES, di),
                  pl.BlockSpec((1, di, D_STATE), lambda i: (rev(i), 0, 0)), pl.BlockSpec((BLK, di), lambda i: (rev(i), 0)),
                  full(1, di)],
        out_specs=[pl.BlockSpec((BLK, W), lambda i: (rev(i), 0)), pl.BlockSpec((BLK, LANES), lambda i: (rev(i), 0)),
                   full(1, LANES)],
        out_shape=[jax.ShapeDtypeStruct((L, W), F32), jax.ShapeDtypeStruct((L, LANES), F32),
                   jax.ShapeDtypeStruct((1, LANES), F32)],
        scratch_shapes=[pltpu.VMEM((di, D_STATE), F32)], compiler_params=_params("arbitrary"),
    )(xbc, dt, dtt, a_row, a_col, expand, states, dy, d_x)


def _group_sums(v, gsz):
    cols = []
    for j in range(0, v.shape[1], gsz):
        s = jnp.sum(v[:, j:j + gsz], axis=1, keepdims=True)
        cols.append(jnp.broadcast_to(s, (v.shape[0], gsz)))
    return jnp.concatenate(cols, axis=1)


def _gate_fwd(name, y, xbc, zx, d_x, ng_row, gsz):
    L, di = y.shape
    tr = _tile(L, 512, 16)

    def body(y_ref, x_ref, z_ref, d_ref, g_ref, o_ref):
        z = z_ref[...]
        y2 = (y_ref[...] + d_ref[...] * x_ref[...]) * (z * _sigmoid(z))
        r = lax.rsqrt(_group_sums(y2 * y2, gsz) * (1.0 / gsz) + RMS_EPS)
        o_ref[...] = (y2 * r * g_ref[...]).astype(BF16)

    row = pl.BlockSpec((tr, di), lambda i: (i, 0))
    vec = pl.BlockSpec((1, di), lambda i: (0, 0))
    return pl.pallas_call(
        body, name=name, grid=(L // tr,), in_specs=[row, row, row, vec, vec], out_specs=row,
        out_shape=jax.ShapeDtypeStruct((L, di), BF16), compiler_params=_params("parallel"),
    )(y, xbc, zx, d_x, ng_row)


def _gate_bwd(name, dy3, y, xbc, zx, d_x, ng_row, gsz):
    L, di = y.shape
    tr = _tile(L, 256, 16)

    def body(dy_ref, y_ref, x_ref, z_ref, d_ref, g_ref, dz_ref, dy1_ref, dg_ref, dd_ref):
        z, x = z_ref[...], x_ref[...]
        s = _sigmoid(z)
        sz = z * s
        y1 = y_ref[...] + d_ref[...] * x
        y2 = y1 * sz
        r = lax.rsqrt(_group_sums(y2 * y2, gsz) * (1.0 / gsz) + RMS_EPS)
        yg = y2 * r
        dy3 = dy_ref[...]
        dyg = dy3 * g_ref[...]
        dy2 = r * (dyg - yg * (_group_sums(dyg * yg, gsz) * (1.0 / gsz)))
        dz_ref[...] = (dy2 * y1 * (s * (1.0 + z * (1.0 - s)))).astype(BF16)
        dy1 = dy2 * sz
        dy1_ref[...] = dy1

        @pl.when(pl.program_id(0) == 0)
        def _():
            dg_ref[...] = jnp.zeros_like(dg_ref)
            dd_ref[...] = jnp.zeros_like(dd_ref)

        dg_ref[...] += jnp.sum(dy3 * yg, axis=0, keepdims=True)
        dd_ref[...] += jnp.sum(dy1 * x, axis=0, keepdims=True)

    row = pl.BlockSpec((tr, di), lambda i: (i, 0))
    vec = pl.BlockSpec((1, di), lambda i: (0, 0))
    return pl.pallas_call(
        body, name=name, grid=(L // tr,), in_specs=[row, row, row, row, vec, vec], out_specs=[row, row, vec, vec],
        out_shape=[jax.ShapeDtypeStruct((L, di), BF16), jax.ShapeDtypeStruct((L, di), F32),
                   jax.ShapeDtypeStruct((1, di), F32), jax.ShapeDtypeStruct((1, di), F32)],
        compiler_params=_params("arbitrary"),
    )(dy3, y, xbc, zx, d_x, ng_row)


SB_UNROLL_FWD = 4
SB_UNROLL_BWD = 4


def _sb_heads(x):
    low = _iota(x.shape, 1) < HEAD
    zero = jnp.zeros_like(x)
    return jnp.concatenate([jnp.where(low, x, zero), jnp.where(low, zero, x)], axis=0)


def _sb_unheads(x2):
    return jnp.where(_iota((BLK, LANES), 1) < HEAD, x2[:BLK], x2[BLK:])


def _sb_tiles(qq, kblks, dmat, scol, thrs, s0s):
    n = range(len(kblks))
    z = [_dot(qq, kblks[u], NT) for u in n]
    e = [jnp.exp(-jnp.abs(z[u])) for u in n]
    l1 = [jnp.log(1.0 + e[u]) for u in n]
    valid = [(dmat > thrs[u]) & (scol >= s0s[u]) for u in n]
    lsz = [jnp.minimum(z[u], 0.0) - l1[u] for u in n]
    lkm = [jnp.where(valid[u], -jnp.maximum(z[u], 0.0) - l1[u], 0.0) for u in n]
    return z, e, lsz, lkm, valid


def _sb_fwd(name, q, k, v, pad):
    L, D = q.shape
    nb = L // BLK
    U = SB_UNROLL_FWD

    def body(q_ref, k_ref, v_ref, o_ref, o32_ref):
        i = pl.program_id(1)
        after = (_iota((BLK, BLK), 0) > _iota((BLK, BLK), 1)).astype(BF16)
        qq = _sb_heads(q_ref[...] * 0.125)
        dmat = (_iota((2 * BLK, BLK), 0) & (BLK - 1)) - _iota((2 * BLK, BLK), 1)
        scol = _iota((2 * BLK, BLK), 1)

        def group(j, carry):
            c, acc = carry
            rng = range(U)
            kraw = [i - (j * U + u) for u in rng]
            kb = [jnp.maximum(kraw[u], 0) for u in rng]
            rows = [pl.ds(pl.multiple_of(kb[u] * BLK, BLK), BLK) for u in rng]
            thr = [jnp.where(kraw[u] >= 0, (kb[u] - i) * BLK, BLK) for u in rng]
            _, _, lsz, lkm, valid = _sb_tiles(qq, [k_ref[rows[u], :] for u in rng], dmat, scol, thr,
                                              [pad - kb[u] * BLK for u in rng])
            cum = [_dotx(lkm[u], after, parts=2) for u in rng]
            a = []
            for u in rng:
                a.append(jnp.where(valid[u], jnp.exp(lsz[u] + (c + cum[u])), 0.0).astype(BF16))
                c = c + jnp.sum(lkm[u], axis=1, keepdims=True)
            for u in rng:
                acc = acc + _dot(a[u], v_ref[rows[u], :])
            return c, acc

        _, acc = lax.fori_loop(0, (i + U) // U, group,
                               (jnp.zeros((2 * BLK, 1), F32), jnp.zeros((2 * BLK, LANES), F32)))
        out = _sb_unheads(acc)
        o_ref[...] = out.astype(BF16)
        o32_ref[...] = out

    blk = pl.BlockSpec((BLK, LANES), lambda p, i: (i, p))
    col = pl.BlockSpec((L, LANES), lambda p, i: (0, p))
    return pl.pallas_call(
        body, name=name, grid=(D // LANES, nb), in_specs=[blk, col, col], out_specs=[blk, blk],
        out_shape=[jax.ShapeDtypeStruct((L, D), BF16), jax.ShapeDtypeStruct((L, D), F32)],
        compiler_params=_params("parallel", "arbitrary"),
    )(q, k, v)


def _sb_bwd(name, q, k, v, do, o32, pad):
    L, D = q.shape
    nb = L // BLK
    U = SB_UNROLL_BWD

    def body(q_ref, k_ref, v_ref, do_ref, o_ref, dq_ref, dk_ref, dv_ref):
        i = pl.program_id(1)

        @pl.when(i == 0)
        def _():
            dk_ref[...] = jnp.zeros_like(dk_ref)
            dv_ref[...] = jnp.zeros_like(dv_ref)

        after = (_iota((BLK, BLK), 0) > _iota((BLK, BLK), 1)).astype(BF16)
        from_j = (_iota((BLK, BLK), 0) >= _iota((BLK, BLK), 1)).astype(BF16)
        qq = _sb_heads(q_ref[...] * 0.125)
        dd = _sb_heads(do_ref[...])
        dmat = (_iota((2 * BLK, BLK), 0) & (BLK - 1)) - _iota((2 * BLK, BLK), 1)
        scol = _iota((2 * BLK, BLK), 1)
        o2 = jnp.concatenate([o_ref[...], o_ref[...]], axis=0)
        total = jnp.sum(dd.astype(F32) * o2, axis=1, keepdims=True)

        def group(j, carry):
            c, met, acc = carry
            rng = range(U)
            kraw = [i - (j * U + u) for u in rng]
            kb = [jnp.maximum(kraw[u], 0) for u in rng]
            rows = [pl.ds(pl.multiple_of(kb[u] * BLK, BLK), BLK) for u in rng]
            thr = [jnp.where(kraw[u] >= 0, (kb[u] - i) * BLK, BLK) for u in rng]
            kblk = [k_ref[rows[u], :] for u in rng]
            z, e, lsz, lkm, valid = _sb_tiles(qq, kblk, dmat, scol, thr, [pad - kb[u] * BLK for u in rng])
            da = [_dot(dd, v_ref[rows[u], :], NT) for u in rng]
            cum = [_dotx(lkm[u], after, parts=2) for u in rng]
            a = []
            for u in rng:
                a.append(jnp.where(valid[u], jnp.exp(lsz[u] + (c + cum[u])), 0.0).astype(BF16))
                c = c + jnp.sum(lkm[u], axis=1, keepdims=True)
            dlog = [da[u] * a[u].astype(F32) for u in rng]
            here = [_dotx(dlog[u], from_j, parts=2) for u in rng]
            dz = []
            for u in rng:
                inv = 1.0 / (1.0 + e[u])
                sig = jnp.where(z[u] >= 0.0, inv, e[u] * inv)
                before = total - (met + here[u])
                dz.append((dlog[u] * (1.0 - sig) - jnp.where(valid[u], before * sig, 0.0)).astype(BF16))
                met = met + jnp.sum(dlog[u], axis=1, keepdims=True)
            for u in rng:
                dk_ref[rows[u], :] += _dot(dz[u], qq, TN)
                dv_ref[rows[u], :] += _dot(a[u], dd, TN)
                acc = acc + _dot(dz[u], kblk[u])
            return c, met, acc

        col1 = jnp.zeros((2 * BLK, 1), F32)
        _, _, acc = lax.fori_loop(0, (i + U) // U, group, (col1, col1, jnp.zeros((2 * BLK, LANES), F32)))
        dq_ref[...] = _sb_unheads(acc) * 0.125

    blk = pl.BlockSpec((BLK, LANES), lambda p, i: (i, p))
    col = pl.BlockSpec((L, LANES), lambda p, i: (0, p))
    return pl.pallas_call(
        body, name=name, grid=(D // LANES, nb), in_specs=[blk, col, col, blk, blk], out_specs=[blk, col, col],
        out_shape=[jax.ShapeDtypeStruct((L, D), F32)] * 3, compiler_params=_params("parallel", "arbitrary"),
    )(q, k, v, do, o32)


def _local_step(x, tgt, w):
    S, D = x.shape
    nm = w["meta_tokens"].shape[0]
    pad = BLK - nm
    L = pad + nm + S
    assert L % BLK == 0 and 0 < nm <= BLK
    di = w["ssm_out_proj"].shape[0]
    nh = w["ssm_dt_bias"].shape[0]
    assert di == nh * HEAD and nh <= LANES
    conv_dim = w["ssm_conv_w"].shape[1]
    ng = (conv_dim - di) // (2 * D_STATE)
    zp = di + conv_dim + LANES
    g = w["norm_g"]
    grads = {}

    h0 = jnp.concatenate([jnp.zeros((pad, D), F32), w["meta_tokens"], x], axis=0)

    h1, s1 = _ffn_fwd("f00", h0, g[0, 0], w["ffn_w1"][0, 0], w["ffn_w3"][0, 0], w["ffn_w2"][0, 0])
    u0 = _rms_fwd("m_norm", h1, g[0, 1])
    w_in = jnp.concatenate([w["ssm_in_proj"], jnp.zeros((D, zp - w["ssm_in_proj"].shape[1]), BF16)], axis=1)
    tl = _tile(L, 704, 16)
    (zx,) = _matmul("m_inproj", [(u0, w_in)], "nn", tl, _tile(zp, 1024, LANES), D, [F32])
    conv_b = w["ssm_conv_b"].reshape(1, conv_dim)
    xbc = _conv_fwd("m_conv", zx, di, w["ssm_conv_w"], conv_b, pad)
    bias_row = jnp.zeros((1, LANES), F32).at[0, :nh].set(w["ssm_dt_bias"])
    dt, dtt = _dt_fwd("m_dt", zx, di + conv_dim, bias_row, nh, pad)
    a_neg = -jnp.exp(w["ssm_a_log"])
    a_row = jnp.zeros((1, LANES), F32).at[0, :nh].set(a_neg)
    a_col = jnp.broadcast_to(a_row.reshape(LANES, 1), (LANES, LANES))
    expand = (jnp.arange(LANES)[:, None] == (jnp.arange(di) // HEAD)[None, :]).astype(BF16)
    y_ssd, states = _ssd_fwd("m_ssd", xbc, dt, dtt, a_row, a_col, expand, di, ng)
    d_x = jnp.repeat(w["ssm_d"], HEAD).reshape(1, di)
    ssm_g = w["ssm_norm_g"].reshape(1, di)
    gsz = di // ng
    y3 = _gate_fwd("m_gate", y_ssd, xbc, zx, d_x, ssm_g, gsz)
    (h2,) = _matmul("m_outproj", [(y3, w["ssm_out_proj"])], "nn", tl, D, di, [F32],
                    epilogue=lambda accs, ex: [ex[0] + accs[0]], extras=[(h1, "mn")])
    h3, s2 = _ffn_fwd("f01", h2, g[0, 2], w["ffn_w1"][0, 1], w["ffn_w3"][0, 1], w["ffn_w2"][0, 1])

    kv_in = _rms_fwd("kv_norm", h3, w["kv_norm_g"])
    (k_raw,) = _matmul("kv_k", [(kv_in, w["w_k"])], "nn", tl, D, D, [F32])
    (v_sh,) = _matmul("kv_v", [(kv_in, w["w_v"])], "nn", tl, D, D, [BF16])
    kg = jnp.tile(w["k_norm_g"], D // HEAD).reshape(1, D)
    k_sh = _headrms_fwd("kv_knorm", k_raw, kg)

    h4, s3 = _ffn_fwd("f10", h3, g[1, 0], w["ffn_w1"][1, 0], w["ffn_w3"][1, 0], w["ffn_w2"][1, 0])
    u1 = _rms_fwd("a_norm", h4, g[1, 1])
    (q_raw,) = _matmul("a_q", [(u1, w["sb_w_q"])], "nn", tl, D, D, [F32])
    qg = jnp.tile(w["sb_q_norm_g"], D // HEAD).reshape(1, D)
    q = _headrms_fwd("a_qnorm", q_raw, qg)
    o, o32 = _sb_fwd("a_attn", q, k_sh, v_sh, pad)
    (h5,) = _matmul("a_o", [(o, w["sb_w_o"])], "nn", tl, D, D, [F32],
                    epilogue=lambda accs, ex: [ex[0] + accs[0]], extras=[(h4, "mn")])
    h6, s4 = _ffn_fwd("f11", h5, g[1, 2], w["ffn_w1"][1, 1], w["ffn_w3"][1, 1], w["ffn_w2"][1, 1])

    dh6, sq = _loss("loss", h6, tgt, pad + nm)

    dg = jnp.zeros_like(g)
    dw1 = [[None, None], [None, None]]
    dw3 = [[None, None], [None, None]]
    dw2 = [[None, None], [None, None]]
    dh5, dgv, dw1[1][1], dw3[1][1], dw2[1][1] = _ffn_bwd("b11", dh6, h5, g[1, 2], w["ffn_w1"][1, 1], w["ffn_w3"][1, 1],
                                                           w["ffn_w2"][1, 1], s4)
    dg = dg.at[1, 2].set(dgv)
    td = _tile(D, 512, LANES)
    (do,) = _matmul("b_do", [(dh5, w["sb_w_o"])], "nt", tl, D, D, [BF16])
    (grads["sb_w_o"],) = _matmul("b_dwo", [(o, dh5)], "tn", D, td, tl, [F32])
    dq, dk, dv = _sb_bwd("b_attn", q, k_sh, v_sh, do, o32, pad)
    dq_raw, dqg = _headrms_bwd("b_qnorm", dq, q_raw, qg)
    grads["sb_q_norm_g"] = dqg.reshape(D // HEAD, HEAD).sum(0)
    (grads["sb_w_q"],) = _matmul("b_dwq", [(u1, dq_raw)], "tn", D, td, tl, [F32])
    (du1,) = _matmul("b_du1", [(dq_raw, w["sb_w_q"])], "nt", tl, D, D, [F32])
    dh4, dgv = _rms_bwd("b_anorm", du1, h4, g[1, 1], dh5)
    dg = dg.at[1, 1].set(dgv.reshape(-1))
    dh3, dgv, dw1[1][0], dw3[1][0], dw2[1][0] = _ffn_bwd("b10", dh4, h3, g[1, 0], w["ffn_w1"][1, 0], w["ffn_w3"][1, 0],
                                                           w["ffn_w2"][1, 0], s3)
    dg = dg.at[1, 0].set(dgv)

    dk_raw, dkg = _headrms_bwd("b_knorm", dk, k_raw, kg)
    grads["k_norm_g"] = dkg.reshape(D // HEAD, HEAD).sum(0)
    (grads["w_k"],) = _matmul("b_dwk", [(kv_in, dk_raw)], "tn", D, td, tl, [F32])
    (grads["w_v"],) = _matmul("b_dwv", [(kv_in, dv)], "tn", D, td, tl, [F32])
    (dkv_in,) = _matmul("b_dkvin", [(dk_raw, w["w_k"]), (dv, w["w_v"])], "nt", tl, D, D, [F32])
    dh3, dgv = _rms_bwd("b_kvnorm", dkv_in, h3, w["kv_norm_g"], dh3)
    grads["kv_norm_g"] = dgv.reshape(-1)

    dh2, dgv, dw1[0][1], dw3[0][1], dw2[0][1] = _ffn_bwd("b01", dh3, h2, g[0, 2], w["ffn_w1"][0, 1], w["ffn_w3"][0, 1],
                                                           w["ffn_w2"][0, 1], s2)
    dg = dg.at[0, 2].set(dgv)
    (dy3,) = _matmul("b_dy3", [(dh2, w["ssm_out_proj"])], "nt", tl, _tile(di, 1024, LANES), D, [F32])
    (grads["ssm_out_proj"],) = _matmul("b_dwout", [(y3, dh2)], "tn", _tile(di, 1024, LANES), D, tl, [F32])
    dz, dy1, dssm_g, dd_x = _gate_bwd("b_gate", dy3, y_ssd, xbc, zx, d_x, ssm_g, gsz)
    grads["ssm_norm_g"] = dssm_g.reshape(-1)
    grads["ssm_d"] = dd_x.reshape(nh, HEAD).sum(1)
    dxbc, ddt, da = _ssd_bwd("b_ssd", xbc, dt, dtt, a_row, a_col, expand, states, dy1, d_x, di, ng)
    grads["ssm_a_log"] = da[0, :nh] * a_neg
    ddt_raw, dbias = _dt_bwd("b_dt", ddt, zx, di + conv_dim, bias_row, nh, pad)
    grads["ssm_dt_bias"] = dbias[0, :nh]
    dpre, grads["ssm_conv_w"], dconv_b = _conv_bwd_pre("b_convpre", dxbc, zx, di, w["ssm_conv_w"], conv_b, pad)
    grads["ssm_conv_b"] = dconv_b.reshape(-1)
    dxbc_raw = _conv_bwd_in("b_convin", dpre, w["ssm_conv_w"])
    dzx = jnp.concatenate([dz, dxbc_raw, ddt_raw], axis=1)
    (dw_in,) = _matmul("b_dwin", [(u0, dzx)], "tn", D, _tile(zp, 1024, LANES), tl, [F32])
    grads["ssm_in_proj"] = dw_in[:, :w["ssm_in_proj"].shape[1]]
    (du0,) = _matmul("b_du0", [(dzx, w_in)], "nt", _tile(L, 352, 16), td, zp, [F32])
    dh1, dgv = _rms_bwd("b_mnorm", du0, h1, g[0, 1], dh2)
    dg = dg.at[0, 1].set(dgv.reshape(-1))
    dh0, dgv, dw1[0][0], dw3[0][0], dw2[0][0] = _ffn_bwd("b00", dh1, h0, g[0, 0], w["ffn_w1"][0, 0], w["ffn_w3"][0, 0],
                                                           w["ffn_w2"][0, 0], s1)
    dg = dg.at[0, 0].set(dgv)

    grads["norm_g"] = dg
    grads["ffn_w1"] = jnp.stack([jnp.stack(r) for r in dw1])
    grads["ffn_w3"] = jnp.stack([jnp.stack(r) for r in dw3])
    grads["ffn_w2"] = jnp.stack([jnp.stack(r) for r in dw2])
    grads["meta_tokens"] = dh0[pad:pad + nm]
    return sq, dh0[pad + nm:], grads


HBM_SPEC = pl.BlockSpec(memory_space=pltpu.HBM)


def _place():
    return lax.axis_index("x"), lax.axis_index("y"), lax.axis_index("c")


def _allgather8(name, blk):
    m, n = blk.shape

    def body(x_ref, out_ref, send_sems, recv_sems, local_sem):
        x, y, c = _place()
        me, sibling = (x, y, c), (x, y, 1 - c)
        chips = [(1 - x, y), (x, 1 - y), (1 - x, 1 - y)]

        def rows(px, py, pc):
            return out_ref.at[pl.ds((4 * px + 2 * py + pc) * m, m), :]

        def copy(k, block, to, src=None):
            return pltpu.make_async_remote_copy(
                src_ref=rows(*block) if src is None else src, dst_ref=rows(*block),
                send_sem=send_sems.at[k], recv_sem=recv_sems.at[k], device_id=to, device_id_type=MESH)

        mine = pltpu.make_async_copy(x_ref, rows(*me), local_sem)
        mine.start()
        first = [copy(0, me, sibling, src=x_ref)]
        first += [copy(1 + j, me, (*chip, c), src=x_ref) for j, chip in enumerate(chips)]
        for cp in first:
            cp.start()
        passed = [copy(4 + j, (*chip, c), sibling) for j, chip in enumerate(chips)]
        for j, chip in enumerate(chips):
            copy(1 + j, (*chip, c), me).wait_recv()
            passed[j].start()
        copy(0, sibling, me).wait_recv()
        for j, chip in enumerate(chips):
            copy(4 + j, (*chip, 1 - c), me).wait_recv()
        for cp in first + passed:
            cp.wait_send()
        mine.wait()

    return pl.pallas_call(
        body, name=name, out_shape=jax.ShapeDtypeStruct((8 * m, n), blk.dtype),
        in_specs=[HBM_SPEC], out_specs=HBM_SPEC,
        scratch_shapes=[pltpu.SemaphoreType.DMA((7,)), pltpu.SemaphoreType.DMA((7,)), pltpu.SemaphoreType.DMA],
    )(blk)


def _exchange8(name, g):
    _, m, n = g.shape

    def body(g_ref, out_ref, send_sems, recv_sems, local_sem):
        x, y, c = _place()
        me_id = 4 * x + 2 * y + c
        mine = pltpu.make_async_copy(g_ref.at[me_id], out_ref.at[me_id], local_sem)
        mine.start()
        sends, recvs = [], []
        for k in range(1, 8):
            px = 1 - x if k & 4 else x
            py = 1 - y if k & 2 else y
            pc = 1 - c if k & 1 else c
            pid = 4 * px + 2 * py + pc
            sends.append(pltpu.make_async_remote_copy(
                src_ref=g_ref.at[pid], dst_ref=out_ref.at[me_id], send_sem=send_sems.at[k - 1],
                recv_sem=recv_sems.at[k - 1], device_id=(px, py, pc), device_id_type=MESH))
            recvs.append(pltpu.make_async_remote_copy(
                src_ref=g_ref.at[me_id], dst_ref=out_ref.at[pid], send_sem=send_sems.at[k - 1],
                recv_sem=recv_sems.at[k - 1], device_id=(px, py, pc), device_id_type=MESH))
        for cp in sends:
            cp.start()
        for cp in recvs:
            cp.wait_recv()
        for cp in sends:
            cp.wait_send()
        mine.wait()

    return pl.pallas_call(
        body, name=name, out_shape=jax.ShapeDtypeStruct(g.shape, g.dtype), in_specs=[HBM_SPEC], out_specs=HBM_SPEC,
        scratch_shapes=[pltpu.SemaphoreType.DMA((7,)), pltpu.SemaphoreType.DMA((7,)), pltpu.SemaphoreType.DMA],
    )(g)


def _pairshare(name, half):
    m, n = half.shape

    def body(x_ref, out_ref, send_sem, recv_sem, local_sem):
        x, y, c = _place()
        mine = pltpu.make_async_copy(x_ref, out_ref.at[pl.ds(c * m, m), :], local_sem)
        mine.start()
        send = pltpu.make_async_remote_copy(
            src_ref=x_ref, dst_ref=out_ref.at[pl.ds(c * m, m), :], send_sem=send_sem, recv_sem=recv_sem,
            device_id=(x, y, 1 - c), device_id_type=MESH)
        send.start()
        pltpu.make_async_remote_copy(
            src_ref=x_ref, dst_ref=out_ref.at[pl.ds((1 - c) * m, m), :], send_sem=send_sem, recv_sem=recv_sem,
            device_id=(x, y, 1 - c), device_id_type=MESH).wait_recv()
        send.wait_send()
        mine.wait()

    return pl.pallas_call(
        body, name=name, out_shape=jax.ShapeDtypeStruct((2 * m, n), half.dtype), in_specs=[HBM_SPEC], out_specs=HBM_SPEC,
        scratch_shapes=[pltpu.SemaphoreType.DMA, pltpu.SemaphoreType.DMA, pltpu.SemaphoreType.DMA],
    )(half)


def _sum8(name, parts):
    _, m, n = parts.shape
    tr = _tile(m, 256, 16)

    def body(p_ref, o_ref):
        acc = p_ref[0].astype(F32)
        for s in range(1, 8):
            acc = acc + p_ref[s].astype(F32)
        o_ref[...] = acc

    return pl.pallas_call(
        body, name=name, grid=(m // tr,), in_specs=[pl.BlockSpec((8, tr, n), lambda i: (0, i, 0))],
        out_specs=pl.BlockSpec((tr, n), lambda i: (i, 0)), out_shape=jax.ShapeDtypeStruct((m, n), F32),
        compiler_params=_params("parallel"),
    )(parts)


def _adamw(name, w, g, m, v):
    shape = w.shape
    cols = shape[-1]
    rows = math.prod(shape[:-1])
    tr = _tile(rows, 512, 8) if rows * cols > 2 ** 19 else rows

    def body(w_ref, g_ref, m_ref, v_ref, d_ref, mo_ref, vo_ref):
        gv = g_ref[...]
        m2 = ADAM_B1 * m_ref[...] + (1.0 - ADAM_B1) * gv
        v2 = ADAM_B2 * v_ref[...] + (1.0 - ADAM_B2) * (gv * gv)
        m_hat = m2 / (1.0 - ADAM_B1 ** ADAM_STEP)
        v_hat = v2 / (1.0 - ADAM_B2 ** ADAM_STEP)
        d_ref[...] = -ADAM_LR * (m_hat / (jnp.sqrt(v_hat) + ADAM_EPS) + ADAM_WD * w_ref[...])
        mo_ref[...] = m2
        vo_ref[...] = v2

    spec = pl.BlockSpec((tr, cols), lambda i: (i, 0))
    outs = pl.pallas_call(
        body, name=name, grid=(rows // tr,), in_specs=[spec] * 4, out_specs=[spec] * 3,
        out_shape=[jax.ShapeDtypeStruct((rows, cols), F32)] * 3, compiler_params=_params("parallel"),
    )(*(a.reshape(rows, cols) for a in (w, g, m, v)))
    return tuple(o.reshape(shape) for o in outs)


WEIGHTS = ["meta_tokens", "norm_g", "ffn_w1", "ffn_w3", "ffn_w2", "ssm_in_proj", "ssm_conv_w", "ssm_conv_b", "ssm_dt_bias",
           "ssm_a_log", "ssm_d", "ssm_norm_g", "ssm_out_proj", "kv_norm_g", "w_k", "k_norm_g", "w_v", "sb_w_q",
           "sb_q_norm_g", "sb_w_o"]
SHARD_AXIS = {"meta_tokens": 1, "norm_g": 2, "ffn_w1": 3, "ffn_w3": 3, "ffn_w2": 2, "ssm_in_proj": 2, "ssm_conv_w": 2,
              "ssm_conv_b": 1, "ssm_norm_g": 1, "ssm_out_proj": 1, "w_k": 0, "w_v": 0, "sb_w_q": 1, "sb_w_o": 1}
MATRICES = ["ffn_w1", "ffn_w3", "ffn_w2", "ssm_in_proj", "ssm_out_proj", "w_k", "w_v", "sb_w_q", "sb_w_o"]
VECTORS = [n for n in WEIGHTS if n in SHARD_AXIS and n not in MATRICES]
REPLICATED = [n for n in WEIGHTS if n not in SHARD_AXIS]
LAYER_AXIS = ("ssm_", "sb_")
PACK_COLS = 1024
N_CHIPS = 4


def _pack(arrays, row_mult, dtype):
    segs = []
    for a in arrays:
        n = math.prod(a.shape)
        r = -(-n // PACK_COLS)
        flat = a.reshape(-1).astype(dtype)
        if r * PACK_COLS != n:
            flat = jnp.pad(flat, (0, r * PACK_COLS - n))
        segs.append(flat.reshape(r, PACK_COLS))
    rows = sum(s.shape[0] for s in segs)
    extra = -rows % row_mult
    if extra:
        segs.append(jnp.zeros((extra, PACK_COLS), dtype))
    return jnp.concatenate(segs, axis=0)


def _unpack(packed, shapes):
    lead = packed.shape[:-2]
    out, r0 = [], 0
    for shp in shapes:
        n = math.prod(shp)
        r = -(-n // PACK_COLS)
        seg = packed[..., r0:r0 + r, :]
        if r * PACK_COLS != n:
            seg = seg.reshape(*lead, r * PACK_COLS)[..., :n]
        out.append(seg.reshape(*lead, *shp))
        r0 += r
    return out


def _join(stack, axis):
    return jnp.concatenate([stack[s] for s in range(N_CHIPS)], axis=axis)


def _shards(full, axis):
    n = full.shape[axis] // N_CHIPS
    return [lax.slice_in_dim(full, s * n, (s + 1) * n, axis=axis) for s in range(N_CHIPS)]


def _drop_layer(name, a):
    return a[0] if name.startswith(LAYER_AXIS) else a


def kernel(x, meta_tokens, norm_g, ffn_w1, ffn_w3, ffn_w2, ssm_in_proj, ssm_conv_w, ssm_conv_b, ssm_dt_bias, ssm_a_log, ssm_d, ssm_norm_g, ssm_out_proj, kv_norm_g, w_k, k_norm_g, w_v, sb_w_q, sb_q_norm_g, sb_w_o, loss_target, m_meta_tokens, m_norm_g, m_ffn_w1, m_ffn_w3, m_ffn_w2, m_ssm_in_proj, m_ssm_conv_w, m_ssm_conv_b, m_ssm_dt_bias, m_ssm_a_log, m_ssm_d, m_ssm_norm_g, m_ssm_out_proj, m_kv_norm_g, m_w_k, m_k_norm_g, m_w_v, m_sb_w_q, m_sb_q_norm_g, m_sb_w_o, v_meta_tokens, v_norm_g, v_ffn_w1, v_ffn_w3, v_ffn_w2, v_ssm_in_proj, v_ssm_conv_w, v_ssm_conv_b, v_ssm_dt_bias, v_ssm_a_log, v_ssm_d, v_ssm_norm_g, v_ssm_out_proj, v_kv_norm_g, v_w_k, v_k_norm_g, v_w_v, v_sb_w_q, v_sb_q_norm_g, v_sb_w_o):
    args = locals()
    w_in = {n: args[n] for n in WEIGHTS}
    m_in = {n: args["m_" + n] for n in WEIGHTS}
    v_in = {n: args["v_" + n] for n in WEIGHTS}
    c = lax.axis_index("c")

    def gather(names, dtype, row_mult):
        packed = _pack([w_in[n] for n in names], 2 * row_mult, dtype)
        half = packed.shape[0] // 2
        mine = lax.dynamic_slice_in_dim(packed, c * half, half, axis=0)
        allp = _allgather8(f"gather_{jnp.dtype(dtype).name}", mine).reshape(N_CHIPS, 2 * half, PACK_COLS)
        stacks = _unpack(allp, [w_in[n].shape for n in names])
        return {n: _join(s, SHARD_AXIS[n]) for n, s in zip(names, stacks)}

    full = {**gather(MATRICES, BF16, 16), **gather(VECTORS, F32, 8)}
    full.update({n: w_in[n] for n in REPLICATED})
    full = {n: _drop_layer(n, a) for n, a in full.items()}

    sq, grad_x, grads = _local_step(x[0], loss_target[0], full)
    loss = lax.psum(0.5 / x.shape[-1] * jnp.sum(sq), ("x", "y", "c"))
    grads = {n: (g[None] if n.startswith(LAYER_AXIS) else g) for n, g in grads.items()}

    sharded = MATRICES + VECTORS
    per_chip = [[] for _ in range(N_CHIPS)]
    for n in sharded:
        for p, s in enumerate(_shards(grads[n], SHARD_AXIS[n])):
            per_chip[p].append(s)
    contrib = jnp.stack([_pack(parts, 32, BF16) for parts in per_chip])
    rows = contrib.shape[1]
    got = _exchange8("grad_exchange", contrib.reshape(2 * N_CHIPS, rows // 2, PACK_COLS))
    reduced = _pairshare("grad_share", _sum8("grad_sum", got))
    g_out = dict(zip(sharded, _unpack(reduced, [w_in[n].shape for n in sharded])))

    rep = _pack([grads[n] for n in REPLICATED], 8, F32)
    rep_sum = _sum8("rep_sum", _allgather8("rep_gather", rep).reshape(8, rep.shape[0], PACK_COLS))
    g_out.update(zip(REPLICATED, _unpack(rep_sum, [w_in[n].shape for n in REPLICATED])))

    delta, new_m, new_v = {}, {}, {}
    for n in WEIGHTS:
        delta[n], new_m[n], new_v[n] = _adamw(f"adamw_{n}", w_in[n], g_out[n], m_in[n], v_in[n])
    return (loss, grad_x[None], *[g_out[n] for n in WEIGHTS], *[delta[n] for n in WEIGHTS],
            *[new_m[n] for n in WEIGHTS], *[new_v[n] for n in WEIGHTS])
```

```python
import functools
import math

import jax
import jax.numpy as jnp
from jax import lax
from jax.experimental import pallas as pl
from jax.experimental.pallas import tpu as pltpu

F32, BF16 = jnp.float32, jnp.bfloat16
RMS_EPS = 1e-6
LANES = 128
HEAD = 64
D_STATE = 128
BLK = 128
FFN_RES = 0.5
VMEM_LIMIT = 56 * 2 ** 20
ADAM_LR, ADAM_B1, ADAM_B2, ADAM_EPS, ADAM_WD, ADAM_STEP = 0.001, 0.9, 0.999, 1e-08, 0.01, 10
MESH = pl.DeviceIdType.MESH

NN = (((1,), (0,)), ((), ()))
NT = (((1,), (1,)), ((), ()))
TN = (((0,), (0,)), ((), ()))


def _dot(a, b, dn=NN):
    return lax.dot_general(a, b, dn, preferred_element_type=F32)


def _split(x, parts):
    out = []
    for _ in range(parts):
        p = x.astype(BF16)
        out.append(p)
        x = x - p.astype(F32)
    return out


def _dotx(a, b, dn=NN, parts=3, split="a"):
    if split == "a":
        return sum(_dot(p, b, dn) for p in _split(a, parts))
    return sum(_dot(a, p, dn) for p in _split(b, parts))


def _tile(n, target, mult):
    best = None
    for d in range(mult, min(n, target) + 1, mult):
        if n % d == 0:
            best = d
    return n if best is None else best


def _params(*sem):
    return pltpu.CompilerParams(dimension_semantics=tuple(sem) if sem else None, vmem_limit_bytes=VMEM_LIMIT)


def _iota(shape, axis):
    return lax.broadcasted_iota(jnp.int32, shape, axis)


def _sigmoid(x):
    return 1.0 / (1.0 + jnp.exp(-x))


def _matmul(name, pairs, mode, tm, tn, tk, out_dtypes, epilogue=None, extras=(), separate=False):
    a0, b0 = pairs[0]
    if mode == "nn":
        (M, K), N = a0.shape, b0.shape[1]
    elif mode == "nt":
        (M, K), N = a0.shape, b0.shape[0]
    else:
        (K, M), N = a0.shape, b0.shape[1]
    assert M % tm == 0 and N % tn == 0 and K % tk == 0, (name, M, N, K, tm, tn, tk)
    nM, nN, nK = M // tm, N // tn, K // tk
    np_, ne, no = len(pairs), len(extras), len(out_dtypes)
    n_acc = np_ if separate else 1
    dn = {"nn": NN, "nt": NT, "tn": TN}[mode]

    def body(*refs):
        ab, ex = refs[:2 * np_], refs[2 * np_:2 * np_ + ne]
        outs, accs = refs[2 * np_ + ne:2 * np_ + ne + no], refs[2 * np_ + ne + no:]
        k = pl.program_id(2)

        def prod(i):
            return _dot(ab[2 * i][...].astype(BF16), ab[2 * i + 1][...].astype(BF16), dn)

        ps = [prod(i) for i in range(np_)]
        if not separate:
            ps = [functools.reduce(lambda u, v: u + v, ps)]

        def finish(vals):
            res = epilogue(vals, [e[...] for e in ex]) if epilogue is not None else vals
            for o, r in zip(outs, res):
                o[...] = r.astype(o.dtype)

        if nK == 1:
            finish(ps)
        else:
            @pl.when(k == 0)
            def _():
                for acc, p in zip(accs, ps):
                    acc[...] = p

            @pl.when(k > 0)
            def _():
                for acc, p in zip(accs, ps):
                    acc[...] += p

            @pl.when(k == nK - 1)
            def _():
                finish([acc[...] for acc in accs])

    if mode == "tn":
        a_spec = pl.BlockSpec((tk, tm), lambda n, m, k: (k, m))
    else:
        a_spec = pl.BlockSpec((tm, tk), lambda n, m, k: (m, k))
    if mode == "nt":
        b_spec = pl.BlockSpec((tn, tk), lambda n, m, k: (n, k))
    else:
        b_spec = pl.BlockSpec((tk, tn), lambda n, m, k: (k, n))
    in_specs, args = [], []
    for a, b in pairs:
        in_specs += [a_spec, b_spec]
        args += [a, b]
    for arr, kind in extras:
        if kind == "mn":
            in_specs.append(pl.BlockSpec((tm, tn), lambda n, m, k: (m, n)))
        else:
            in_specs.append(pl.BlockSpec((1, tn), lambda n, m, k: (0, n)))
        args.append(arr)
    out_specs = [pl.BlockSpec((tm, tn), lambda n, m, k: (m, n)) for _ in out_dtypes]
    res = pl.pallas_call(
        body, name=name, grid=(nN, nM, nK), in_specs=in_specs, out_specs=out_specs,
        out_shape=[jax.ShapeDtypeStruct((M, N), d) for d in out_dtypes],
        scratch_shapes=[pltpu.VMEM((tm, tn), F32) for _ in range(n_acc)] if nK > 1 else [],
        compiler_params=_params("parallel", "parallel", "arbitrary"),
    )(*args)
    return res


def _rms_fwd(name, h, g):
    L, D = h.shape
    tr = _tile(L, 1024, 16)

    def body(h_ref, g_ref, o_ref):
        x = h_ref[...]
        r = lax.rsqrt(jnp.mean(x * x, axis=-1, keepdims=True) + RMS_EPS)
        o_ref[...] = (x * r * g_ref[...]).astype(BF16)

    return pl.pallas_call(
        body, name=name, grid=(L // tr,),
        in_specs=[pl.BlockSpec((tr, D), lambda i: (i, 0)), pl.BlockSpec((1, D), lambda i: (0, 0))],
        out_specs=pl.BlockSpec((tr, D), lambda i: (i, 0)),
        out_shape=jax.ShapeDtypeStruct((L, D), BF16), compiler_params=_params("parallel"),
    )(h, g.reshape(1, D))


def _rms_bwd(name, dxn, h, g, dres):
    L, D = h.shape
    tr = _tile(L, 512, 8)

    def body(dxn_ref, h_ref, g_ref, dres_ref, dh_ref, dg_ref):
        x = h_ref[...]
        r = lax.rsqrt(jnp.mean(x * x, axis=-1, keepdims=True) + RMS_EPS)
        xh = x * r
        dxn = dxn_ref[...]
        dxh = dxn * g_ref[...]
        dh_ref[...] = dres_ref[...] + r * (dxh - xh * jnp.mean(dxh * xh, axis=-1, keepdims=True))

        @pl.when(pl.program_id(0) == 0)
        def _():
            dg_ref[...] = jnp.zeros_like(dg_ref)

        dg_ref[...] += jnp.sum(dxn * xh, axis=0, keepdims=True)

    row = pl.BlockSpec((tr, D), lambda i: (i, 0))
    vec = pl.BlockSpec((1, D), lambda i: (0, 0))
    return pl.pallas_call(
        body, name=name, grid=(L // tr,), in_specs=[row, row, vec, row], out_specs=[row, vec],
        out_shape=[jax.ShapeDtypeStruct((L, D), F32), jax.ShapeDtypeStruct((1, D), F32)],
        compiler_params=_params("arbitrary"),
    )(dxn, h, g.reshape(1, D), dres)


def _head_sums(x2):
    blockdiag = (_iota((LANES, LANES), 0) // HEAD == _iota((LANES, LANES), 1) // HEAD).astype(BF16)
    cols = [_dotx(x2[:, j:j + LANES], blockdiag) for j in range(0, x2.shape[1], LANES)]
    return jnp.concatenate(cols, axis=1) if len(cols) > 1 else cols[0]


def _headrms_fwd(name, raw, g):
    L, D = raw.shape
    tr = _tile(L, 512, 16)

    def body(x_ref, g_ref, o_ref):
        x = x_ref[...]
        r = lax.rsqrt(_head_sums(x * x) * (1.0 / HEAD) + RMS_EPS)
        o_ref[...] = (x * r * g_ref[...]).astype(BF16)

    return pl.pallas_call(
        body, name=name, grid=(L // tr,),
        in_specs=[pl.BlockSpec((tr, D), lambda i: (i, 0)), pl.BlockSpec((1, D), lambda i: (0, 0))],
        out_specs=pl.BlockSpec((tr, D), lambda i: (i, 0)),
        out_shape=jax.ShapeDtypeStruct((L, D), BF16), compiler_params=_params("parallel"),
    )(raw, g)


def _headrms_bwd(name, dy, raw, g):
    L, D = raw.shape
    tr = _tile(L, 512, 16)

    def body(dy_ref, x_ref, g_ref, dx_ref, dg_ref):
        x = x_ref[...]
        dy = dy_ref[...]
        r = lax.rsqrt(_head_sums(x * x) * (1.0 / HEAD) + RMS_EPS)
        xh = x * r
        dxh = dy * g_ref[...]
        dx_ref[...] = (r * (dxh - xh * (_head_sums(dxh * xh) * (1.0 / HEAD)))).astype(BF16)

        @pl.when(pl.program_id(0) == 0)
        def _():
            dg_ref[...] = jnp.zeros_like(dg_ref)

        dg_ref[...] += jnp.sum(dy * xh, axis=0, keepdims=True)

    row = pl.BlockSpec((tr, D), lambda i: (i, 0))
    vec = pl.BlockSpec((1, D), lambda i: (0, 0))
    return pl.pallas_call(
        body, name=name, grid=(L // tr,), in_specs=[row, row, vec], out_specs=[row, vec],
        out_shape=[jax.ShapeDtypeStruct((L, D), BF16), jax.ShapeDtypeStruct((1, D), F32)],
        compiler_params=_params("arbitrary"),
    )(dy, raw, g)


def _loss(name, h, tgt, pad_rows):
    L, D = h.shape
    nb = L // BLK
    assert pad_rows == BLK

    def body(h_ref, t_ref, dh_ref, s_ref):
        i = pl.program_id(0)

        @pl.when(i == 0)
        def _():
            s_ref[...] = jnp.zeros_like(s_ref)
            dh_ref[...] = jnp.zeros_like(dh_ref)

        @pl.when(i > 0)
        def _():
            e = h_ref[...] - t_ref[...]
            dh_ref[...] = e * (1.0 / D)
            s_ref[...] += jnp.sum(e * e, axis=0, keepdims=True)

    return pl.pallas_call(
        body, name=name, grid=(nb,),
        in_specs=[pl.BlockSpec((BLK, D), lambda i: (i, 0)), pl.BlockSpec((BLK, D), lambda i: (jnp.maximum(i - 1, 0), 0))],
        out_specs=[pl.BlockSpec((BLK, D), lambda i: (i, 0)), pl.BlockSpec((1, D), lambda i: (0, 0))],
        out_shape=[jax.ShapeDtypeStruct((L, D), F32), jax.ShapeDtypeStruct((1, D), F32)],
        compiler_params=_params("arbitrary"),
    )(h, tgt)


def _swiglu_up(name, xn, w1, w3):
    L, D = xn.shape
    Fd = w1.shape[1]
    tm, tn = _tile(L, 704, 16), _tile(Fd, 1408, LANES)

    def epi(accs, _):
        a, b = accs
        return [a, b, a * _sigmoid(a) * b]

    return _matmul(name, [(xn, w1), (xn, w3)], "nn", tm, tn, D, [BF16, BF16, BF16], epilogue=epi, separate=True)


def _swiglu_bwd(name, dh, w2, a, b):
    L, D = dh.shape
    Fd = w2.shape[0]
    tm, tn = _tile(L, 704, 16), _tile(Fd, 1408, LANES)

    def epi(accs, ex):
        dact = accs[0] * FFN_RES
        av, bv = ex[0].astype(F32), ex[1].astype(F32)
        s = _sigmoid(av)
        return [dact * bv * (s * (1.0 + av * (1.0 - s))), dact * av * s]

    return _matmul(name, [(dh, w2)], "nt", tm, tn, D, [BF16, BF16], epilogue=epi, extras=[(a, "mn"), (b, "mn")])


def _ffn_fwd(tag, h, g, w1, w3, w2):
    L, D = h.shape
    xn = _rms_fwd(f"{tag}_norm", h, g)
    a, b, act = _swiglu_up(f"{tag}_up", xn, w1, w3)
    tm = _tile(L, 704, 8)
    (h_out,) = _matmul(f"{tag}_down", [(act, w2)], "nn", tm, D, w2.shape[0], [F32],
                       epilogue=lambda accs, ex: [ex[0] + FFN_RES * accs[0]], extras=[(h, "mn")])
    return h_out, (xn, a, b, act)


def _ffn_bwd(tag, dh_out, h, g, w1, w3, w2, saved):
    xn, a, b, act = saved
    L, D = h.shape
    Fd = w2.shape[0]
    tl = _tile(L, 704, 16)
    da, db = _swiglu_bwd(f"{tag}_dact", dh_out, w2, a, b)
    (dw2,) = _matmul(f"{tag}_dw2", [(act, dh_out)], "tn", _tile(Fd, 1408, LANES), D, tl, [F32],
                     epilogue=lambda accs, ex: [FFN_RES * accs[0]])
    (dw1,) = _matmul(f"{tag}_dw1", [(xn, da)], "tn", D, _tile(Fd, 1408, LANES), tl, [F32])
    (dw3,) = _matmul(f"{tag}_dw3", [(xn, db)], "tn", D, _tile(Fd, 1408, LANES), tl, [F32])
    (dxn,) = _matmul(f"{tag}_dxn", [(da, w1), (db, w3)], "nt", _tile(L, 704, 8), _tile(D, 512, LANES), Fd, [F32])
    dh, dg = _rms_bwd(f"{tag}_dnorm", dxn, h, g, dh_out)
    return dh, dg.reshape(-1), dw1, dw3, dw2


def _conv_taps(ext, k):
    return ext if k == 0 else pltpu.roll(ext, k, axis=0)


def _conv_fwd(name, zx, col0, w, b, pad):
    L = zx.shape[0]
    C = w.shape[1]
    tr, tc = _tile(L, 704, 8), _tile(C, 512, LANES)
    cb = col0 // tc
    assert col0 % tc == 0

    def body(u_ref, halo_ref, w_ref, b_ref, o_ref):
        ext = jnp.concatenate([halo_ref[...], u_ref[...]], axis=0)
        pre = b_ref[...] + sum(_conv_taps(ext, 3 - k)[8:] * w_ref[k:k + 1, :] for k in range(4))
        rows = _iota(pre.shape, 0) + pl.program_id(1) * tr
        o_ref[...] = jnp.where(rows >= pad, pre * _sigmoid(pre), 0.0)

    return pl.pallas_call(
        body, name=name, grid=(C // tc, L // tr),
        in_specs=[pl.BlockSpec((tr, tc), lambda j, i: (i, cb + j)),
                  pl.BlockSpec((8, tc), lambda j, i: (jnp.maximum(i * (tr // 8) - 1, 0), cb + j)),
                  pl.BlockSpec((4, tc), lambda j, i: (0, j)), pl.BlockSpec((1, tc), lambda j, i: (0, j))],
        out_specs=pl.BlockSpec((tr, tc), lambda j, i: (i, j)),
        out_shape=jax.ShapeDtypeStruct((L, C), F32), compiler_params=_params("parallel", "parallel"),
    )(zx, zx, w, b)


def _conv_bwd_pre(name, dact, zx, col0, w, b, pad):
    L = zx.shape[0]
    C = w.shape[1]
    tr, tc = _tile(L, 704, 8), _tile(C, 512, LANES)
    cb = col0 // tc

    def body(d_ref, u_ref, halo_ref, w_ref, b_ref, dp_ref, dw_ref, db_ref):
        ext = jnp.concatenate([halo_ref[...], u_ref[...]], axis=0)
        taps = [_conv_taps(ext, 3 - k)[8:] for k in range(4)]
        pre = b_ref[...] + sum(taps[k] * w_ref[k:k + 1, :] for k in range(4))
        s = _sigmoid(pre)
        rows = _iota(pre.shape, 0) + pl.program_id(1) * tr
        dpre = jnp.where(rows >= pad, d_ref[...] * (s * (1.0 + pre * (1.0 - s))), 0.0)
        dp_ref[...] = dpre

        @pl.when(pl.program_id(1) == 0)
        def _():
            dw_ref[...] = jnp.zeros_like(dw_ref)
            db_ref[...] = jnp.zeros_like(db_ref)

        db_ref[...] += jnp.sum(dpre, axis=0, keepdims=True)
        dw_ref[...] += jnp.concatenate([jnp.sum(dpre * taps[k], axis=0, keepdims=True) for k in range(4)], axis=0)

    return pl.pallas_call(
        body, name=name, grid=(C // tc, L // tr),
        in_specs=[pl.BlockSpec((tr, tc), lambda j, i: (i, j)),
                  pl.BlockSpec((tr, tc), lambda j, i: (i, cb + j)),
                  pl.BlockSpec((8, tc), lambda j, i: (jnp.maximum(i * (tr // 8) - 1, 0), cb + j)),
                  pl.BlockSpec((4, tc), lambda j, i: (0, j)), pl.BlockSpec((1, tc), lambda j, i: (0, j))],
        out_specs=[pl.BlockSpec((tr, tc), lambda j, i: (i, j)), pl.BlockSpec((4, tc), lambda j, i: (0, j)),
                   pl.BlockSpec((1, tc), lambda j, i: (0, j))],
        out_shape=[jax.ShapeDtypeStruct((L, C), F32), jax.ShapeDtypeStruct((4, C), F32), jax.ShapeDtypeStruct((1, C), F32)],
        compiler_params=_params("parallel", "arbitrary"),
    )(dact, zx, zx, w, b)


def _conv_bwd_in(name, dpre, w):
    L, C = dpre.shape
    tr, tc = _tile(L, 704, 16), _tile(C, 512, LANES)
    nr = L // tr

    def body(d_ref, halo_ref, w_ref, o_ref):
        halo = jnp.where(pl.program_id(1) == nr - 1, 0.0, halo_ref[...])
        ext = jnp.concatenate([d_ref[...], halo], axis=0)
        acc = ext[:tr] * w_ref[3:4, :]
        for k in range(3):
            acc = acc + pltpu.roll(ext, tr + 8 - (3 - k), axis=0)[:tr] * w_ref[k:k + 1, :]
        o_ref[...] = acc.astype(BF16)

    return pl.pallas_call(
        body, name=name, grid=(C // tc, nr),
        in_specs=[pl.BlockSpec((tr, tc), lambda j, i: (i, j)),
                  pl.BlockSpec((8, tc), lambda j, i: (jnp.minimum((i + 1) * (tr // 8), L // 8 - 1), j)),
                  pl.BlockSpec((4, tc), lambda j, i: (0, j))],
        out_specs=pl.BlockSpec((tr, tc), lambda j, i: (i, j)),
        out_shape=jax.ShapeDtypeStruct((L, C), BF16), compiler_params=_params("parallel", "parallel"),
    )(dpre, dpre, w)


def _dt_fwd(name, zx, col0, bias_row, nheads, pad):
    L = zx.shape[0]
    cb = col0 // LANES

    def body(x_ref, b_ref, dt_ref, dtt_ref):
        v = x_ref[...] + b_ref[...]
        sp = jnp.maximum(v, 0.0) + jnp.log(1.0 + jnp.exp(-jnp.abs(v)))
        rows = _iota(v.shape, 0) + pl.program_id(0) * BLK
        dt = jnp.where((rows >= pad) & (_iota(v.shape, 1) < nheads), sp, 0.0)
        dt_ref[...] = dt
        dtt_ref[...] = dt.T

    return pl.pallas_call(
        body, name=name, grid=(L // BLK,),
        in_specs=[pl.BlockSpec((BLK, LANES), lambda i: (i, cb)), pl.BlockSpec((1, LANES), lambda i: (0, 0))],
        out_specs=[pl.BlockSpec((BLK, LANES), lambda i: (i, 0)), pl.BlockSpec((LANES, BLK), lambda i: (0, i))],
        out_shape=[jax.ShapeDtypeStruct((L, LANES), F32), jax.ShapeDtypeStruct((LANES, L), F32)],
        compiler_params=_params("parallel"),
    )(zx, bias_row)


def _dt_bwd(name, ddt, zx, col0, bias_row, nheads, pad):
    L = zx.shape[0]
    cb = col0 // LANES

    def body(d_ref, x_ref, b_ref, o_ref, db_ref):
        v = x_ref[...] + b_ref[...]
        rows = _iota(v.shape, 0) + pl.program_id(0) * BLK
        g = jnp.where((rows >= pad) & (_iota(v.shape, 1) < nheads), d_ref[...] * _sigmoid(v), 0.0)
        o_ref[...] = g.astype(BF16)

        @pl.when(pl.program_id(0) == 0)
        def _():
            db_ref[...] = jnp.zeros_like(db_ref)

        db_ref[...] += jnp.sum(g, axis=0, keepdims=True)

    return pl.pallas_call(
        body, name=name, grid=(L // BLK,),
        in_specs=[pl.BlockSpec((BLK, LANES), lambda i: (i, 0)), pl.BlockSpec((BLK, LANES), lambda i: (i, cb)),
                  pl.BlockSpec((1, LANES), lambda i: (0, 0))],
        out_specs=[pl.BlockSpec((BLK, LANES), lambda i: (i, 0)), pl.BlockSpec((1, LANES), lambda i: (0, 0))],
        out_shape=[jax.ShapeDtypeStruct((L, LANES), BF16), jax.ShapeDtypeStruct((1, LANES), F32)],
        compiler_params=_params("arbitrary"),
    )(ddt, zx, bias_row)


def _ssd_common(dt, dtt, a_row, a_col):
    tril = (_iota((BLK, BLK), 0) >= _iota((BLK, BLK), 1)).astype(BF16)
    cum = _dotx(tril, dt * a_row, split="b")
    cumt = _dotx(dtt * a_col, tril, NT)
    return cum, cumt


def _ssd_fwd(name, xbc, dt, dtt, a_row, a_col, expand, di, ng):
    L = xbc.shape[0]
    nc = L // BLK
    hpg = di // HEAD // ng
    gw = hpg * HEAD
    assert gw % LANES == 0

    def body(x_ref, dt_ref, dtt_ref, ar_ref, ac_ref, ex_ref, y_ref, st_ref, h_ref):
        @pl.when(pl.program_id(0) == 0)
        def _():
            h_ref[...] = jnp.zeros_like(h_ref)

        st_ref[0] = h_ref[...]
        dt, dtt = dt_ref[...], dtt_ref[...]
        cum, cumt = _ssd_common(dt, dtt, ar_ref[...], ac_ref[...])
        ex = ex_ref[...]
        ecum_x = _dotx(jnp.exp(cum), ex)
        wend_x = _dotx(jnp.exp(cum[BLK - 1:BLK, :] - cum) * dt, ex)
        ecl = jnp.broadcast_to(jnp.exp(cumt[:, BLK - 1:BLK]), (LANES, LANES))
        decay_h = _dotx(ex, ecl, TN, split="b")
        causal = _iota((BLK, BLK), 0) >= _iota((BLK, BLK), 1)
        low = _iota((BLK, LANES), 1) < HEAD
        for g in range(ng):
            xg = x_ref[:, g * gw:(g + 1) * gw]
            bg = x_ref[:, di + g * D_STATE:di + (g + 1) * D_STATE].astype(BF16)
            cg = x_ref[:, di + (ng + g) * D_STATE:di + (ng + g + 1) * D_STATE].astype(BF16)
            hg = h_ref[g * gw:(g + 1) * gw, :]
            gram = _dot(cg, bg, NT)
            yoff = _dot(cg, hg.astype(BF16), NT) * ecum_x[:, g * gw:(g + 1) * gw]
            parts = []
            for j in range(gw // LANES):
                xp = xg[:, j * LANES:(j + 1) * LANES].astype(BF16)
                yd = []
                for hh in range(2):
                    h = g * hpg + 2 * j + hh
                    seg = cum[:, h:h + 1] - cumt[h:h + 1, :]
                    m = gram * jnp.where(causal, jnp.exp(jnp.minimum(seg, 0.0)), 0.0) * dtt[h:h + 1, :]
                    yd.append(_dot(m.astype(BF16), xp))
                parts.append(jnp.where(low, yd[0], yd[1]))
            y_ref[:, g * gw:(g + 1) * gw] = jnp.concatenate(parts, axis=1) + yoff
            xw = (xg * wend_x[:, g * gw:(g + 1) * gw]).astype(BF16)
            h_ref[g * gw:(g + 1) * gw, :] = hg * decay_h[g * gw:(g + 1) * gw, :] + _dot(xw, bg, TN)

    W = xbc.shape[1]
    full = lambda r, c: pl.BlockSpec((r, c), lambda i: (0, 0))
    return pl.pallas_call(
        body, name=name, grid=(nc,),
        in_specs=[pl.BlockSpec((BLK, W), lambda i: (i, 0)), pl.BlockSpec((BLK, LANES), lambda i: (i, 0)),
                  pl.BlockSpec((LANES, BLK), lambda i: (0, i)), full(1, LANES), full(LANES, LANES), full(LANES, di)],
        out_specs=[pl.BlockSpec((BLK, di), lambda i: (i, 0)), pl.BlockSpec((1, di, D_STATE), lambda i: (i, 0, 0))],
        out_shape=[jax.ShapeDtypeStruct((L, di), F32), jax.ShapeDtypeStruct((nc, di, D_STATE), F32)],
        scratch_shapes=[pltpu.VMEM((di, D_STATE), F32)], compiler_params=_params("arbitrary"),
    )(xbc, dt, dtt, a_row, a_col, expand)


def _ssd_bwd(name, xbc, dt, dtt, a_row, a_col, expand, states, dy, d_x, di, ng):
    L, W = xbc.shape
    nc = L // BLK
    hpg = di // HEAD // ng
    gw = hpg * HEAD

    def body(x_ref, dt_ref, dtt_ref, ar_ref, ac_ref, ex_ref, st_ref, dy_ref, dx_ref_in, dxo_ref, ddt_ref, da_ref, dh_ref):
        @pl.when(pl.program_id(0) == 0)
        def _():
            dh_ref[...] = jnp.zeros_like(dh_ref)
            da_ref[...] = jnp.zeros_like(da_ref)

        dt, dtt, a_row = dt_ref[...], dtt_ref[...], ar_ref[...]
        cum, cumt = _ssd_common(dt, dtt, a_row, ac_ref[...])
        ex = ex_ref[...]
        ecum = jnp.exp(cum)
        ecum_x = _dotx(ecum, ex)
        e_s = jnp.exp(cum[BLK - 1:BLK, :] - cum)
        wend_x = _dotx(e_s * dt, ex)
        ecl_col = jnp.exp(cumt[:, BLK - 1:BLK])
        decay_h = _dotx(ex, jnp.broadcast_to(ecl_col, (LANES, LANES)), TN, split="b")
        causal = _iota((BLK, BLK), 0) >= _iota((BLK, BLK), 1)
        low = _iota((BLK, LANES), 1) < HEAD
        lane = _iota((1, LANES), 1)
        sub = _iota((LANES, 1), 0)
        dcum_c = jnp.zeros((BLK, LANES), F32)
        dcum_r = jnp.zeros((LANES, BLK), F32)
        ddt_r = jnp.zeros((LANES, BLK), F32)
        zoff = []
        dwend_src = []
        for g in range(ng):
            gs = slice(g * gw, (g + 1) * gw)
            xg = x_ref[:, gs]
            bg = x_ref[:, di + g * D_STATE:di + (g + 1) * D_STATE].astype(BF16)
            cg = x_ref[:, di + (ng + g) * D_STATE:di + (ng + g + 1) * D_STATE].astype(BF16)
            hprev = st_ref[0, gs, :]
            dhn = dh_ref[gs, :]
            dyg = dy_ref[:, gs]
            gram = _dot(cg, bg, NT)
            dgram = jnp.zeros((BLK, BLK), F32)
            dxg = []
            for j in range(gw // LANES):
                xp = xg[:, j * LANES:(j + 1) * LANES].astype(BF16)
                dyp = dyg[:, j * LANES:(j + 1) * LANES]
                dxh = []
                for hh in range(2):
                    h = g * hpg + 2 * j + hh
                    seg = cum[:, h:h + 1] - cumt[h:h + 1, :]
                    lm = jnp.where(causal, jnp.exp(jnp.minimum(seg, 0.0)), 0.0)
                    dtr = dtt[h:h + 1, :]
                    m = gram * lm * dtr
                    dym = jnp.where(low if hh == 0 else ~low, dyp, 0.0).astype(BF16)
                    dxh.append(_dot(m.astype(BF16), dym, TN))
                    dm = _dot(dym, xp, NT)
                    dgram = dgram + dm * lm * dtr
                    v = dm * gram * lm
                    wv = v * dtr
                    ddt_r = ddt_r + jnp.where(sub == h, jnp.sum(v, axis=0, keepdims=True), 0.0)
                    dcum_r = dcum_r - jnp.where(sub == h, jnp.sum(wv, axis=0, keepdims=True), 0.0)
                    dcum_c = dcum_c + jnp.where(lane == h, jnp.sum(wv, axis=1, keepdims=True), 0.0)
                dxg.append(jnp.where(low, dxh[0], dxh[1]))
            dx_diag = jnp.concatenate(dxg, axis=1)
            hb = hprev.astype(BF16)
            yoff = _dot(cg, hb, NT) * ecum_x[:, gs]
            dye = (dyg * ecum_x[:, gs]).astype(BF16)
            dcg = _dot(dye, hb) + _dot(dgram.astype(BF16), bg)
            dbg = _dot(dgram.astype(BF16), cg, TN)
            dh_prev = _dot(dye, cg, TN)
            zoff.append(dyg * yoff)
            dhb = dhn.astype(BF16)
            dxw = _dot(bg, dhb, NT)
            xw = (xg * wend_x[:, gs]).astype(BF16)
            dbg = dbg + _dot(xw, dhb)
            dwend_src.append(dxw * xg)
            dxo_ref[:, gs] = dx_diag + dxw * wend_x[:, gs] + dyg * dx_ref_in[:, gs]
            dxo_ref[:, di + g * D_STATE:di + (g + 1) * D_STATE] = dbg
            dxo_ref[:, di + (ng + g) * D_STATE:di + (ng + g + 1) * D_STATE] = dcg
            prod = dhn * hprev
            dd = jnp.sum(_dotx(ex[:, gs], prod, split="b"), axis=1, keepdims=True)
            dcum_r = dcum_r + jnp.where(_iota((1, BLK), 1) == BLK - 1, dd * ecl_col, 0.0)
            dh_ref[gs, :] = dhn * decay_h[gs, :] + dh_prev
        dcum_c = dcum_c + _dotx(jnp.concatenate(zoff, axis=1), ex, NT)
        dwend = _dotx(jnp.concatenate(dwend_src, axis=1), ex, NT)
        ddt_c = dwend * e_s
        de = dwend * dt * e_s
        dcum_c = dcum_c - de + jnp.where(_iota((BLK, 1), 0) == BLK - 1, jnp.sum(de, axis=0, keepdims=True), 0.0)
        dcum = dcum_c + dcum_r.T
        triu = (_iota((BLK, BLK), 0) <= _iota((BLK, BLK), 1)).astype(BF16)
        da = _dotx(triu, dcum, split="b")
        ddt_ref[...] = ddt_c + ddt_r.T + da * a_row
        da_ref[...] += jnp.sum(da * dt, axis=0, keepdims=True)

    rev = lambda i: nc - 1 - i
    full = lambda r, c: pl.BlockSpec((r, c), lambda i: (0, 0))
    return pl.pallas_call(
        body, name=name, grid=(nc,),
        in_specs=[pl.BlockSpec((BLK, W), lambda i: (rev(i), 0)), pl.BlockSpec((BLK, LANES), lambda i: (rev(i), 0)),
                  pl.BlockSpec((LANES, BLK), lambda i: (0, rev(i))), full(1, LANES), full(LANES, LANES), full(LANES, di),
                  pl.BlockSpec((1, di, D_STATE), lambda i: (rev(i), 0, 0)), pl.BlockSpec((BLK, di), lambda i: (rev(i), 0)),
                  full(1, di)],
        out_specs=[pl.BlockSpec((BLK, W), lambda i: (rev(i), 0)), pl.BlockSpec((BLK, LANES), lambda i: (rev(i), 0)),
                   full(1, LANES)],
        out_shape=[jax.ShapeDtypeStruct((L, W), F32), jax.ShapeDtypeStruct((L, LANES), F32),
                   jax.ShapeDtypeStruct((1, LANES), F32)],
        scratch_shapes=[pltpu.VMEM((di, D_STATE), F32)], compiler_params=_params("arbitrary"),
    )(xbc, dt, dtt, a_row, a_col, expand, states, dy, d_x)


def _group_sums(v, gsz):
    cols = []
    for j in range(0, v.shape[1], gsz):
        s = jnp.sum(v[:, j:j + gsz], axis=1, keepdims=True)
        cols.append(jnp.broadcast_to(s, (v.shape[0], gsz)))
    return jnp.concatenate(cols, axis=1)


def _gate_fwd(name, y, xbc, zx, d_x, ng_row, gsz):
    L, di = y.shape
    tr = _tile(L, 512, 16)

    def body(y_ref, x_ref, z_ref, d_ref, g_ref, o_ref):
        z = z_ref[...]
        y2 = (y_ref[...] + d_ref[...] * x_ref[...]) * (z * _sigmoid(z))
        r = lax.rsqrt(_group_sums(y2 * y2, gsz) * (1.0 / gsz) + RMS_EPS)
        o_ref[...] = (y2 * r * g_ref[...]).astype(BF16)

    row = pl.BlockSpec((tr, di), lambda i: (i, 0))
    vec = pl.BlockSpec((1, di), lambda i: (0, 0))
    return pl.pallas_call(
        body, name=name, grid=(L // tr,), in_specs=[row, row, row, vec, vec], out_specs=row,
        out_shape=jax.ShapeDtypeStruct((L, di), BF16), compiler_params=_params("parallel"),
    )(y, xbc, zx, d_x, ng_row)


def _gate_bwd(name, dy3, y, xbc, zx, d_x, ng_row, gsz):
    L, di = y.shape
    tr = _tile(L, 256, 16)

    def body(dy_ref, y_ref, x_ref, z_ref, d_ref, g_ref, dz_ref, dy1_ref, dg_ref, dd_ref):
        z, x = z_ref[...], x_ref[...]
        s = _sigmoid(z)
        sz = z * s
        y1 = y_ref[...] + d_ref[...] * x
        y2 = y1 * sz
        r = lax.rsqrt(_group_sums(y2 * y2, gsz) * (1.0 / gsz) + RMS_EPS)
        yg = y2 * r
        dy3 = dy_ref[...]
        dyg = dy3 * g_ref[...]
        dy2 = r * (dyg - yg * (_group_sums(dyg * yg, gsz) * (1.0 / gsz)))
        dz_ref[...] = (dy2 * y1 * (s * (1.0 + z * (1.0 - s)))).astype(BF16)
        dy1 = dy2 * sz
        dy1_ref[...] = dy1

        @pl.when(pl.program_id(0) == 0)
        def _():
            dg_ref[...] = jnp.zeros_like(dg_ref)
            dd_ref[...] = jnp.zeros_like(dd_ref)

        dg_ref[...] += jnp.sum(dy3 * yg, axis=0, keepdims=True)
        dd_ref[...] += jnp.sum(dy1 * x, axis=0, keepdims=True)

    row = pl.BlockSpec((tr, di), lambda i: (i, 0))
    vec = pl.BlockSpec((1, di), lambda i: (0, 0))
    return pl.pallas_call(
        body, name=name, grid=(L // tr,), in_specs=[row, row, row, row, vec, vec], out_specs=[row, row, vec, vec],
        out_shape=[jax.ShapeDtypeStruct((L, di), BF16), jax.ShapeDtypeStruct((L, di), F32),
                   jax.ShapeDtypeStruct((1, di), F32), jax.ShapeDtypeStruct((1, di), F32)],
        compiler_params=_params("arbitrary"),
    )(dy3, y, xbc, zx, d_x, ng_row)


SB_UNROLL_FWD = 4
SB_UNROLL_BWD = 4


def _sb_heads(x):
    low = _iota(x.shape, 1) < HEAD
    zero = jnp.zeros_like(x)
    return jnp.concatenate([jnp.where(low, x, zero), jnp.where(low, zero, x)], axis=0)


def _sb_unheads(x2):
    return jnp.where(_iota((BLK, LANES), 1) < HEAD, x2[:BLK], x2[BLK:])


def _sb_tiles(qq, kblks, dmat, scol, thrs, s0s):
    n = range(len(kblks))
    z = [_dot(qq, kblks[u], NT) for u in n]
    e = [jnp.exp(-jnp.abs(z[u])) for u in n]
    l1 = [jnp.log(1.0 + e[u]) for u in n]
    lsz = [jnp.minimum(z[u], 0.0) - l1[u] for u in n]
    if thrs is None:
        return z, e, lsz, [-jnp.maximum(z[u], 0.0) - l1[u] for u in n], None
    valid = [(dmat > thrs[u]) & (scol >= s0s[u]) for u in n]
    lkm = [jnp.where(valid[u], -jnp.maximum(z[u], 0.0) - l1[u], 0.0) for u in n]
    return z, e, lsz, lkm, valid


def _sb_sweep(group, i, U, init):
    last = i // U
    carry = group(True, 0, init)
    carry = lax.fori_loop(1, jnp.maximum(last, 1), functools.partial(group, False), carry)
    return lax.fori_loop(jnp.maximum(last, 1), last + 1, functools.partial(group, True), carry)


def _sb_fwd(name, q, k, v, pad):
    L, D = q.shape
    nb = L // BLK
    U = SB_UNROLL_FWD

    def body(q_ref, k_ref, v_ref, o_ref, o32_ref):
        i = pl.program_id(1)
        after = (_iota((BLK, BLK), 0) > _iota((BLK, BLK), 1)).astype(BF16)
        qq = _sb_heads(q_ref[...] * 0.125)
        dmat = (_iota((2 * BLK, BLK), 0) & (BLK - 1)) - _iota((2 * BLK, BLK), 1)
        scol = _iota((2 * BLK, BLK), 1)

        def group(masked, j, carry):
            c, acc = carry
            rng = range(U)
            kraw = [i - (j * U + u) for u in rng]
            kb = [jnp.maximum(kraw[u], 0) for u in rng]
            rows = [pl.ds(pl.multiple_of(kb[u] * BLK, BLK), BLK) for u in rng]
            thr = [jnp.where(kraw[u] >= 0, (kb[u] - i) * BLK, BLK) for u in rng] if masked else None
            s0 = [pad - kb[u] * BLK for u in rng] if masked else None
            _, _, lsz, lkm, valid = _sb_tiles(qq, [k_ref[rows[u], :] for u in rng], dmat, scol, thr, s0)
            cum = [_dotx(lkm[u], after, parts=2) for u in rng]
            a = []
            for u in rng:
                w = jnp.exp(lsz[u] + (c + cum[u]))
                a.append((jnp.where(valid[u], w, 0.0) if masked else w).astype(BF16))
                c = c + jnp.sum(lkm[u], axis=1, keepdims=True)
            for u in rng:
                acc = acc + _dot(a[u], v_ref[rows[u], :])
            return c, acc

        _, acc = _sb_sweep(group, i, U, (jnp.zeros((2 * BLK, 1), F32), jnp.zeros((2 * BLK, LANES), F32)))
        out = _sb_unheads(acc)
        o_ref[...] = out.astype(BF16)
        o32_ref[...] = out

    blk = pl.BlockSpec((BLK, LANES), lambda p, i: (i, p))
    col = pl.BlockSpec((L, LANES), lambda p, i: (0, p))
    return pl.pallas_call(
        body, name=name, grid=(D // LANES, nb), in_specs=[blk, col, col], out_specs=[blk, blk],
        out_shape=[jax.ShapeDtypeStruct((L, D), BF16), jax.ShapeDtypeStruct((L, D), F32)],
        compiler_params=_params("parallel", "arbitrary"),
    )(q, k, v)


def _sb_bwd(name, q, k, v, do, o32, pad):
    L, D = q.shape
    nb = L // BLK
    U = SB_UNROLL_BWD

    def body(q_ref, k_ref, v_ref, do_ref, o_ref, dq_ref, dk_ref, dv_ref):
        i = pl.program_id(1)

        @pl.when(i == 0)
        def _():
            dk_ref[...] = jnp.zeros_like(dk_ref)
            dv_ref[...] = jnp.zeros_like(dv_ref)

        after = (_iota((BLK, BLK), 0) > _iota((BLK, BLK), 1)).astype(BF16)
        from_j = (_iota((BLK, BLK), 0) >= _iota((BLK, BLK), 1)).astype(BF16)
        qq = _sb_heads(q_ref[...] * 0.125)
        dd = _sb_heads(do_ref[...])
        dmat = (_iota((2 * BLK, BLK), 0) & (BLK - 1)) - _iota((2 * BLK, BLK), 1)
        scol = _iota((2 * BLK, BLK), 1)
        o2 = jnp.concatenate([o_ref[...], o_ref[...]], axis=0)
        total = jnp.sum(dd.astype(F32) * o2, axis=1, keepdims=True)

        def group(masked, j, carry):
            c, met, acc = carry
            rng = range(U)
            kraw = [i - (j * U + u) for u in rng]
            kb = [jnp.maximum(kraw[u], 0) for u in rng]
            rows = [pl.ds(pl.multiple_of(kb[u] * BLK, BLK), BLK) for u in rng]
            thr = [jnp.where(kraw[u] >= 0, (kb[u] - i) * BLK, BLK) for u in rng] if masked else None
            s0 = [pad - kb[u] * BLK for u in rng] if masked else None
            kblk = [k_ref[rows[u], :] for u in rng]
            z, e, lsz, lkm, valid = _sb_tiles(qq, kblk, dmat, scol, thr, s0)
            da = [_dot(dd, v_ref[rows[u], :], NT) for u in rng]
            cum = [_dotx(lkm[u], after, parts=2) for u in rng]
            a = []
            for u in rng:
                w = jnp.exp(lsz[u] + (c + cum[u]))
                a.append((jnp.where(valid[u], w, 0.0) if masked else w).astype(BF16))
                c = c + jnp.sum(lkm[u], axis=1, keepdims=True)
            dlog = [da[u] * a[u].astype(F32) for u in rng]
            here = [_dotx(dlog[u], from_j, parts=2) for u in rng]
            dz = []
            for u in rng:
                inv = 1.0 / (1.0 + e[u])
                sig = jnp.where(z[u] >= 0.0, inv, e[u] * inv)
                pull = (total - (met + here[u])) * sig
                dz.append((dlog[u] * (1.0 - sig) - (jnp.where(valid[u], pull, 0.0) if masked else pull)).astype(BF16))
                met = met + jnp.sum(dlog[u], axis=1, keepdims=True)
            for u in rng:
                dk_ref[rows[u], :] += _dot(dz[u], qq, TN)
                dv_ref[rows[u], :] += _dot(a[u], dd, TN)
                acc = acc + _dot(dz[u], kblk[u])
            return c, met, acc

        col1 = jnp.zeros((2 * BLK, 1), F32)
        _, _, acc = _sb_sweep(group, i, U, (col1, col1, jnp.zeros((2 * BLK, LANES), F32)))
        dq_ref[...] = _sb_unheads(acc) * 0.125

    blk = pl.BlockSpec((BLK, LANES), lambda p, i: (i, p))
    col = pl.BlockSpec((L, LANES), lambda p, i: (0, p))
    return pl.pallas_call(
        body, name=name, grid=(D // LANES, nb), in_specs=[blk, col, col, blk, blk], out_specs=[blk, col, col],
        out_shape=[jax.ShapeDtypeStruct((L, D), F32)] * 3, compiler_params=_params("parallel", "arbitrary"),
    )(q, k, v, do, o32)


def _local_step(x, tgt, w):
    S, D = x.shape
    nm = w["meta_tokens"].shape[0]
    pad = BLK - nm
    L = pad + nm + S
    assert L % BLK == 0 and 0 < nm <= BLK
    di = w["ssm_out_proj"].shape[0]
    nh = w["ssm_dt_bias"].shape[0]
    assert di == nh * HEAD and nh <= LANES
    conv_dim = w["ssm_conv_w"].shape[1]
    ng = (conv_dim - di) // (2 * D_STATE)
    zp = di + conv_dim + LANES
    g = w["norm_g"]
    grads = {}

    h0 = jnp.concatenate([jnp.zeros((pad, D), F32), w["meta_tokens"], x], axis=0)

    h1, s1 = _ffn_fwd("f00", h0, g[0, 0], w["ffn_w1"][0, 0], w["ffn_w3"][0, 0], w["ffn_w2"][0, 0])
    u0 = _rms_fwd("m_norm", h1, g[0, 1])
    w_in = jnp.concatenate([w["ssm_in_proj"], jnp.zeros((D, zp - w["ssm_in_proj"].shape[1]), BF16)], axis=1)
    tl = _tile(L, 704, 16)
    (zx,) = _matmul("m_inproj", [(u0, w_in)], "nn", tl, _tile(zp, 1024, LANES), D, [F32])
    conv_b = w["ssm_conv_b"].reshape(1, conv_dim)
    xbc = _conv_fwd("m_conv", zx, di, w["ssm_conv_w"], conv_b, pad)
    bias_row = jnp.zeros((1, LANES), F32).at[0, :nh].set(w["ssm_dt_bias"])
    dt, dtt = _dt_fwd("m_dt", zx, di + conv_dim, bias_row, nh, pad)
    a_neg = -jnp.exp(w["ssm_a_log"])
    a_row = jnp.zeros((1, LANES), F32).at[0, :nh].set(a_neg)
    a_col = jnp.broadcast_to(a_row.reshape(LANES, 1), (LANES, LANES))
    expand = (jnp.arange(LANES)[:, None] == (jnp.arange(di) // HEAD)[None, :]).astype(BF16)
    y_ssd, states = _ssd_fwd("m_ssd", xbc, dt, dtt, a_row, a_col, expand, di, ng)
    d_x = jnp.repeat(w["ssm_d"], HEAD).reshape(1, di)
    ssm_g = w["ssm_norm_g"].reshape(1, di)
    gsz = di // ng
    y3 = _gate_fwd("m_gate", y_ssd, xbc, zx, d_x, ssm_g, gsz)
    (h2,) = _matmul("m_outproj", [(y3, w["ssm_out_proj"])], "nn", tl, D, di, [F32],
                    epilogue=lambda accs, ex: [ex[0] + accs[0]], extras=[(h1, "mn")])
    h3, s2 = _ffn_fwd("f01", h2, g[0, 2], w["ffn_w1"][0, 1], w["ffn_w3"][0, 1], w["ffn_w2"][0, 1])

    kv_in = _rms_fwd("kv_norm", h3, w["kv_norm_g"])
    (k_raw,) = _matmul("kv_k", [(kv_in, w["w_k"])], "nn", tl, D, D, [F32])
    (v_sh,) = _matmul("kv_v", [(kv_in, w["w_v"])], "nn", tl, D, D, [BF16])
    kg = jnp.tile(w["k_norm_g"], D // HEAD).reshape(1, D)
    k_sh = _headrms_fwd("kv_knorm", k_raw, kg)

    h4, s3 = _ffn_fwd("f10", h3, g[1, 0], w["ffn_w1"][1, 0], w["ffn_w3"][1, 0], w["ffn_w2"][1, 0])
    u1 = _rms_fwd("a_norm", h4, g[1, 1])
    (q_raw,) = _matmul("a_q", [(u1, w["sb_w_q"])], "nn", tl, D, D, [F32])
    qg = jnp.tile(w["sb_q_norm_g"], D // HEAD).reshape(1, D)
    q = _headrms_fwd("a_qnorm", q_raw, qg)
    o, o32 = _sb_fwd("a_attn", q, k_sh, v_sh, pad)
    (h5,) = _matmul("a_o", [(o, w["sb_w_o"])], "nn", tl, D, D, [F32],
                    epilogue=lambda accs, ex: [ex[0] + accs[0]], extras=[(h4, "mn")])
    h6, s4 = _ffn_fwd("f11", h5, g[1, 2], w["ffn_w1"][1, 1], w["ffn_w3"][1, 1], w["ffn_w2"][1, 1])

    dh6, sq = _loss("loss", h6, tgt, pad + nm)

    dg = jnp.zeros_like(g)
    dw1 = [[None, None], [None, None]]
    dw3 = [[None, None], [None, None]]
    dw2 = [[None, None], [None, None]]
    dh5, dgv, dw1[1][1], dw3[1][1], dw2[1][1] = _ffn_bwd("b11", dh6, h5, g[1, 2], w["ffn_w1"][1, 1], w["ffn_w3"][1, 1],
                                                           w["ffn_w2"][1, 1], s4)
    dg = dg.at[1, 2].set(dgv)
    td = _tile(D, 512, LANES)
    (do,) = _matmul("b_do", [(dh5, w["sb_w_o"])], "nt", tl, D, D, [BF16])
    (grads["sb_w_o"],) = _matmul("b_dwo", [(o, dh5)], "tn", D, td, tl, [F32])
    dq, dk, dv = _sb_bwd("b_attn", q, k_sh, v_sh, do, o32, pad)
    dq_raw, dqg = _headrms_bwd("b_qnorm", dq, q_raw, qg)
    grads["sb_q_norm_g"] = dqg.reshape(D // HEAD, HEAD).sum(0)
    (grads["sb_w_q"],) = _matmul("b_dwq", [(u1, dq_raw)], "tn", D, td, tl, [F32])
    (du1,) = _matmul("b_du1", [(dq_raw, w["sb_w_q"])], "nt", tl, D, D, [F32])
    dh4, dgv = _rms_bwd("b_anorm", du1, h4, g[1, 1], dh5)
    dg = dg.at[1, 1].set(dgv.reshape(-1))
    dh3, dgv, dw1[1][0], dw3[1][0], dw2[1][0] = _ffn_bwd("b10", dh4, h3, g[1, 0], w["ffn_w1"][1, 0], w["ffn_w3"][1, 0],
                                                           w["ffn_w2"][1, 0], s3)
    dg = dg.at[1, 0].set(dgv)

    dk_raw, dkg = _headrms_bwd("b_knorm", dk, k_raw, kg)
    grads["k_norm_g"] = dkg.reshape(D // HEAD, HEAD).sum(0)
    (grads["w_k"],) = _matmul("b_dwk", [(kv_in, dk_raw)], "tn", D, td, tl, [F32])
    (grads["w_v"],) = _matmul("b_dwv", [(kv_in, dv)], "tn", D, td, tl, [F32])
    (dkv_in,) = _matmul("b_dkvin", [(dk_raw, w["w_k"]), (dv, w["w_v"])], "nt", tl, D, D, [F32])
    dh3, dgv = _rms_bwd("b_kvnorm", dkv_in, h3, w["kv_norm_g"], dh3)
    grads["kv_norm_g"] = dgv.reshape(-1)

    dh2, dgv, dw1[0][1], dw3[0][1], dw2[0][1] = _ffn_bwd("b01", dh3, h2, g[0, 2], w["ffn_w1"][0, 1], w["ffn_w3"][0, 1],
                                                           w["ffn_w2"][0, 1], s2)
    dg = dg.at[0, 2].set(dgv)
    (dy3,) = _matmul("b_dy3", [(dh2, w["ssm_out_proj"])], "nt", tl, _tile(di, 1024, LANES), D, [F32])
    (grads["ssm_out_proj"],) = _matmul("b_dwout", [(y3, dh2)], "tn", _tile(di, 1024, LANES), D, tl, [F32])
    dz, dy1, dssm_g, dd_x = _gate_bwd("b_gate", dy3, y_ssd, xbc, zx, d_x, ssm_g, gsz)
    grads["ssm_norm_g"] = dssm_g.reshape(-1)
    grads["ssm_d"] = dd_x.reshape(nh, HEAD).sum(1)
    dxbc, ddt, da = _ssd_bwd("b_ssd", xbc, dt, dtt, a_row, a_col, expand, states, dy1, d_x, di, ng)
    grads["ssm_a_log"] = da[0, :nh] * a_neg
    ddt_raw, dbias = _dt_bwd("b_dt", ddt, zx, di + conv_dim, bias_row, nh, pad)
    grads["ssm_dt_bias"] = dbias[0, :nh]
    dpre, grads["ssm_conv_w"], dconv_b = _conv_bwd_pre("b_convpre", dxbc, zx, di, w["ssm_conv_w"], conv_b, pad)
    grads["ssm_conv_b"] = dconv_b.reshape(-1)
    dxbc_raw = _conv_bwd_in("b_convin", dpre, w["ssm_conv_w"])
    dzx = jnp.concatenate([dz, dxbc_raw, ddt_raw], axis=1)
    (dw_in,) = _matmul("b_dwin", [(u0, dzx)], "tn", D, _tile(zp, 1024, LANES), tl, [F32])
    grads["ssm_in_proj"] = dw_in[:, :w["ssm_in_proj"].shape[1]]
    (du0,) = _matmul("b_du0", [(dzx, w_in)], "nt", _tile(L, 352, 16), td, zp, [F32])
    dh1, dgv = _rms_bwd("b_mnorm", du0, h1, g[0, 1], dh2)
    dg = dg.at[0, 1].set(dgv.reshape(-1))
    dh0, dgv, dw1[0][0], dw3[0][0], dw2[0][0] = _ffn_bwd("b00", dh1, h0, g[0, 0], w["ffn_w1"][0, 0], w["ffn_w3"][0, 0],
                                                           w["ffn_w2"][0, 0], s1)
    dg = dg.at[0, 0].set(dgv)

    grads["norm_g"] = dg
    grads["ffn_w1"] = jnp.stack([jnp.stack(r) for r in dw1])
    grads["ffn_w3"] = jnp.stack([jnp.stack(r) for r in dw3])
    grads["ffn_w2"] = jnp.stack([jnp.stack(r) for r in dw2])
    grads["meta_tokens"] = dh0[pad:pad + nm]
    return sq, dh0[pad + nm:], grads


HBM_SPEC = pl.BlockSpec(memory_space=pltpu.HBM)
PAIR_PIECES = 8


def _place():
    return lax.axis_index("x"), lax.axis_index("y"), lax.axis_index("c")


def _allgather8(name, blk):
    m, n = blk.shape

    def body(x_ref, out_ref, send_sems, recv_sems, local_sem):
        x, y, c = _place()
        me, sibling = (x, y, c), (x, y, 1 - c)
        chips = [(1 - x, y), (x, 1 - y), (1 - x, 1 - y)]

        def rows(px, py, pc):
            return out_ref.at[pl.ds((4 * px + 2 * py + pc) * m, m), :]

        def copy(k, block, to, src=None):
            return pltpu.make_async_remote_copy(
                src_ref=rows(*block) if src is None else src, dst_ref=rows(*block),
                send_sem=send_sems.at[k], recv_sem=recv_sems.at[k], device_id=to, device_id_type=MESH)

        mine = pltpu.make_async_copy(x_ref, rows(*me), local_sem)
        mine.start()
        first = [copy(0, me, sibling, src=x_ref)]
        first += [copy(1 + j, me, (*chip, c), src=x_ref) for j, chip in enumerate(chips)]
        for cp in first:
            cp.start()
        passed = [copy(4 + j, (*chip, c), sibling) for j, chip in enumerate(chips)]
        for j, chip in enumerate(chips):
            copy(1 + j, (*chip, c), me).wait_recv()
            passed[j].start()
        copy(0, sibling, me).wait_recv()
        for j, chip in enumerate(chips):
            copy(4 + j, (*chip, 1 - c), me).wait_recv()
        for cp in first + passed:
            cp.wait_send()
        mine.wait()

    return pl.pallas_call(
        body, name=name, out_shape=jax.ShapeDtypeStruct((8 * m, n), blk.dtype),
        in_specs=[HBM_SPEC], out_specs=HBM_SPEC,
        scratch_shapes=[pltpu.SemaphoreType.DMA((7,)), pltpu.SemaphoreType.DMA((7,)), pltpu.SemaphoreType.DMA],
    )(blk)


def _exchange8(name, g):
    _, m, n = g.shape

    def body(g_ref, out_ref, send_sems, recv_sems, local_sem):
        x, y, c = _place()
        me_id = 4 * x + 2 * y + c
        mine = pltpu.make_async_copy(g_ref.at[me_id], out_ref.at[me_id], local_sem)
        mine.start()
        sends, recvs = [], []
        for k in range(1, 8):
            px = 1 - x if k & 4 else x
            py = 1 - y if k & 2 else y
            pc = 1 - c if k & 1 else c
            pid = 4 * px + 2 * py + pc
            sends.append(pltpu.make_async_remote_copy(
                src_ref=g_ref.at[pid], dst_ref=out_ref.at[me_id], send_sem=send_sems.at[k - 1],
                recv_sem=recv_sems.at[k - 1], device_id=(px, py, pc), device_id_type=MESH))
            recvs.append(pltpu.make_async_remote_copy(
                src_ref=g_ref.at[me_id], dst_ref=out_ref.at[pid], send_sem=send_sems.at[k - 1],
                recv_sem=recv_sems.at[k - 1], device_id=(px, py, pc), device_id_type=MESH))
        for cp in sends:
            cp.start()
        for cp in recvs:
            cp.wait_recv()
        for cp in sends:
            cp.wait_send()
        mine.wait()

    return pl.pallas_call(
        body, name=name, out_shape=jax.ShapeDtypeStruct(g.shape, g.dtype), in_specs=[HBM_SPEC], out_specs=HBM_SPEC,
        scratch_shapes=[pltpu.SemaphoreType.DMA((7,)), pltpu.SemaphoreType.DMA((7,)), pltpu.SemaphoreType.DMA],
    )(g)


def _pairshare(name, half):
    m, n = half.shape
    pieces = PAIR_PIECES if m % (8 * PAIR_PIECES) == 0 else 1
    pm = m // pieces

    def body(x_ref, out_ref, send_sems, recv_sems, local_sems):
        x, y, c = _place()

        def rows(core, j):
            return out_ref.at[pl.ds(core * m + j * pm, pm), :]

        def src(j):
            return x_ref.at[pl.ds(j * pm, pm), :]

        def remote(j, core):
            return pltpu.make_async_remote_copy(
                src_ref=src(j), dst_ref=rows(core, j), send_sem=send_sems.at[j], recv_sem=recv_sems.at[j],
                device_id=(x, y, 1 - c), device_id_type=MESH)

        sends = [remote(j, c) for j in range(pieces)]
        mine = [pltpu.make_async_copy(src(j), rows(c, j), local_sems.at[j]) for j in range(pieces)]
        for cp in sends + mine:
            cp.start()
        for j in range(pieces):
            remote(j, 1 - c).wait_recv()
        for cp in sends:
            cp.wait_send()
        for cp in mine:
            cp.wait()

    return pl.pallas_call(
        body, name=name, out_shape=jax.ShapeDtypeStruct((2 * m, n), half.dtype), in_specs=[HBM_SPEC], out_specs=HBM_SPEC,
        scratch_shapes=[pltpu.SemaphoreType.DMA((pieces,)), pltpu.SemaphoreType.DMA((pieces,)),
                        pltpu.SemaphoreType.DMA((pieces,))],
    )(half)


def _sum8(name, parts):
    _, m, n = parts.shape
    tr = _tile(m, 512, 16)

    def body(p_ref, o_ref):
        acc = p_ref[0].astype(F32)
        for s in range(1, 8):
            acc = acc + p_ref[s].astype(F32)
        o_ref[...] = acc

    return pl.pallas_call(
        body, name=name, grid=(m // tr,), in_specs=[pl.BlockSpec((8, tr, n), lambda i: (0, i, 0))],
        out_specs=pl.BlockSpec((tr, n), lambda i: (i, 0)), out_shape=jax.ShapeDtypeStruct((m, n), F32),
        compiler_params=_params("parallel"),
    )(parts)


def _adamw(name, w, g, m, v):
    shape = w.shape
    cols = shape[-1]
    rows = math.prod(shape[:-1])
    tr = _tile(rows, 512, 8) if rows * cols > 2 ** 19 else rows

    def body(w_ref, g_ref, m_ref, v_ref, d_ref, mo_ref, vo_ref):
        gv = g_ref[...]
        m2 = ADAM_B1 * m_ref[...] + (1.0 - ADAM_B1) * gv
        v2 = ADAM_B2 * v_ref[...] + (1.0 - ADAM_B2) * (gv * gv)
        m_hat = m2 / (1.0 - ADAM_B1 ** ADAM_STEP)
        v_hat = v2 / (1.0 - ADAM_B2 ** ADAM_STEP)
        d_ref[...] = -ADAM_LR * (m_hat / (jnp.sqrt(v_hat) + ADAM_EPS) + ADAM_WD * w_ref[...])
        mo_ref[...] = m2
        vo_ref[...] = v2

    spec = pl.BlockSpec((tr, cols), lambda i: (i, 0))
    outs = pl.pallas_call(
        body, name=name, grid=(rows // tr,), in_specs=[spec] * 4, out_specs=[spec] * 3,
        out_shape=[jax.ShapeDtypeStruct((rows, cols), F32)] * 3, compiler_params=_params("parallel"),
    )(*(a.reshape(rows, cols) for a in (w, g, m, v)))
    return tuple(o.reshape(shape) for o in outs)


WEIGHTS = ["meta_tokens", "norm_g", "ffn_w1", "ffn_w3", "ffn_w2", "ssm_in_proj", "ssm_conv_w", "ssm_conv_b", "ssm_dt_bias",
           "ssm_a_log", "ssm_d", "ssm_norm_g", "ssm_out_proj", "kv_norm_g", "w_k", "k_norm_g", "w_v", "sb_w_q",
           "sb_q_norm_g", "sb_w_o"]
SHARD_AXIS = {"meta_tokens": 1, "norm_g": 2, "ffn_w1": 3, "ffn_w3": 3, "ffn_w2": 2, "ssm_in_proj": 2, "ssm_conv_w": 2,
              "ssm_conv_b": 1, "ssm_norm_g": 1, "ssm_out_proj": 1, "w_k": 0, "w_v": 0, "sb_w_q": 1, "sb_w_o": 1}
MATRICES = ["ffn_w1", "ffn_w3", "ffn_w2", "ssm_in_proj", "ssm_out_proj", "w_k", "w_v", "sb_w_q", "sb_w_o"]
VECTORS = [n for n in WEIGHTS if n in SHARD_AXIS and n not in MATRICES]
REPLICATED = [n for n in WEIGHTS if n not in SHARD_AXIS]
LAYER_AXIS = ("ssm_", "sb_")
PACK_COLS = 1024
N_CHIPS = 4


def _pack(arrays, row_mult, dtype):
    segs = []
    for a in arrays:
        n = math.prod(a.shape)
        r = -(-n // PACK_COLS)
        flat = a.reshape(-1).astype(dtype)
        if r * PACK_COLS != n:
            flat = jnp.pad(flat, (0, r * PACK_COLS - n))
        segs.append(flat.reshape(r, PACK_COLS))
    rows = sum(s.shape[0] for s in segs)
    extra = -rows % row_mult
    if extra:
        segs.append(jnp.zeros((extra, PACK_COLS), dtype))
    return jnp.concatenate(segs, axis=0)


def _unpack(packed, shapes):
    lead = packed.shape[:-2]
    out, r0 = [], 0
    for shp in shapes:
        n = math.prod(shp)
        r = -(-n // PACK_COLS)
        seg = packed[..., r0:r0 + r, :]
        if r * PACK_COLS != n:
            seg = seg.reshape(*lead, r * PACK_COLS)[..., :n]
        out.append(seg.reshape(*lead, *shp))
        r0 += r
    return out


def _join(stack, axis):
    return jnp.concatenate([stack[s] for s in range(N_CHIPS)], axis=axis)


def _shards(full, axis):
    n = full.shape[axis] // N_CHIPS
    return [lax.slice_in_dim(full, s * n, (s + 1) * n, axis=axis) for s in range(N_CHIPS)]


def _drop_layer(name, a):
    return a[0] if name.startswith(LAYER_AXIS) else a


def kernel(x, meta_tokens, norm_g, ffn_w1, ffn_w3, ffn_w2, ssm_in_proj, ssm_conv_w, ssm_conv_b, ssm_dt_bias, ssm_a_log, ssm_d, ssm_norm_g, ssm_out_proj, kv_norm_g, w_k, k_norm_g, w_v, sb_w_q, sb_q_norm_g, sb_w_o, loss_target, m_meta_tokens, m_norm_g, m_ffn_w1, m_ffn_w3, m_ffn_w2, m_ssm_in_proj, m_ssm_conv_w, m_ssm_conv_b, m_ssm_dt_bias, m_ssm_a_log, m_ssm_d, m_ssm_norm_g, m_ssm_out_proj, m_kv_norm_g, m_w_k, m_k_norm_g, m_w_v, m_sb_w_q, m_sb_q_norm_g, m_sb_w_o, v_meta_tokens, v_norm_g, v_ffn_w1, v_ffn_w3, v_ffn_w2, v_ssm_in_proj, v_ssm_conv_w, v_ssm_conv_b, v_ssm_dt_bias, v_ssm_a_log, v_ssm_d, v_ssm_norm_g, v_ssm_out_proj, v_kv_norm_g, v_w_k, v_k_norm_g, v_w_v, v_sb_w_q, v_sb_q_norm_g, v_sb_w_o):
    args = locals()
    w_in = {n: args[n] for n in WEIGHTS}
    m_in = {n: args["m_" + n] for n in WEIGHTS}
    v_in = {n: args["v_" + n] for n in WEIGHTS}
    c = lax.axis_index("c")

    def gather(names, dtype, row_mult):
        packed = _pack([w_in[n] for n in names], 2 * row_mult, dtype)
        half = packed.shape[0] // 2
        mine = lax.dynamic_slice_in_dim(packed, c * half, half, axis=0)
        allp = _allgather8(f"gather_{jnp.dtype(dtype).name}", mine).reshape(N_CHIPS, 2 * half, PACK_COLS)
        stacks = _unpack(allp, [w_in[n].shape for n in names])
        return {n: _join(s, SHARD_AXIS[n]) for n, s in zip(names, stacks)}

    full = {**gather(MATRICES, BF16, 16), **gather(VECTORS, F32, 8)}
    full.update({n: w_in[n] for n in REPLICATED})
    full = {n: _drop_layer(n, a) for n, a in full.items()}

    sq, grad_x, grads = _local_step(x[0], loss_target[0], full)
    loss = lax.psum(0.5 / x.shape[-1] * jnp.sum(sq), ("x", "y", "c"))
    grads = {n: (g[None] if n.startswith(LAYER_AXIS) else g) for n, g in grads.items()}

    sharded = MATRICES + VECTORS
    per_chip = [[] for _ in range(N_CHIPS)]
    for n in sharded:
        for p, s in enumerate(_shards(grads[n], SHARD_AXIS[n])):
            per_chip[p].append(s)
    contrib = jnp.stack([_pack(parts, 16 * PAIR_PIECES, BF16) for parts in per_chip])
    rows = contrib.shape[1]
    got = _exchange8("grad_exchange", contrib.reshape(2 * N_CHIPS, rows // 2, PACK_COLS))
    reduced = _pairshare("grad_share", _sum8("grad_sum", got))
    g_out = dict(zip(sharded, _unpack(reduced, [w_in[n].shape for n in sharded])))

    rep = _pack([grads[n] for n in REPLICATED], 8, F32)
    rep_sum = _sum8("rep_sum", _allgather8("rep_gather", rep).reshape(8, rep.shape[0], PACK_COLS))
    g_out.update(zip(REPLICATED, _unpack(rep_sum, [w_in[n].shape for n in REPLICATED])))

    delta, new_m, new_v = {}, {}, {}
    for n in WEIGHTS:
        delta[n], new_m[n], new_v[n] = _adamw(f"adamw_{n}", w_in[n], g_out[n], m_in[n], v_in[n])
    return (loss, grad_x[None], *[g_out[n] for n in WEIGHTS], *[delta[n] for n in WEIGHTS],
            *[new_m[n] for n in WEIGHTS], *[new_v[n] for n in WEIGHTS])
```

```python
import functools
import math

import jax
import jax.numpy as jnp
from jax import lax
from jax.experimental import pallas as pl
from jax.experimental.pallas import tpu as pltpu

F32, BF16 = jnp.float32, jnp.bfloat16
RMS_EPS = 1e-6
LANES = 128
HEAD = 64
D_STATE = 128
BLK = 128
FFN_RES = 0.5
VMEM_LIMIT = 56 * 2 ** 20
ADAM_LR, ADAM_B1, ADAM_B2, ADAM_EPS, ADAM_WD, ADAM_STEP = 0.001, 0.9, 0.999, 1e-08, 0.01, 10
MESH = pl.DeviceIdType.MESH

NN = (((1,), (0,)), ((), ()))
NT = (((1,), (1,)), ((), ()))
TN = (((0,), (0,)), ((), ()))


def _dot(a, b, dn=NN):
    return lax.dot_general(a, b, dn, preferred_element_type=F32)


def _split(x, parts):
    out = []
    for _ in range(parts):
        p = x.astype(BF16)
        out.append(p)
        x = x - p.astype(F32)
    return out


def _dotx(a, b, dn=NN, parts=3, split="a"):
    if split == "a":
        return sum(_dot(p, b, dn) for p in _split(a, parts))
    return sum(_dot(a, p, dn) for p in _split(b, parts))


def _tile(n, target, mult):
    best = None
    for d in range(mult, min(n, target) + 1, mult):
        if n % d == 0:
            best = d
    return n if best is None else best


def _params(*sem):
    return pltpu.CompilerParams(dimension_semantics=tuple(sem) if sem else None, vmem_limit_bytes=VMEM_LIMIT)


def _iota(shape, axis):
    return lax.broadcasted_iota(jnp.int32, shape, axis)


def _sigmoid(x):
    return 1.0 / (1.0 + jnp.exp(-x))


def _matmul(name, pairs, mode, tm, tn, tk, out_dtypes, epilogue=None, extras=(), separate=False):
    a0, b0 = pairs[0]
    if mode == "nn":
        (M, K), N = a0.shape, b0.shape[1]
    elif mode == "nt":
        (M, K), N = a0.shape, b0.shape[0]
    else:
        (K, M), N = a0.shape, b0.shape[1]
    assert M % tm == 0 and N % tn == 0 and K % tk == 0, (name, M, N, K, tm, tn, tk)
    nM, nN, nK = M // tm, N // tn, K // tk
    np_, ne, no = len(pairs), len(extras), len(out_dtypes)
    n_acc = np_ if separate else 1
    dn = {"nn": NN, "nt": NT, "tn": TN}[mode]

    def body(*refs):
        ab, ex = refs[:2 * np_], refs[2 * np_:2 * np_ + ne]
        outs, accs = refs[2 * np_ + ne:2 * np_ + ne + no], refs[2 * np_ + ne + no:]
        k = pl.program_id(2)

        def prod(i):
            return _dot(ab[2 * i][...].astype(BF16), ab[2 * i + 1][...].astype(BF16), dn)

        ps = [prod(i) for i in range(np_)]
        if not separate:
            ps = [functools.reduce(lambda u, v: u + v, ps)]

        def finish(vals):
            res = epilogue(vals, [e[...] for e in ex]) if epilogue is not None else vals
            for o, r in zip(outs, res):
                o[...] = r.astype(o.dtype)

        if nK == 1:
            finish(ps)
        else:
            @pl.when(k == 0)
            def _():
                for acc, p in zip(accs, ps):
                    acc[...] = p

            @pl.when(k > 0)
            def _():
                for acc, p in zip(accs, ps):
                    acc[...] += p

            @pl.when(k == nK - 1)
            def _():
                finish([acc[...] for acc in accs])

    if mode == "tn":
        a_spec = pl.BlockSpec((tk, tm), lambda n, m, k: (k, m))
    else:
        a_spec = pl.BlockSpec((tm, tk), lambda n, m, k: (m, k))
    if mode == "nt":
        b_spec = pl.BlockSpec((tn, tk), lambda n, m, k: (n, k))
    else:
        b_spec = pl.BlockSpec((tk, tn), lambda n, m, k: (k, n))
    in_specs, args = [], []
    for a, b in pairs:
        in_specs += [a_spec, b_spec]
        args += [a, b]
    for arr, kind in extras:
        if kind == "mn":
            in_specs.append(pl.BlockSpec((tm, tn), lambda n, m, k: (m, n)))
        else:
            in_specs.append(pl.BlockSpec((1, tn), lambda n, m, k: (0, n)))
        args.append(arr)
    out_specs = [pl.BlockSpec((tm, tn), lambda n, m, k: (m, n)) for _ in out_dtypes]
    res = pl.pallas_call(
        body, name=name, grid=(nN, nM, nK), in_specs=in_specs, out_specs=out_specs,
        out_shape=[jax.ShapeDtypeStruct((M, N), d) for d in out_dtypes],
        scratch_shapes=[pltpu.VMEM((tm, tn), F32) for _ in range(n_acc)] if nK > 1 else [],
        compiler_params=_params("parallel", "parallel", "arbitrary"),
    )(*args)
    return res


def _rms_fwd(name, h, g):
    L, D = h.shape
    tr = _tile(L, 1024, 16)

    def body(h_ref, g_ref, o_ref):
        x = h_ref[...]
        r = lax.rsqrt(jnp.mean(x * x, axis=-1, keepdims=True) + RMS_EPS)
        o_ref[...] = (x * r * g_ref[...]).astype(BF16)

    return pl.pallas_call(
        body, name=name, grid=(L // tr,),
        in_specs=[pl.BlockSpec((tr, D), lambda i: (i, 0)), pl.BlockSpec((1, D), lambda i: (0, 0))],
        out_specs=pl.BlockSpec((tr, D), lambda i: (i, 0)),
        out_shape=jax.ShapeDtypeStruct((L, D), BF16), compiler_params=_params("parallel"),
    )(h, g.reshape(1, D))


def _rms_bwd(name, dxn, h, g, dres):
    L, D = h.shape
    tr = _tile(L, 512, 8)

    def body(dxn_ref, h_ref, g_ref, dres_ref, dh_ref, dg_ref):
        x = h_ref[...]
        r = lax.rsqrt(jnp.mean(x * x, axis=-1, keepdims=True) + RMS_EPS)
        xh = x * r
        dxn = dxn_ref[...]
        dxh = dxn * g_ref[...]
        dh_ref[...] = dres_ref[...] + r * (dxh - xh * jnp.mean(dxh * xh, axis=-1, keepdims=True))

        @pl.when(pl.program_id(0) == 0)
        def _():
            dg_ref[...] = jnp.zeros_like(dg_ref)

        dg_ref[...] += jnp.sum(dxn * xh, axis=0, keepdims=True)

    row = pl.BlockSpec((tr, D), lambda i: (i, 0))
    vec = pl.BlockSpec((1, D), lambda i: (0, 0))
    return pl.pallas_call(
        body, name=name, grid=(L // tr,), in_specs=[row, row, vec, row], out_specs=[row, vec],
        out_shape=[jax.ShapeDtypeStruct((L, D), F32), jax.ShapeDtypeStruct((1, D), F32)],
        compiler_params=_params("arbitrary"),
    )(dxn, h, g.reshape(1, D), dres)


def _head_sums(x2):
    blockdiag = (_iota((LANES, LANES), 0) // HEAD == _iota((LANES, LANES), 1) // HEAD).astype(BF16)
    cols = [_dotx(x2[:, j:j + LANES], blockdiag) for j in range(0, x2.shape[1], LANES)]
    return jnp.concatenate(cols, axis=1) if len(cols) > 1 else cols[0]


def _headrms_fwd(name, raw, g):
    L, D = raw.shape
    tr = _tile(L, 512, 16)

    def body(x_ref, g_ref, o_ref):
        x = x_ref[...]
        r = lax.rsqrt(_head_sums(x * x) * (1.0 / HEAD) + RMS_EPS)
        o_ref[...] = (x * r * g_ref[...]).astype(BF16)

    return pl.pallas_call(
        body, name=name, grid=(L // tr,),
        in_specs=[pl.BlockSpec((tr, D), lambda i: (i, 0)), pl.BlockSpec((1, D), lambda i: (0, 0))],
        out_specs=pl.BlockSpec((tr, D), lambda i: (i, 0)),
        out_shape=jax.ShapeDtypeStruct((L, D), BF16), compiler_params=_params("parallel"),
    )(raw, g)


def _headrms_bwd(name, dy, raw, g):
    L, D = raw.shape
    tr = _tile(L, 512, 16)

    def body(dy_ref, x_ref, g_ref, dx_ref, dg_ref):
        x = x_ref[...]
        dy = dy_ref[...]
        r = lax.rsqrt(_head_sums(x * x) * (1.0 / HEAD) + RMS_EPS)
        xh = x * r
        dxh = dy * g_ref[...]
        dx_ref[...] = (r * (dxh - xh * (_head_sums(dxh * xh) * (1.0 / HEAD)))).astype(BF16)

        @pl.when(pl.program_id(0) == 0)
        def _():
            dg_ref[...] = jnp.zeros_like(dg_ref)

        dg_ref[...] += jnp.sum(dy * xh, axis=0, keepdims=True)

    row = pl.BlockSpec((tr, D), lambda i: (i, 0))
    vec = pl.BlockSpec((1, D), lambda i: (0, 0))
    return pl.pallas_call(
        body, name=name, grid=(L // tr,), in_specs=[row, row, vec], out_specs=[row, vec],
        out_shape=[jax.ShapeDtypeStruct((L, D), BF16), jax.ShapeDtypeStruct((1, D), F32)],
        compiler_params=_params("arbitrary"),
    )(dy, raw, g)


def _loss(name, h, tgt, pad_rows):
    L, D = h.shape
    nb = L // BLK
    assert pad_rows == BLK

    def body(h_ref, t_ref, dh_ref, s_ref):
        i = pl.program_id(0)

        @pl.when(i == 0)
        def _():
            s_ref[...] = jnp.zeros_like(s_ref)
            dh_ref[...] = jnp.zeros_like(dh_ref)

        @pl.when(i > 0)
        def _():
            e = h_ref[...] - t_ref[...]
            dh_ref[...] = e * (1.0 / D)
            s_ref[...] += jnp.sum(e * e, axis=0, keepdims=True)

    return pl.pallas_call(
        body, name=name, grid=(nb,),
        in_specs=[pl.BlockSpec((BLK, D), lambda i: (i, 0)), pl.BlockSpec((BLK, D), lambda i: (jnp.maximum(i - 1, 0), 0))],
        out_specs=[pl.BlockSpec((BLK, D), lambda i: (i, 0)), pl.BlockSpec((1, D), lambda i: (0, 0))],
        out_shape=[jax.ShapeDtypeStruct((L, D), F32), jax.ShapeDtypeStruct((1, D), F32)],
        compiler_params=_params("arbitrary"),
    )(h, tgt)


def _swiglu_up(name, xn, w1, w3):
    L, D = xn.shape
    Fd = w1.shape[1]
    tm, tn = _tile(L, 704, 16), _tile(Fd, 1408, LANES)

    def epi(accs, _):
        a, b = accs
        return [a, b, a * _sigmoid(a) * b]

    return _matmul(name, [(xn, w1), (xn, w3)], "nn", tm, tn, D, [BF16, BF16, BF16], epilogue=epi, separate=True)


def _swiglu_bwd(name, dh, w2, a, b):
    L, D = dh.shape
    Fd = w2.shape[0]
    tm, tn = _tile(L, 704, 16), _tile(Fd, 1408, LANES)

    def epi(accs, ex):
        dact = accs[0] * FFN_RES
        av, bv = ex[0].astype(F32), ex[1].astype(F32)
        s = _sigmoid(av)
        return [dact * bv * (s * (1.0 + av * (1.0 - s))), dact * av * s]

    return _matmul(name, [(dh, w2)], "nt", tm, tn, D, [BF16, BF16], epilogue=epi, extras=[(a, "mn"), (b, "mn")])


def _ffn_fwd(tag, h, g, w1, w3, w2):
    L, D = h.shape
    xn = _rms_fwd(f"{tag}_norm", h, g)
    a, b, act = _swiglu_up(f"{tag}_up", xn, w1, w3)
    tm = _tile(L, 704, 8)
    (h_out,) = _matmul(f"{tag}_down", [(act, w2)], "nn", tm, D, w2.shape[0], [F32],
                       epilogue=lambda accs, ex: [ex[0] + FFN_RES * accs[0]], extras=[(h, "mn")])
    return h_out, (xn, a, b, act)


def _ffn_bwd(tag, dh_out, h, g, w1, w3, w2, saved):
    xn, a, b, act = saved
    L, D = h.shape
    Fd = w2.shape[0]
    tl = _tile(L, 704, 16)
    da, db = _swiglu_bwd(f"{tag}_dact", dh_out, w2, a, b)
    (dw2,) = _matmul(f"{tag}_dw2", [(act, dh_out)], "tn", _tile(Fd, 1408, LANES), D, tl, [F32],
                     epilogue=lambda accs, ex: [FFN_RES * accs[0]])
    (dw1,) = _matmul(f"{tag}_dw1", [(xn, da)], "tn", D, _tile(Fd, 1408, LANES), tl, [F32])
    (dw3,) = _matmul(f"{tag}_dw3", [(xn, db)], "tn", D, _tile(Fd, 1408, LANES), tl, [F32])
    (dxn,) = _matmul(f"{tag}_dxn", [(da, w1), (db, w3)], "nt", _tile(L, 704, 8), _tile(D, 512, LANES), Fd, [F32])
    dh, dg = _rms_bwd(f"{tag}_dnorm", dxn, h, g, dh_out)
    return dh, dg.reshape(-1), dw1, dw3, dw2


def _conv_taps(ext, k):
    return ext if k == 0 else pltpu.roll(ext, k, axis=0)


def _conv_fwd(name, zx, col0, w, b, pad):
    L = zx.shape[0]
    C = w.shape[1]
    tr, tc = _tile(L, 704, 8), _tile(C, 512, LANES)
    cb = col0 // tc
    assert col0 % tc == 0

    def body(u_ref, halo_ref, w_ref, b_ref, o_ref):
        ext = jnp.concatenate([halo_ref[...], u_ref[...]], axis=0)
        pre = b_ref[...] + sum(_conv_taps(ext, 3 - k)[8:] * w_ref[k:k + 1, :] for k in range(4))
        rows = _iota(pre.shape, 0) + pl.program_id(1) * tr
        o_ref[...] = jnp.where(rows >= pad, pre * _sigmoid(pre), 0.0)

    return pl.pallas_call(
        body, name=name, grid=(C // tc, L // tr),
        in_specs=[pl.BlockSpec((tr, tc), lambda j, i: (i, cb + j)),
                  pl.BlockSpec((8, tc), lambda j, i: (jnp.maximum(i * (tr // 8) - 1, 0), cb + j)),
                  pl.BlockSpec((4, tc), lambda j, i: (0, j)), pl.BlockSpec((1, tc), lambda j, i: (0, j))],
        out_specs=pl.BlockSpec((tr, tc), lambda j, i: (i, j)),
        out_shape=jax.ShapeDtypeStruct((L, C), F32), compiler_params=_params("parallel", "parallel"),
    )(zx, zx, w, b)


def _conv_bwd_pre(name, dact, zx, col0, w, b, pad):
    L = zx.shape[0]
    C = w.shape[1]
    tr, tc = _tile(L, 704, 8), _tile(C, 512, LANES)
    cb = col0 // tc

    def body(d_ref, u_ref, halo_ref, w_ref, b_ref, dp_ref, dw_ref, db_ref):
        ext = jnp.concatenate([halo_ref[...], u_ref[...]], axis=0)
        taps = [_conv_taps(ext, 3 - k)[8:] for k in range(4)]
        pre = b_ref[...] + sum(taps[k] * w_ref[k:k + 1, :] for k in range(4))
        s = _sigmoid(pre)
        rows = _iota(pre.shape, 0) + pl.program_id(1) * tr
        dpre = jnp.where(rows >= pad, d_ref[...] * (s * (1.0 + pre * (1.0 - s))), 0.0)
        dp_ref[...] = dpre

        @pl.when(pl.program_id(1) == 0)
        def _():
            dw_ref[...] = jnp.zeros_like(dw_ref)
            db_ref[...] = jnp.zeros_like(db_ref)

        db_ref[...] += jnp.sum(dpre, axis=0, keepdims=True)
        dw_ref[...] += jnp.concatenate([jnp.sum(dpre * taps[k], axis=0, keepdims=True) for k in range(4)], axis=0)

    return pl.pallas_call(
        body, name=name, grid=(C // tc, L // tr),
        in_specs=[pl.BlockSpec((tr, tc), lambda j, i: (i, j)),
                  pl.BlockSpec((tr, tc), lambda j, i: (i, cb + j)),
                  pl.BlockSpec((8, tc), lambda j, i: (jnp.maximum(i * (tr // 8) - 1, 0), cb + j)),
                  pl.BlockSpec((4, tc), lambda j, i: (0, j)), pl.BlockSpec((1, tc), lambda j, i: (0, j))],
        out_specs=[pl.BlockSpec((tr, tc), lambda j, i: (i, j)), pl.BlockSpec((4, tc), lambda j, i: (0, j)),
                   pl.BlockSpec((1, tc), lambda j, i: (0, j))],
        out_shape=[jax.ShapeDtypeStruct((L, C), F32), jax.ShapeDtypeStruct((4, C), F32), jax.ShapeDtypeStruct((1, C), F32)],
        compiler_params=_params("parallel", "arbitrary"),
    )(dact, zx, zx, w, b)


def _conv_bwd_in(name, dpre, w):
    L, C = dpre.shape
    tr, tc = _tile(L, 704, 16), _tile(C, 512, LANES)
    nr = L // tr

    def body(d_ref, halo_ref, w_ref, o_ref):
        halo = jnp.where(pl.program_id(1) == nr - 1, 0.0, halo_ref[...])
        ext = jnp.concatenate([d_ref[...], halo], axis=0)
        acc = ext[:tr] * w_ref[3:4, :]
        for k in range(3):
            acc = acc + pltpu.roll(ext, tr + 8 - (3 - k), axis=0)[:tr] * w_ref[k:k + 1, :]
        o_ref[...] = acc.astype(BF16)

    return pl.pallas_call(
        body, name=name, grid=(C // tc, nr),
        in_specs=[pl.BlockSpec((tr, tc), lambda j, i: (i, j)),
                  pl.BlockSpec((8, tc), lambda j, i: (jnp.minimum((i + 1) * (tr // 8), L // 8 - 1), j)),
                  pl.BlockSpec((4, tc), lambda j, i: (0, j))],
        out_specs=pl.BlockSpec((tr, tc), lambda j, i: (i, j)),
        out_shape=jax.ShapeDtypeStruct((L, C), BF16), compiler_params=_params("parallel", "parallel"),
    )(dpre, dpre, w)


def _dt_fwd(name, zx, col0, bias_row, nheads, pad):
    L = zx.shape[0]
    cb = col0 // LANES

    def body(x_ref, b_ref, dt_ref, dtt_ref):
        v = x_ref[...] + b_ref[...]
        sp = jnp.maximum(v, 0.0) + jnp.log(1.0 + jnp.exp(-jnp.abs(v)))
        rows = _iota(v.shape, 0) + pl.program_id(0) * BLK
        dt = jnp.where((rows >= pad) & (_iota(v.shape, 1) < nheads), sp, 0.0)
        dt_ref[...] = dt
        dtt_ref[...] = dt.T

    return pl.pallas_call(
        body, name=name, grid=(L // BLK,),
        in_specs=[pl.BlockSpec((BLK, LANES), lambda i: (i, cb)), pl.BlockSpec((1, LANES), lambda i: (0, 0))],
        out_specs=[pl.BlockSpec((BLK, LANES), lambda i: (i, 0)), pl.BlockSpec((LANES, BLK), lambda i: (0, i))],
        out_shape=[jax.ShapeDtypeStruct((L, LANES), F32), jax.ShapeDtypeStruct((LANES, L), F32)],
        compiler_params=_params("parallel"),
    )(zx, bias_row)


def _dt_bwd(name, ddt, zx, col0, bias_row, nheads, pad):
    L = zx.shape[0]
    cb = col0 // LANES

    def body(d_ref, x_ref, b_ref, o_ref, db_ref):
        v = x_ref[...] + b_ref[...]
        rows = _iota(v.shape, 0) + pl.program_id(0) * BLK
        g = jnp.where((rows >= pad) & (_iota(v.shape, 1) < nheads), d_ref[...] * _sigmoid(v), 0.0)
        o_ref[...] = g.astype(BF16)

        @pl.when(pl.program_id(0) == 0)
        def _():
            db_ref[...] = jnp.zeros_like(db_ref)

        db_ref[...] += jnp.sum(g, axis=0, keepdims=True)

    return pl.pallas_call(
        body, name=name, grid=(L // BLK,),
        in_specs=[pl.BlockSpec((BLK, LANES), lambda i: (i, 0)), pl.BlockSpec((BLK, LANES), lambda i: (i, cb)),
                  pl.BlockSpec((1, LANES), lambda i: (0, 0))],
        out_specs=[pl.BlockSpec((BLK, LANES), lambda i: (i, 0)), pl.BlockSpec((1, LANES), lambda i: (0, 0))],
        out_shape=[jax.ShapeDtypeStruct((L, LANES), BF16), jax.ShapeDtypeStruct((1, LANES), F32)],
        compiler_params=_params("arbitrary"),
    )(ddt, zx, bias_row)


def _ssd_common(dt, dtt, a_row, a_col):
    tril = (_iota((BLK, BLK), 0) >= _iota((BLK, BLK), 1)).astype(BF16)
    cum = _dotx(tril, dt * a_row, split="b")
    cumt = _dotx(dtt * a_col, tril, NT)
    return cum, cumt


def _ssd_fwd(name, xbc, dt, dtt, a_row, a_col, expand, di, ng):
    L = xbc.shape[0]
    nc = L // BLK
    hpg = di // HEAD // ng
    gw = hpg * HEAD
    assert gw % LANES == 0

    def body(x_ref, dt_ref, dtt_ref, ar_ref, ac_ref, ex_ref, y_ref, st_ref, h_ref):
        @pl.when(pl.program_id(0) == 0)
        def _():
            h_ref[...] = jnp.zeros_like(h_ref)

        st_ref[0] = h_ref[...]
        dt, dtt = dt_ref[...], dtt_ref[...]
        cum, cumt = _ssd_common(dt, dtt, ar_ref[...], ac_ref[...])
        ex = ex_ref[...]
        ecum_x = _dotx(jnp.exp(cum), ex)
        wend_x = _dotx(jnp.exp(cum[BLK - 1:BLK, :] - cum) * dt, ex)
        ecl = jnp.broadcast_to(jnp.exp(cumt[:, BLK - 1:BLK]), (LANES, LANES))
        decay_h = _dotx(ex, ecl, TN, split="b")
        causal = _iota((BLK, BLK), 0) >= _iota((BLK, BLK), 1)
        low = _iota((BLK, LANES), 1) < HEAD
        for g in range(ng):
            xg = x_ref[:, g * gw:(g + 1) * gw]
            bg = x_ref[:, di + g * D_STATE:di + (g + 1) * D_STATE].astype(BF16)
            cg = x_ref[:, di + (ng + g) * D_STATE:di + (ng + g + 1) * D_STATE].astype(BF16)
            hg = h_ref[g * gw:(g + 1) * gw, :]
            gram = _dot(cg, bg, NT)
            yoff = _dot(cg, hg.astype(BF16), NT) * ecum_x[:, g * gw:(g + 1) * gw]
            parts = []
            for j in range(gw // LANES):
                xp = xg[:, j * LANES:(j + 1) * LANES].astype(BF16)
                yd = []
                for hh in range(2):
                    h = g * hpg + 2 * j + hh
                    seg = cum[:, h:h + 1] - cumt[h:h + 1, :]
                    m = gram * jnp.where(causal, jnp.exp(jnp.minimum(seg, 0.0)), 0.0) * dtt[h:h + 1, :]
                    yd.append(_dot(m.astype(BF16), xp))
                parts.append(jnp.where(low, yd[0], yd[1]))
            y_ref[:, g * gw:(g + 1) * gw] = jnp.concatenate(parts, axis=1) + yoff
            xw = (xg * wend_x[:, g * gw:(g + 1) * gw]).astype(BF16)
            h_ref[g * gw:(g + 1) * gw, :] = hg * decay_h[g * gw:(g + 1) * gw, :] + _dot(xw, bg, TN)

    W = xbc.shape[1]
    full = lambda r, c: pl.BlockSpec((r, c), lambda i: (0, 0))
    return pl.pallas_call(
        body, name=name, grid=(nc,),
        in_specs=[pl.BlockSpec((BLK, W), lambda i: (i, 0)), pl.BlockSpec((BLK, LANES), lambda i: (i, 0)),
                  pl.BlockSpec((LANES, BLK), lambda i: (0, i)), full(1, LANES), full(LANES, LANES), full(LANES, di)],
        out_specs=[pl.BlockSpec((BLK, di), lambda i: (i, 0)), pl.BlockSpec((1, di, D_STATE), lambda i: (i, 0, 0))],
        out_shape=[jax.ShapeDtypeStruct((L, di), F32), jax.ShapeDtypeStruct((nc, di, D_STATE), F32)],
        scratch_shapes=[pltpu.VMEM((di, D_STATE), F32)], compiler_params=_params("arbitrary"),
    )(xbc, dt, dtt, a_row, a_col, expand)


def _ssd_bwd(name, xbc, dt, dtt, a_row, a_col, expand, states, dy, d_x, di, ng):
    L, W = xbc.shape
    nc = L // BLK
    hpg = di // HEAD // ng
    gw = hpg * HEAD

    def body(x_ref, dt_ref, dtt_ref, ar_ref, ac_ref, ex_ref, st_ref, dy_ref, dx_ref_in, dxo_ref, ddt_ref, da_ref, dh_ref):
        @pl.when(pl.program_id(0) == 0)
        def _():
            dh_ref[...] = jnp.zeros_like(dh_ref)
            da_ref[...] = jnp.zeros_like(da_ref)

        dt, dtt, a_row = dt_ref[...], dtt_ref[...], ar_ref[...]
        cum, cumt = _ssd_common(dt, dtt, a_row, ac_ref[...])
        ex = ex_ref[...]
        ecum = jnp.exp(cum)
        ecum_x = _dotx(ecum, ex)
        e_s = jnp.exp(cum[BLK - 1:BLK, :] - cum)
        wend_x = _dotx(e_s * dt, ex)
        ecl_col = jnp.exp(cumt[:, BLK - 1:BLK])
        decay_h = _dotx(ex, jnp.broadcast_to(ecl_col, (LANES, LANES)), TN, split="b")
        causal = _iota((BLK, BLK), 0) >= _iota((BLK, BLK), 1)
        low = _iota((BLK, LANES), 1) < HEAD
        lane = _iota((1, LANES), 1)
        sub = _iota((LANES, 1), 0)
        dcum_c = jnp.zeros((BLK, LANES), F32)
        dcum_r = jnp.zeros((LANES, BLK), F32)
        ddt_r = jnp.zeros((LANES, BLK), F32)
        zoff = []
        dwend_src = []
        for g in range(ng):
            gs = slice(g * gw, (g + 1) * gw)
            xg = x_ref[:, gs]
            bg = x_ref[:, di + g * D_STATE:di + (g + 1) * D_STATE].astype(BF16)
            cg = x_ref[:, di + (ng + g) * D_STATE:di + (ng + g + 1) * D_STATE].astype(BF16)
            hprev = st_ref[0, gs, :]
            dhn = dh_ref[gs, :]
            dyg = dy_ref[:, gs]
            gram = _dot(cg, bg, NT)
            dgram = jnp.zeros((BLK, BLK), F32)
            dxg = []
            for j in range(gw // LANES):
                xp = xg[:, j * LANES:(j + 1) * LANES].astype(BF16)
                dyp = dyg[:, j * LANES:(j + 1) * LANES]
                dxh = []
                for hh in range(2):
                    h = g * hpg + 2 * j + hh
                    seg = cum[:, h:h + 1] - cumt[h:h + 1, :]
                    lm = jnp.where(causal, jnp.exp(jnp.minimum(seg, 0.0)), 0.0)
                    dtr = dtt[h:h + 1, :]
                    m = gram * lm * dtr
                    dym = jnp.where(low if hh == 0 else ~low, dyp, 0.0).astype(BF16)
                    dxh.append(_dot(m.astype(BF16), dym, TN))
                    dm = _dot(dym, xp, NT)
                    dgram = dgram + dm * lm * dtr
                    v = dm * gram * lm
                    wv = v * dtr
                    ddt_r = ddt_r + jnp.where(sub == h, jnp.sum(v, axis=0, keepdims=True), 0.0)
                    dcum_r = dcum_r - jnp.where(sub == h, jnp.sum(wv, axis=0, keepdims=True), 0.0)
                    dcum_c = dcum_c + jnp.where(lane == h, jnp.sum(wv, axis=1, keepdims=True), 0.0)
                dxg.append(jnp.where(low, dxh[0], dxh[1]))
            dx_diag = jnp.concatenate(dxg, axis=1)
            hb = hprev.astype(BF16)
            yoff = _dot(cg, hb, NT) * ecum_x[:, gs]
            dye = (dyg * ecum_x[:, gs]).astype(BF16)
            dcg = _dot(dye, hb) + _dot(dgram.astype(BF16), bg)
            dbg = _dot(dgram.astype(BF16), cg, TN)
            dh_prev = _dot(dye, cg, TN)
            zoff.append(dyg * yoff)
            dhb = dhn.astype(BF16)
            dxw = _dot(bg, dhb, NT)
            xw = (xg * wend_x[:, gs]).astype(BF16)
            dbg = dbg + _dot(xw, dhb)
            dwend_src.append(dxw * xg)
            dxo_ref[:, gs] = dx_diag + dxw * wend_x[:, gs] + dyg * dx_ref_in[:, gs]
            dxo_ref[:, di + g * D_STATE:di + (g + 1) * D_STATE] = dbg
            dxo_ref[:, di + (ng + g) * D_STATE:di + (ng + g + 1) * D_STATE] = dcg
            prod = dhn * hprev
            dd = jnp.sum(_dotx(ex[:, gs], prod, split="b"), axis=1, keepdims=True)
            dcum_r = dcum_r + jnp.where(_iota((1, BLK), 1) == BLK - 1, dd * ecl_col, 0.0)
            dh_ref[gs, :] = dhn * decay_h[gs, :] + dh_prev
        dcum_c = dcum_c + _dotx(jnp.concatenate(zoff, axis=1), ex, NT)
        dwend = _dotx(jnp.concatenate(dwend_src, axis=1), ex, NT)
        ddt_c = dwend * e_s
        de = dwend * dt * e_s
        dcum_c = dcum_c - de + jnp.where(_iota((BLK, 1), 0) == BLK - 1, jnp.sum(de, axis=0, keepdims=True), 0.0)
        dcum = dcum_c + dcum_r.T
        triu = (_iota((BLK, BLK), 0) <= _iota((BLK, BLK), 1)).astype(BF16)
        da = _dotx(triu, dcum, split="b")
        ddt_ref[...] = ddt_c + ddt_r.T + da * a_row
        da_ref[...] += jnp.sum(da * dt, axis=0, keepdims=True)

    rev = lambda i: nc - 1 - i
    full = lambda r, c: pl.BlockSpec((r, c), lambda i: (0, 0))
    return pl.pallas_call(
        body, name=name, grid=(nc,),
        in_specs=[pl.BlockSpec((BLK, W), lambda i: (rev(i), 0)), pl.BlockSpec((BLK, LANES), lambda i: (rev(i), 0)),
                  pl.BlockSpec((LANES, BLK), lambda i: (0, rev(i))), full(1, LANES), full(LANES, LANES), full(LANES, di),
                  pl.BlockSpec((1, di, D_STATE), lambda i: (rev(i), 0, 0)), pl.BlockSpec((BLK, di), lambda i: (rev(i), 0)),
                  full(1, di)],
        out_specs=[pl.BlockSpec((BLK, W), lambda i: (rev(i), 0)), pl.BlockSpec((BLK, LANES), lambda i: (rev(i), 0)),
                   full(1, LANES)],
        out_shape=[jax.ShapeDtypeStruct((L, W), F32), jax.ShapeDtypeStruct((L, LANES), F32),
                   jax.ShapeDtypeStruct((1, LANES), F32)],
        scratch_shapes=[pltpu.VMEM((di, D_STATE), F32)], compiler_params=_params("arbitrary"),
    )(xbc, dt, dtt, a_row, a_col, expand, states, dy, d_x)


def _group_sums(v, gsz):
    cols = []
    for j in range(0, v.shape[1], gsz):
        s = jnp.sum(v[:, j:j + gsz], axis=1, keepdims=True)
        cols.append(jnp.broadcast_to(s, (v.shape[0], gsz)))
    return jnp.concatenate(cols, axis=1)


def _gate_fwd(name, y, xbc, zx, d_x, ng_row, gsz):
    L, di = y.shape
    tr = _tile(L, 512, 16)

    def body(y_ref, x_ref, z_ref, d_ref, g_ref, o_ref):
        z = z_ref[...]
        y2 = (y_ref[...] + d_ref[...] * x_ref[...]) * (z * _sigmoid(z))
        r = lax.rsqrt(_group_sums(y2 * y2, gsz) * (1.0 / gsz) + RMS_EPS)
        o_ref[...] = (y2 * r * g_ref[...]).astype(BF16)

    row = pl.BlockSpec((tr, di), lambda i: (i, 0))
    vec = pl.BlockSpec((1, di), lambda i: (0, 0))
    return pl.pallas_call(
        body, name=name, grid=(L // tr,), in_specs=[row, row, row, vec, vec], out_specs=row,
        out_shape=jax.ShapeDtypeStruct((L, di), BF16), compiler_params=_params("parallel"),
    )(y, xbc, zx, d_x, ng_row)


def _gate_bwd(name, dy3, y, xbc, zx, d_x, ng_row, gsz):
    L, di = y.shape
    tr = _tile(L, 256, 16)

    def body(dy_ref, y_ref, x_ref, z_ref, d_ref, g_ref, dz_ref, dy1_ref, dg_ref, dd_ref):
        z, x = z_ref[...], x_ref[...]
        s = _sigmoid(z)
        sz = z * s
        y1 = y_ref[...] + d_ref[...] * x
        y2 = y1 * sz
        r = lax.rsqrt(_group_sums(y2 * y2, gsz) * (1.0 / gsz) + RMS_EPS)
        yg = y2 * r
        dy3 = dy_ref[...]
        dyg = dy3 * g_ref[...]
        dy2 = r * (dyg - yg * (_group_sums(dyg * yg, gsz) * (1.0 / gsz)))
        dz_ref[...] = (dy2 * y1 * (s * (1.0 + z * (1.0 - s)))).astype(BF16)
        dy1 = dy2 * sz
        dy1_ref[...] = dy1

        @pl.when(pl.program_id(0) == 0)
        def _():
            dg_ref[...] = jnp.zeros_like(dg_ref)
            dd_ref[...] = jnp.zeros_like(dd_ref)

        dg_ref[...] += jnp.sum(dy3 * yg, axis=0, keepdims=True)
        dd_ref[...] += jnp.sum(dy1 * x, axis=0, keepdims=True)

    row = pl.BlockSpec((tr, di), lambda i: (i, 0))
    vec = pl.BlockSpec((1, di), lambda i: (0, 0))
    return pl.pallas_call(
        body, name=name, grid=(L // tr,), in_specs=[row, row, row, row, vec, vec], out_specs=[row, row, vec, vec],
        out_shape=[jax.ShapeDtypeStruct((L, di), BF16), jax.ShapeDtypeStruct((L, di), F32),
                   jax.ShapeDtypeStruct((1, di), F32), jax.ShapeDtypeStruct((1, di), F32)],
        compiler_params=_params("arbitrary"),
    )(dy3, y, xbc, zx, d_x, ng_row)


SB_EDGE = 4
SB_INNER = 8


def _sb_heads(x):
    low = _iota(x.shape, 1) < HEAD
    zero = jnp.zeros_like(x)
    return jnp.concatenate([jnp.where(low, x, zero), jnp.where(low, zero, x)], axis=0)


def _sb_unheads(x2):
    return jnp.where(_iota((BLK, LANES), 1) < HEAD, x2[:BLK], x2[BLK:])


def _sb_tiles(qq, kblks, dmat, scol, thrs, s0s):
    n = range(len(kblks))
    z = [_dot(qq, kblks[u], NT) for u in n]
    e = [jnp.exp(-jnp.abs(z[u])) for u in n]
    l1 = [jnp.log(1.0 + e[u]) for u in n]
    lsz = [jnp.minimum(z[u], 0.0) - l1[u] for u in n]
    if thrs is None:
        return z, e, lsz, [-jnp.maximum(z[u], 0.0) - l1[u] for u in n], None
    valid = [(dmat > thrs[u]) & (scol >= s0s[u]) for u in n]
    lkm = [jnp.where(valid[u], -jnp.maximum(z[u], 0.0) - l1[u], 0.0) for u in n]
    return z, e, lsz, lkm, valid


def _sb_sweep(group, i, init):
    ue, ui = SB_EDGE, SB_INNER
    carry = group(True, ue, 0, init)
    n_inner = jnp.maximum((i - ue) // ui, 0)
    carry = lax.fori_loop(0, n_inner, lambda g, cr: group(False, ui, ue + g * ui, cr), carry)
    off = ue + n_inner * ui
    n_tail = jnp.maximum((i - off + ue) // ue, 0)
    return lax.fori_loop(0, n_tail, lambda g, cr: group(True, ue, off + g * ue, cr), carry)


def _sb_fwd(name, q, k, v, pad):
    L, D = q.shape
    nb = L // BLK

    def body(q_ref, k_ref, v_ref, o_ref, o32_ref):
        i = pl.program_id(1)
        after = (_iota((BLK, BLK), 0) > _iota((BLK, BLK), 1)).astype(BF16)
        qq = _sb_heads(q_ref[...] * 0.125)
        dmat = (_iota((2 * BLK, BLK), 0) & (BLK - 1)) - _iota((2 * BLK, BLK), 1)
        scol = _iota((2 * BLK, BLK), 1)

        def group(masked, U, off, carry):
            c, acc = carry
            rng = range(U)
            kraw = [i - (off + u) for u in rng]
            kb = [jnp.maximum(kraw[u], 0) for u in rng]
            rows = [pl.ds(pl.multiple_of(kb[u] * BLK, BLK), BLK) for u in rng]
            thr = [jnp.where(kraw[u] >= 0, (kb[u] - i) * BLK, BLK) for u in rng] if masked else None
            s0 = [pad - kb[u] * BLK for u in rng] if masked else None
            _, _, lsz, lkm, valid = _sb_tiles(qq, [k_ref[rows[u], :] for u in rng], dmat, scol, thr, s0)
            cum = [_dotx(lkm[u], after, parts=2) for u in rng]
            a = []
            for u in rng:
                w = jnp.exp(lsz[u] + (c + cum[u]))
                a.append((jnp.where(valid[u], w, 0.0) if masked else w).astype(BF16))
                c = c + jnp.sum(lkm[u], axis=1, keepdims=True)
            for u in rng:
                acc = acc + _dot(a[u], v_ref[rows[u], :])
            return c, acc

        _, acc = _sb_sweep(group, i, (jnp.zeros((2 * BLK, 1), F32), jnp.zeros((2 * BLK, LANES), F32)))
        out = _sb_unheads(acc)
        o_ref[...] = out.astype(BF16)
        o32_ref[...] = out

    blk = pl.BlockSpec((BLK, LANES), lambda p, i: (i, p))
    col = pl.BlockSpec((L, LANES), lambda p, i: (0, p))
    return pl.pallas_call(
        body, name=name, grid=(D // LANES, nb), in_specs=[blk, col, col], out_specs=[blk, blk],
        out_shape=[jax.ShapeDtypeStruct((L, D), BF16), jax.ShapeDtypeStruct((L, D), F32)],
        compiler_params=_params("parallel", "arbitrary"),
    )(q, k, v)


def _sb_bwd(name, q, k, v, do, o32, pad):
    L, D = q.shape
    nb = L // BLK

    def body(q_ref, k_ref, v_ref, do_ref, o_ref, dq_ref, dk_ref, dv_ref):
        i = pl.program_id(1)

        @pl.when(i == 0)
        def _():
            dk_ref[...] = jnp.zeros_like(dk_ref)
            dv_ref[...] = jnp.zeros_like(dv_ref)

        after = (_iota((BLK, BLK), 0) > _iota((BLK, BLK), 1)).astype(BF16)
        from_j = (_iota((BLK, BLK), 0) >= _iota((BLK, BLK), 1)).astype(BF16)
        qq = _sb_heads(q_ref[...] * 0.125)
        dd = _sb_heads(do_ref[...])
        dmat = (_iota((2 * BLK, BLK), 0) & (BLK - 1)) - _iota((2 * BLK, BLK), 1)
        scol = _iota((2 * BLK, BLK), 1)
        o2 = jnp.concatenate([o_ref[...], o_ref[...]], axis=0)
        total = jnp.sum(dd.astype(F32) * o2, axis=1, keepdims=True)

        def group(masked, U, off, carry):
            c, met, acc = carry
            rng = range(U)
            kraw = [i - (off + u) for u in rng]
            kb = [jnp.maximum(kraw[u], 0) for u in rng]
            rows = [pl.ds(pl.multiple_of(kb[u] * BLK, BLK), BLK) for u in rng]
            thr = [jnp.where(kraw[u] >= 0, (kb[u] - i) * BLK, BLK) for u in rng] if masked else None
            s0 = [pad - kb[u] * BLK for u in rng] if masked else None
            kblk = [k_ref[rows[u], :] for u in rng]
            z, e, lsz, lkm, valid = _sb_tiles(qq, kblk, dmat, scol, thr, s0)
            da = [_dot(dd, v_ref[rows[u], :], NT) for u in rng]
            cum = [_dotx(lkm[u], after, parts=2) for u in rng]
            a = []
            for u in rng:
                w = jnp.exp(lsz[u] + (c + cum[u]))
                a.append((jnp.where(valid[u], w, 0.0) if masked else w).astype(BF16))
                c = c + jnp.sum(lkm[u], axis=1, keepdims=True)
            dlog = [da[u] * a[u].astype(F32) for u in rng]
            here = [_dotx(dlog[u], from_j, parts=2) for u in rng]
            dz = []
            for u in rng:
                inv = 1.0 / (1.0 + e[u])
                sig = jnp.where(z[u] >= 0.0, inv, e[u] * inv)
                pull = (total - (met + here[u])) * sig
                dz.append((dlog[u] * (1.0 - sig) - (jnp.where(valid[u], pull, 0.0) if masked else pull)).astype(BF16))
                met = met + jnp.sum(dlog[u], axis=1, keepdims=True)
            for u in rng:
                dk_ref[rows[u], :] += _dot(dz[u], qq, TN)
                dv_ref[rows[u], :] += _dot(a[u], dd, TN)
                acc = acc + _dot(dz[u], kblk[u])
            return c, met, acc

        col1 = jnp.zeros((2 * BLK, 1), F32)
        _, _, acc = _sb_sweep(group, i, (col1, col1, jnp.zeros((2 * BLK, LANES), F32)))
        dq_ref[...] = _sb_unheads(acc) * 0.125

    blk = pl.BlockSpec((BLK, LANES), lambda p, i: (i, p))
    col = pl.BlockSpec((L, LANES), lambda p, i: (0, p))
    return pl.pallas_call(
        body, name=name, grid=(D // LANES, nb), in_specs=[blk, col, col, blk, blk], out_specs=[blk, col, col],
        out_shape=[jax.ShapeDtypeStruct((L, D), F32)] * 3, compiler_params=_params("parallel", "arbitrary"),
    )(q, k, v, do, o32)


def _local_step(x, tgt, w):
    S, D = x.shape
    nm = w["meta_tokens"].shape[0]
    pad = BLK - nm
    L = pad + nm + S
    assert L % BLK == 0 and 0 < nm <= BLK
    di = w["ssm_out_proj"].shape[0]
    nh = w["ssm_dt_bias"].shape[0]
    assert di == nh * HEAD and nh <= LANES
    conv_dim = w["ssm_conv_w"].shape[1]
    ng = (conv_dim - di) // (2 * D_STATE)
    zp = di + conv_dim + LANES
    g = w["norm_g"]
    grads = {}

    h0 = jnp.concatenate([jnp.zeros((pad, D), F32), w["meta_tokens"], x], axis=0)

    h1, s1 = _ffn_fwd("f00", h0, g[0, 0], w["ffn_w1"][0, 0], w["ffn_w3"][0, 0], w["ffn_w2"][0, 0])
    u0 = _rms_fwd("m_norm", h1, g[0, 1])
    w_in = jnp.concatenate([w["ssm_in_proj"], jnp.zeros((D, zp - w["ssm_in_proj"].shape[1]), BF16)], axis=1)
    tl = _tile(L, 704, 16)
    (zx,) = _matmul("m_inproj", [(u0, w_in)], "nn", tl, _tile(zp, 1024, LANES), D, [F32])
    conv_b = w["ssm_conv_b"].reshape(1, conv_dim)
    xbc = _conv_fwd("m_conv", zx, di, w["ssm_conv_w"], conv_b, pad)
    bias_row = jnp.zeros((1, LANES), F32).at[0, :nh].set(w["ssm_dt_bias"])
    dt, dtt = _dt_fwd("m_dt", zx, di + conv_dim, bias_row, nh, pad)
    a_neg = -jnp.exp(w["ssm_a_log"])
    a_row = jnp.zeros((1, LANES), F32).at[0, :nh].set(a_neg)
    a_col = jnp.broadcast_to(a_row.reshape(LANES, 1), (LANES, LANES))
    expand = (jnp.arange(LANES)[:, None] == (jnp.arange(di) // HEAD)[None, :]).astype(BF16)
    y_ssd, states = _ssd_fwd("m_ssd", xbc, dt, dtt, a_row, a_col, expand, di, ng)
    d_x = jnp.repeat(w["ssm_d"], HEAD).reshape(1, di)
    ssm_g = w["ssm_norm_g"].reshape(1, di)
    gsz = di // ng
    y3 = _gate_fwd("m_gate", y_ssd, xbc, zx, d_x, ssm_g, gsz)
    (h2,) = _matmul("m_outproj", [(y3, w["ssm_out_proj"])], "nn", tl, D, di, [F32],
                    epilogue=lambda accs, ex: [ex[0] + accs[0]], extras=[(h1, "mn")])
    h3, s2 = _ffn_fwd("f01", h2, g[0, 2], w["ffn_w1"][0, 1], w["ffn_w3"][0, 1], w["ffn_w2"][0, 1])

    kv_in = _rms_fwd("kv_norm", h3, w["kv_norm_g"])
    (k_raw,) = _matmul("kv_k", [(kv_in, w["w_k"])], "nn", tl, D, D, [F32])
    (v_sh,) = _matmul("kv_v", [(kv_in, w["w_v"])], "nn", tl, D, D, [BF16])
    kg = jnp.tile(w["k_norm_g"], D // HEAD).reshape(1, D)
    k_sh = _headrms_fwd("kv_knorm", k_raw, kg)

    h4, s3 = _ffn_fwd("f10", h3, g[1, 0], w["ffn_w1"][1, 0], w["ffn_w3"][1, 0], w["ffn_w2"][1, 0])
    u1 = _rms_fwd("a_norm", h4, g[1, 1])
    (q_raw,) = _matmul("a_q", [(u1, w["sb_w_q"])], "nn", tl, D, D, [F32])
    qg = jnp.tile(w["sb_q_norm_g"], D // HEAD).reshape(1, D)
    q = _headrms_fwd("a_qnorm", q_raw, qg)
    o, o32 = _sb_fwd("a_attn", q, k_sh, v_sh, pad)
    (h5,) = _matmul("a_o", [(o, w["sb_w_o"])], "nn", tl, D, D, [F32],
                    epilogue=lambda accs, ex: [ex[0] + accs[0]], extras=[(h4, "mn")])
    h6, s4 = _ffn_fwd("f11", h5, g[1, 2], w["ffn_w1"][1, 1], w["ffn_w3"][1, 1], w["ffn_w2"][1, 1])

    dh6, sq = _loss("loss", h6, tgt, pad + nm)

    dg = jnp.zeros_like(g)
    dw1 = [[None, None], [None, None]]
    dw3 = [[None, None], [None, None]]
    dw2 = [[None, None], [None, None]]
    dh5, dgv, dw1[1][1], dw3[1][1], dw2[1][1] = _ffn_bwd("b11", dh6, h5, g[1, 2], w["ffn_w1"][1, 1], w["ffn_w3"][1, 1],
                                                           w["ffn_w2"][1, 1], s4)
    dg = dg.at[1, 2].set(dgv)
    td = _tile(D, 512, LANES)
    (do,) = _matmul("b_do", [(dh5, w["sb_w_o"])], "nt", tl, D, D, [BF16])
    (grads["sb_w_o"],) = _matmul("b_dwo", [(o, dh5)], "tn", D, td, tl, [F32])
    dq, dk, dv = _sb_bwd("b_attn", q, k_sh, v_sh, do, o32, pad)
    dq_raw, dqg = _headrms_bwd("b_qnorm", dq, q_raw, qg)
    grads["sb_q_norm_g"] = dqg.reshape(D // HEAD, HEAD).sum(0)
    (grads["sb_w_q"],) = _matmul("b_dwq", [(u1, dq_raw)], "tn", D, td, tl, [F32])
    (du1,) = _matmul("b_du1", [(dq_raw, w["sb_w_q"])], "nt", tl, D, D, [F32])
    dh4, dgv = _rms_bwd("b_anorm", du1, h4, g[1, 1], dh5)
    dg = dg.at[1, 1].set(dgv.reshape(-1))
    dh3, dgv, dw1[1][0], dw3[1][0], dw2[1][0] = _ffn_bwd("b10", dh4, h3, g[1, 0], w["ffn_w1"][1, 0], w["ffn_w3"][1, 0],
                                                           w["ffn_w2"][1, 0], s3)
    dg = dg.at[1, 0].set(dgv)

    dk_raw, dkg = _headrms_bwd("b_knorm", dk, k_raw, kg)
    grads["k_norm_g"] = dkg.reshape(D // HEAD, HEAD).sum(0)
    (grads["w_k"],) = _matmul("b_dwk", [(kv_in, dk_raw)], "tn", D, td, tl, [F32])
    (grads["w_v"],) = _matmul("b_dwv", [(kv_in, dv)], "tn", D, td, tl, [F32])
    (dkv_in,) = _matmul("b_dkvin", [(dk_raw, w["w_k"]), (dv, w["w_v"])], "nt", tl, D, D, [F32])
    dh3, dgv = _rms_bwd("b_kvnorm", dkv_in, h3, w["kv_norm_g"], dh3)
    grads["kv_norm_g"] = dgv.reshape(-1)

    dh2, dgv, dw1[0][1], dw3[0][1], dw2[0][1] = _ffn_bwd("b01", dh3, h2, g[0, 2], w["ffn_w1"][0, 1], w["ffn_w3"][0, 1],
                                                           w["ffn_w2"][0, 1], s2)
    dg = dg.at[0, 2].set(dgv)
    (dy3,) = _matmul("b_dy3", [(dh2, w["ssm_out_proj"])], "nt", tl, _tile(di, 1024, LANES), D, [F32])
    (grads["ssm_out_proj"],) = _matmul("b_dwout", [(y3, dh2)], "tn", _tile(di, 1024, LANES), D, tl, [F32])
    dz, dy1, dssm_g, dd_x = _gate_bwd("b_gate", dy3, y_ssd, xbc, zx, d_x, ssm_g, gsz)
    grads["ssm_norm_g"] = dssm_g.reshape(-1)
    grads["ssm_d"] = dd_x.reshape(nh, HEAD).sum(1)
    dxbc, ddt, da = _ssd_bwd("b_ssd", xbc, dt, dtt, a_row, a_col, expand, states, dy1, d_x, di, ng)
    grads["ssm_a_log"] = da[0, :nh] * a_neg
    ddt_raw, dbias = _dt_bwd("b_dt", ddt, zx, di + conv_dim, bias_row, nh, pad)
    grads["ssm_dt_bias"] = dbias[0, :nh]
    dpre, grads["ssm_conv_w"], dconv_b = _conv_bwd_pre("b_convpre", dxbc, zx, di, w["ssm_conv_w"], conv_b, pad)
    grads["ssm_conv_b"] = dconv_b.reshape(-1)
    dxbc_raw = _conv_bwd_in("b_convin", dpre, w["ssm_conv_w"])
    dzx = jnp.concatenate([dz, dxbc_raw, ddt_raw], axis=1)
    (dw_in,) = _matmul("b_dwin", [(u0, dzx)], "tn", D, _tile(zp, 1024, LANES), tl, [F32])
    grads["ssm_in_proj"] = dw_in[:, :w["ssm_in_proj"].shape[1]]
    (du0,) = _matmul("b_du0", [(dzx, w_in)], "nt", _tile(L, 352, 16), td, zp, [F32])
    dh1, dgv = _rms_bwd("b_mnorm", du0, h1, g[0, 1], dh2)
    dg = dg.at[0, 1].set(dgv.reshape(-1))
    dh0, dgv, dw1[0][0], dw3[0][0], dw2[0][0] = _ffn_bwd("b00", dh1, h0, g[0, 0], w["ffn_w1"][0, 0], w["ffn_w3"][0, 0],
                                                           w["ffn_w2"][0, 0], s1)
    dg = dg.at[0, 0].set(dgv)

    grads["norm_g"] = dg
    grads["ffn_w1"] = jnp.stack([jnp.stack(r) for r in dw1])
    grads["ffn_w3"] = jnp.stack([jnp.stack(r) for r in dw3])
    grads["ffn_w2"] = jnp.stack([jnp.stack(r) for r in dw2])
    grads["meta_tokens"] = dh0[pad:pad + nm]
    return sq, dh0[pad + nm:], grads


HBM_SPEC = pl.BlockSpec(memory_space=pltpu.HBM)
PAIR_PIECES = 8


def _place():
    return lax.axis_index("x"), lax.axis_index("y"), lax.axis_index("c")


def _allgather8(name, blk):
    m, n = blk.shape

    def body(x_ref, out_ref, send_sems, recv_sems, local_sem):
        x, y, c = _place()
        me, sibling = (x, y, c), (x, y, 1 - c)
        chips = [(1 - x, y), (x, 1 - y), (1 - x, 1 - y)]

        def rows(px, py, pc):
            return out_ref.at[pl.ds((4 * px + 2 * py + pc) * m, m), :]

        def copy(k, block, to, src=None):
            return pltpu.make_async_remote_copy(
                src_ref=rows(*block) if src is None else src, dst_ref=rows(*block),
                send_sem=send_sems.at[k], recv_sem=recv_sems.at[k], device_id=to, device_id_type=MESH)

        mine = pltpu.make_async_copy(x_ref, rows(*me), local_sem)
        mine.start()
        first = [copy(0, me, sibling, src=x_ref)]
        first += [copy(1 + j, me, (*chip, c), src=x_ref) for j, chip in enumerate(chips)]
        for cp in first:
            cp.start()
        passed = [copy(4 + j, (*chip, c), sibling) for j, chip in enumerate(chips)]
        for j, chip in enumerate(chips):
            copy(1 + j, (*chip, c), me).wait_recv()
            passed[j].start()
        copy(0, sibling, me).wait_recv()
        for j, chip in enumerate(chips):
            copy(4 + j, (*chip, 1 - c), me).wait_recv()
        for cp in first + passed:
            cp.wait_send()
        mine.wait()

    return pl.pallas_call(
        body, name=name, out_shape=jax.ShapeDtypeStruct((8 * m, n), blk.dtype),
        in_specs=[HBM_SPEC], out_specs=HBM_SPEC,
        scratch_shapes=[pltpu.SemaphoreType.DMA((7,)), pltpu.SemaphoreType.DMA((7,)), pltpu.SemaphoreType.DMA],
    )(blk)


def _exchange8(name, g):
    _, m, n = g.shape

    def body(g_ref, out_ref, send_sems, recv_sems, local_sem):
        x, y, c = _place()
        me_id = 4 * x + 2 * y + c
        mine = pltpu.make_async_copy(g_ref.at[me_id], out_ref.at[me_id], local_sem)
        mine.start()
        sends, recvs = [], []
        for k in range(1, 8):
            px = 1 - x if k & 4 else x
            py = 1 - y if k & 2 else y
            pc = 1 - c if k & 1 else c
            pid = 4 * px + 2 * py + pc
            sends.append(pltpu.make_async_remote_copy(
                src_ref=g_ref.at[pid], dst_ref=out_ref.at[me_id], send_sem=send_sems.at[k - 1],
                recv_sem=recv_sems.at[k - 1], device_id=(px, py, pc), device_id_type=MESH))
            recvs.append(pltpu.make_async_remote_copy(
                src_ref=g_ref.at[me_id], dst_ref=out_ref.at[pid], send_sem=send_sems.at[k - 1],
                recv_sem=recv_sems.at[k - 1], device_id=(px, py, pc), device_id_type=MESH))
        for cp in sends:
            cp.start()
        for cp in recvs:
            cp.wait_recv()
        for cp in sends:
            cp.wait_send()
        mine.wait()

    return pl.pallas_call(
        body, name=name, out_shape=jax.ShapeDtypeStruct(g.shape, g.dtype), in_specs=[HBM_SPEC], out_specs=HBM_SPEC,
        scratch_shapes=[pltpu.SemaphoreType.DMA((7,)), pltpu.SemaphoreType.DMA((7,)), pltpu.SemaphoreType.DMA],
    )(g)


def _pairshare(name, half):
    m, n = half.shape
    pieces = PAIR_PIECES if m % (8 * PAIR_PIECES) == 0 else 1
    pm = m // pieces

    def body(x_ref, got_ref, send_sems, recv_sems):
        x, y, c = _place()

        def copy(j):
            rows = pl.ds(j * pm, pm)
            return pltpu.make_async_remote_copy(
                src_ref=x_ref.at[rows, :], dst_ref=got_ref.at[rows, :], send_sem=send_sems.at[j],
                recv_sem=recv_sems.at[j], device_id=(x, y, 1 - c), device_id_type=MESH)

        copies = [copy(j) for j in range(pieces)]
        for cp in copies:
            cp.start()
        for cp in copies:
            cp.wait()

    got = pl.pallas_call(
        body, name=name, out_shape=jax.ShapeDtypeStruct((m, n), half.dtype), in_specs=[HBM_SPEC], out_specs=HBM_SPEC,
        scratch_shapes=[pltpu.SemaphoreType.DMA((pieces,)), pltpu.SemaphoreType.DMA((pieces,))],
    )(half)
    south = lax.axis_index("c") == 0
    return jnp.concatenate([jnp.where(south, half, got), jnp.where(south, got, half)], axis=0)


def _sum8(name, parts):
    _, m, n = parts.shape
    tr = _tile(m, 512, 16)

    def body(p_ref, o_ref):
        acc = p_ref[0].astype(F32)
        for s in range(1, 8):
            acc = acc + p_ref[s].astype(F32)
        o_ref[...] = acc

    return pl.pallas_call(
        body, name=name, grid=(m // tr,), in_specs=[pl.BlockSpec((8, tr, n), lambda i: (0, i, 0))],
        out_specs=pl.BlockSpec((tr, n), lambda i: (i, 0)), out_shape=jax.ShapeDtypeStruct((m, n), F32),
        compiler_params=_params("parallel"),
    )(parts)


def _adamw(name, w, g, m, v):
    shape = w.shape
    cols = shape[-1]
    rows = math.prod(shape[:-1])
    tr = _tile(rows, 512, 8) if rows * cols > 2 ** 19 else rows

    def body(w_ref, g_ref, m_ref, v_ref, d_ref, mo_ref, vo_ref):
        gv = g_ref[...]
        m2 = ADAM_B1 * m_ref[...] + (1.0 - ADAM_B1) * gv
        v2 = ADAM_B2 * v_ref[...] + (1.0 - ADAM_B2) * (gv * gv)
        m_hat = m2 / (1.0 - ADAM_B1 ** ADAM_STEP)
        v_hat = v2 / (1.0 - ADAM_B2 ** ADAM_STEP)
        d_ref[...] = -ADAM_LR * (m_hat / (jnp.sqrt(v_hat) + ADAM_EPS) + ADAM_WD * w_ref[...])
        mo_ref[...] = m2
        vo_ref[...] = v2

    spec = pl.BlockSpec((tr, cols), lambda i: (i, 0))
    outs = pl.pallas_call(
        body, name=name, grid=(rows // tr,), in_specs=[spec] * 4, out_specs=[spec] * 3,
        out_shape=[jax.ShapeDtypeStruct((rows, cols), F32)] * 3, compiler_params=_params("parallel"),
    )(*(a.reshape(rows, cols) for a in (w, g, m, v)))
    return tuple(o.reshape(shape) for o in outs)


WEIGHTS = ["meta_tokens", "norm_g", "ffn_w1", "ffn_w3", "ffn_w2", "ssm_in_proj", "ssm_conv_w", "ssm_conv_b", "ssm_dt_bias",
           "ssm_a_log", "ssm_d", "ssm_norm_g", "ssm_out_proj", "kv_norm_g", "w_k", "k_norm_g", "w_v", "sb_w_q",
           "sb_q_norm_g", "sb_w_o"]
SHARD_AXIS = {"meta_tokens": 1, "norm_g": 2, "ffn_w1": 3, "ffn_w3": 3, "ffn_w2": 2, "ssm_in_proj": 2, "ssm_conv_w": 2,
              "ssm_conv_b": 1, "ssm_norm_g": 1, "ssm_out_proj": 1, "w_k": 0, "w_v": 0, "sb_w_q": 1, "sb_w_o": 1}
MATRICES = ["ffn_w1", "ffn_w3", "ffn_w2", "ssm_in_proj", "ssm_out_proj", "w_k", "w_v", "sb_w_q", "sb_w_o"]
VECTORS = [n for n in WEIGHTS if n in SHARD_AXIS and n not in MATRICES]
REPLICATED = [n for n in WEIGHTS if n not in SHARD_AXIS]
LAYER_AXIS = ("ssm_", "sb_")
PACK_COLS = 1024
N_CHIPS = 4


def _pack(arrays, row_mult, dtype):
    segs = []
    for a in arrays:
        n = math.prod(a.shape)
        r = -(-n // PACK_COLS)
        flat = a.reshape(-1).astype(dtype)
        if r * PACK_COLS != n:
            flat = jnp.pad(flat, (0, r * PACK_COLS - n))
        segs.append(flat.reshape(r, PACK_COLS))
    rows = sum(s.shape[0] for s in segs)
    extra = -rows % row_mult
    if extra:
        segs.append(jnp.zeros((extra, PACK_COLS), dtype))
    return jnp.concatenate(segs, axis=0)


def _unpack(packed, shapes):
    lead = packed.shape[:-2]
    out, r0 = [], 0
    for shp in shapes:
        n = math.prod(shp)
        r = -(-n // PACK_COLS)
        seg = packed[..., r0:r0 + r, :]
        if r * PACK_COLS != n:
            seg = seg.reshape(*lead, r * PACK_COLS)[..., :n]
        out.append(seg.reshape(*lead, *shp))
        r0 += r
    return out


def _join(stack, axis):
    return jnp.concatenate([stack[s] for s in range(N_CHIPS)], axis=axis)


def _shards(full, axis):
    n = full.shape[axis] // N_CHIPS
    return [lax.slice_in_dim(full, s * n, (s + 1) * n, axis=axis) for s in range(N_CHIPS)]


def _drop_layer(name, a):
    return a[0] if name.startswith(LAYER_AXIS) else a


def kernel(x, meta_tokens, norm_g, ffn_w1, ffn_w3, ffn_w2, ssm_in_proj, ssm_conv_w, ssm_conv_b, ssm_dt_bias, ssm_a_log, ssm_d, ssm_norm_g, ssm_out_proj, kv_norm_g, w_k, k_norm_g, w_v, sb_w_q, sb_q_norm_g, sb_w_o, loss_target, m_meta_tokens, m_norm_g, m_ffn_w1, m_ffn_w3, m_ffn_w2, m_ssm_in_proj, m_ssm_conv_w, m_ssm_conv_b, m_ssm_dt_bias, m_ssm_a_log, m_ssm_d, m_ssm_norm_g, m_ssm_out_proj, m_kv_norm_g, m_w_k, m_k_norm_g, m_w_v, m_sb_w_q, m_sb_q_norm_g, m_sb_w_o, v_meta_tokens, v_norm_g, v_ffn_w1, v_ffn_w3, v_ffn_w2, v_ssm_in_proj, v_ssm_conv_w, v_ssm_conv_b, v_ssm_dt_bias, v_ssm_a_log, v_ssm_d, v_ssm_norm_g, v_ssm_out_proj, v_kv_norm_g, v_w_k, v_k_norm_g, v_w_v, v_sb_w_q, v_sb_q_norm_g, v_sb_w_o):
    args = locals()
    w_in = {n: args[n] for n in WEIGHTS}
    m_in = {n: args["m_" + n] for n in WEIGHTS}
    v_in = {n: args["v_" + n] for n in WEIGHTS}
    c = lax.axis_index("c")

    def gather(names, dtype, row_mult):
        packed = _pack([w_in[n] for n in names], 2 * row_mult, dtype)
        half = packed.shape[0] // 2
        mine = lax.dynamic_slice_in_dim(packed, c * half, half, axis=0)
        allp = _allgather8(f"gather_{jnp.dtype(dtype).name}", mine).reshape(N_CHIPS, 2 * half, PACK_COLS)
        stacks = _unpack(allp, [w_in[n].shape for n in names])
        return {n: _join(s, SHARD_AXIS[n]) for n, s in zip(names, stacks)}

    full = {**gather(MATRICES, BF16, 16), **gather(VECTORS, F32, 8)}
    full.update({n: w_in[n] for n in REPLICATED})
    full = {n: _drop_layer(n, a) for n, a in full.items()}

    sq, grad_x, grads = _local_step(x[0], loss_target[0], full)
    loss = lax.psum(0.5 / x.shape[-1] * jnp.sum(sq), ("x", "y", "c"))
    grads = {n: (g[None] if n.startswith(LAYER_AXIS) else g) for n, g in grads.items()}

    sharded = MATRICES + VECTORS
    per_chip = [[] for _ in range(N_CHIPS)]
    for n in sharded:
        for p, s in enumerate(_shards(grads[n], SHARD_AXIS[n])):
            per_chip[p].append(s)
    contrib = jnp.stack([_pack(parts, 16 * PAIR_PIECES, BF16) for parts in per_chip])
    rows = contrib.shape[1]
    got = _exchange8("grad_exchange", contrib.reshape(2 * N_CHIPS, rows // 2, PACK_COLS))
    reduced = _pairshare("grad_share", _sum8("grad_sum", got))
    g_out = dict(zip(sharded, _unpack(reduced, [w_in[n].shape for n in sharded])))

    rep = _pack([grads[n] for n in REPLICATED], 8, F32)
    rep_sum = _sum8("rep_sum", _allgather8("rep_gather", rep).reshape(8, rep.shape[0], PACK_COLS))
    g_out.update(zip(REPLICATED, _unpack(rep_sum, [w_in[n].shape for n in REPLICATED])))

    delta, new_m, new_v = {}, {}, {}
    for n in WEIGHTS:
        delta[n], new_m[n], new_v[n] = _adamw(f"adamw_{n}", w_in[n], g_out[n], m_in[n], v_in[n])
    return (loss, grad_x[None], *[g_out[n] for n in WEIGHTS], *[delta[n] for n in WEIGHTS],
            *[new_m[n] for n in WEIGHTS], *[new_v[n] for n in WEIGHTS])
```

```python
import functools
import math

import jax
import jax.numpy as jnp
from jax import lax
from jax.experimental import pallas as pl
from jax.experimental.pallas import tpu as pltpu

F32, BF16 = jnp.float32, jnp.bfloat16
RMS_EPS = 1e-6
LANES = 128
HEAD = 64
D_STATE = 128
BLK = 128
FFN_RES = 0.5
VMEM_LIMIT = 56 * 2 ** 20
ADAM_LR, ADAM_B1, ADAM_B2, ADAM_EPS, ADAM_WD, ADAM_STEP = 0.001, 0.9, 0.999, 1e-08, 0.01, 10
MESH = pl.DeviceIdType.MESH

NN = (((1,), (0,)), ((), ()))
NT = (((1,), (1,)), ((), ()))
TN = (((0,), (0,)), ((), ()))


def _dot(a, b, dn=NN):
    return lax.dot_general(a, b, dn, preferred_element_type=F32)


def _split(x, parts):
    out = []
    for _ in range(parts):
        p = x.astype(BF16)
        out.append(p)
        x = x - p.astype(F32)
    return out


def _dotx(a, b, dn=NN, parts=3, split="a"):
    if split == "a":
        return sum(_dot(p, b, dn) for p in _split(a, parts))
    return sum(_dot(a, p, dn) for p in _split(b, parts))


def _tile(n, target, mult):
    best = None
    for d in range(mult, min(n, target) + 1, mult):
        if n % d == 0:
            best = d
    return n if best is None else best


def _params(*sem):
    return pltpu.CompilerParams(dimension_semantics=tuple(sem) if sem else None, vmem_limit_bytes=VMEM_LIMIT)


def _iota(shape, axis):
    return lax.broadcasted_iota(jnp.int32, shape, axis)


def _sigmoid(x):
    return 1.0 / (1.0 + jnp.exp(-x))


def _matmul(name, pairs, mode, tm, tn, tk, out_dtypes, epilogue=None, extras=(), separate=False):
    a0, b0 = pairs[0]
    if mode == "nn":
        (M, K), N = a0.shape, b0.shape[1]
    elif mode == "nt":
        (M, K), N = a0.shape, b0.shape[0]
    else:
        (K, M), N = a0.shape, b0.shape[1]
    assert M % tm == 0 and N % tn == 0 and K % tk == 0, (name, M, N, K, tm, tn, tk)
    nM, nN, nK = M // tm, N // tn, K // tk
    np_, ne, no = len(pairs), len(extras), len(out_dtypes)
    n_acc = np_ if separate else 1
    dn = {"nn": NN, "nt": NT, "tn": TN}[mode]

    def body(*refs):
        ab, ex = refs[:2 * np_], refs[2 * np_:2 * np_ + ne]
        outs, accs = refs[2 * np_ + ne:2 * np_ + ne + no], refs[2 * np_ + ne + no:]
        k = pl.program_id(2)

        def prod(i):
            return _dot(ab[2 * i][...].astype(BF16), ab[2 * i + 1][...].astype(BF16), dn)

        ps = [prod(i) for i in range(np_)]
        if not separate:
            ps = [functools.reduce(lambda u, v: u + v, ps)]

        def finish(vals):
            res = epilogue(vals, [e[...] for e in ex]) if epilogue is not None else vals
            for o, r in zip(outs, res):
                o[...] = r.astype(o.dtype)

        if nK == 1:
            finish(ps)
        else:
            @pl.when(k == 0)
            def _():
                for acc, p in zip(accs, ps):
                    acc[...] = p

            @pl.when(k > 0)
            def _():
                for acc, p in zip(accs, ps):
                    acc[...] += p

            @pl.when(k == nK - 1)
            def _():
                finish([acc[...] for acc in accs])

    if mode == "tn":
        a_spec = pl.BlockSpec((tk, tm), lambda n, m, k: (k, m))
    else:
        a_spec = pl.BlockSpec((tm, tk), lambda n, m, k: (m, k))
    if mode == "nt":
        b_spec = pl.BlockSpec((tn, tk), lambda n, m, k: (n, k))
    else:
        b_spec = pl.BlockSpec((tk, tn), lambda n, m, k: (k, n))
    in_specs, args = [], []
    for a, b in pairs:
        in_specs += [a_spec, b_spec]
        args += [a, b]
    for arr, kind in extras:
        if kind == "mn":
            in_specs.append(pl.BlockSpec((tm, tn), lambda n, m, k: (m, n)))
        else:
            in_specs.append(pl.BlockSpec((1, tn), lambda n, m, k: (0, n)))
        args.append(arr)
    out_specs = [pl.BlockSpec((tm, tn), lambda n, m, k: (m, n)) for _ in out_dtypes]
    res = pl.pallas_call(
        body, name=name, grid=(nN, nM, nK), in_specs=in_specs, out_specs=out_specs,
        out_shape=[jax.ShapeDtypeStruct((M, N), d) for d in out_dtypes],
        scratch_shapes=[pltpu.VMEM((tm, tn), F32) for _ in range(n_acc)] if nK > 1 else [],
        compiler_params=_params("parallel", "parallel", "arbitrary"),
    )(*args)
    return res


def _rms_fwd(name, h, g):
    L, D = h.shape
    tr = _tile(L, 1024, 16)

    def body(h_ref, g_ref, o_ref):
        x = h_ref[...]
        r = lax.rsqrt(jnp.mean(x * x, axis=-1, keepdims=True) + RMS_EPS)
        o_ref[...] = (x * r * g_ref[...]).astype(BF16)

    return pl.pallas_call(
        body, name=name, grid=(L // tr,),
        in_specs=[pl.BlockSpec((tr, D), lambda i: (i, 0)), pl.BlockSpec((1, D), lambda i: (0, 0))],
        out_specs=pl.BlockSpec((tr, D), lambda i: (i, 0)),
        out_shape=jax.ShapeDtypeStruct((L, D), BF16), compiler_params=_params("parallel"),
    )(h, g.reshape(1, D))


def _rms_bwd(name, dxn, h, g, dres):
    L, D = h.shape
    tr = _tile(L, 512, 8)

    def body(dxn_ref, h_ref, g_ref, dres_ref, dh_ref, dg_ref):
        x = h_ref[...]
        r = lax.rsqrt(jnp.mean(x * x, axis=-1, keepdims=True) + RMS_EPS)
        xh = x * r
        dxn = dxn_ref[...]
        dxh = dxn * g_ref[...]
        dh_ref[...] = dres_ref[...] + r * (dxh - xh * jnp.mean(dxh * xh, axis=-1, keepdims=True))

        @pl.when(pl.program_id(0) == 0)
        def _():
            dg_ref[...] = jnp.zeros_like(dg_ref)

        dg_ref[...] += jnp.sum(dxn * xh, axis=0, keepdims=True)

    row = pl.BlockSpec((tr, D), lambda i: (i, 0))
    vec = pl.BlockSpec((1, D), lambda i: (0, 0))
    return pl.pallas_call(
        body, name=name, grid=(L // tr,), in_specs=[row, row, vec, row], out_specs=[row, vec],
        out_shape=[jax.ShapeDtypeStruct((L, D), F32), jax.ShapeDtypeStruct((1, D), F32)],
        compiler_params=_params("arbitrary"),
    )(dxn, h, g.reshape(1, D), dres)


def _head_sums(x2):
    blockdiag = (_iota((LANES, LANES), 0) // HEAD == _iota((LANES, LANES), 1) // HEAD).astype(BF16)
    cols = [_dotx(x2[:, j:j + LANES], blockdiag) for j in range(0, x2.shape[1], LANES)]
    return jnp.concatenate(cols, axis=1) if len(cols) > 1 else cols[0]


def _headrms_fwd(name, raw, g):
    L, D = raw.shape
    tr = _tile(L, 512, 16)

    def body(x_ref, g_ref, o_ref):
        x = x_ref[...]
        r = lax.rsqrt(_head_sums(x * x) * (1.0 / HEAD) + RMS_EPS)
        o_ref[...] = (x * r * g_ref[...]).astype(BF16)

    return pl.pallas_call(
        body, name=name, grid=(L // tr,),
        in_specs=[pl.BlockSpec((tr, D), lambda i: (i, 0)), pl.BlockSpec((1, D), lambda i: (0, 0))],
        out_specs=pl.BlockSpec((tr, D), lambda i: (i, 0)),
        out_shape=jax.ShapeDtypeStruct((L, D), BF16), compiler_params=_params("parallel"),
    )(raw, g)


def _headrms_bwd(name, dy, raw, g):
    L, D = raw.shape
    tr = _tile(L, 512, 16)

    def body(dy_ref, x_ref, g_ref, dx_ref, dg_ref):
        x = x_ref[...]
        dy = dy_ref[...]
        r = lax.rsqrt(_head_sums(x * x) * (1.0 / HEAD) + RMS_EPS)
        xh = x * r
        dxh = dy * g_ref[...]
        dx_ref[...] = (r * (dxh - xh * (_head_sums(dxh * xh) * (1.0 / HEAD)))).astype(BF16)

        @pl.when(pl.program_id(0) == 0)
        def _():
            dg_ref[...] = jnp.zeros_like(dg_ref)

        dg_ref[...] += jnp.sum(dy * xh, axis=0, keepdims=True)

    row = pl.BlockSpec((tr, D), lambda i: (i, 0))
    vec = pl.BlockSpec((1, D), lambda i: (0, 0))
    return pl.pallas_call(
        body, name=name, grid=(L // tr,), in_specs=[row, row, vec], out_specs=[row, vec],
        out_shape=[jax.ShapeDtypeStruct((L, D), BF16), jax.ShapeDtypeStruct((1, D), F32)],
        compiler_params=_params("arbitrary"),
    )(dy, raw, g)


def _loss(name, h, tgt, pad_rows):
    L, D = h.shape
    nb = L // BLK
    assert pad_rows == BLK

    def body(h_ref, t_ref, dh_ref, s_ref):
        i = pl.program_id(0)

        @pl.when(i == 0)
        def _():
            s_ref[...] = jnp.zeros_like(s_ref)
            dh_ref[...] = jnp.zeros_like(dh_ref)

        @pl.when(i > 0)
        def _():
            e = h_ref[...] - t_ref[...]
            dh_ref[...] = e * (1.0 / D)
            s_ref[...] += jnp.sum(e * e, axis=0, keepdims=True)

    return pl.pallas_call(
        body, name=name, grid=(nb,),
        in_specs=[pl.BlockSpec((BLK, D), lambda i: (i, 0)), pl.BlockSpec((BLK, D), lambda i: (jnp.maximum(i - 1, 0), 0))],
        out_specs=[pl.BlockSpec((BLK, D), lambda i: (i, 0)), pl.BlockSpec((1, D), lambda i: (0, 0))],
        out_shape=[jax.ShapeDtypeStruct((L, D), F32), jax.ShapeDtypeStruct((1, D), F32)],
        compiler_params=_params("arbitrary"),
    )(h, tgt)


def _swiglu_up(name, xn, w1, w3):
    L, D = xn.shape
    Fd = w1.shape[1]
    tm, tn = _tile(L, 704, 16), _tile(Fd, 1408, LANES)

    def epi(accs, _):
        a, b = accs
        return [a, b, a * _sigmoid(a) * b]

    return _matmul(name, [(xn, w1), (xn, w3)], "nn", tm, tn, D, [BF16, BF16, BF16], epilogue=epi, separate=True)


def _swiglu_bwd(name, dh, w2, a, b):
    L, D = dh.shape
    Fd = w2.shape[0]
    tm, tn = _tile(L, 704, 16), _tile(Fd, 1408, LANES)

    def epi(accs, ex):
        dact = accs[0] * FFN_RES
        av, bv = ex[0].astype(F32), ex[1].astype(F32)
        s = _sigmoid(av)
        return [dact * bv * (s * (1.0 + av * (1.0 - s))), dact * av * s]

    return _matmul(name, [(dh, w2)], "nt", tm, tn, D, [BF16, BF16], epilogue=epi, extras=[(a, "mn"), (b, "mn")])


def _ffn_fwd(tag, h, g, w1, w3, w2):
    L, D = h.shape
    xn = _rms_fwd(f"{tag}_norm", h, g)
    a, b, act = _swiglu_up(f"{tag}_up", xn, w1, w3)
    tm = _tile(L, 704, 8)
    (h_out,) = _matmul(f"{tag}_down", [(act, w2)], "nn", tm, D, w2.shape[0], [F32],
                       epilogue=lambda accs, ex: [ex[0] + FFN_RES * accs[0]], extras=[(h, "mn")])
    return h_out, (xn, a, b, act)


def _ffn_bwd(tag, dh_out, h, g, w1, w3, w2, saved):
    xn, a, b, act = saved
    L, D = h.shape
    Fd = w2.shape[0]
    tl = _tile(L, 704, 16)
    da, db = _swiglu_bwd(f"{tag}_dact", dh_out, w2, a, b)
    (dw2,) = _matmul(f"{tag}_dw2", [(act, dh_out)], "tn", _tile(Fd, 1408, LANES), D, tl, [F32],
                     epilogue=lambda accs, ex: [FFN_RES * accs[0]])
    (dw1,) = _matmul(f"{tag}_dw1", [(xn, da)], "tn", D, _tile(Fd, 1408, LANES), tl, [F32])
    (dw3,) = _matmul(f"{tag}_dw3", [(xn, db)], "tn", D, _tile(Fd, 1408, LANES), tl, [F32])
    (dxn,) = _matmul(f"{tag}_dxn", [(da, w1), (db, w3)], "nt", _tile(L, 704, 8), _tile(D, 512, LANES), Fd, [F32])
    dh, dg = _rms_bwd(f"{tag}_dnorm", dxn, h, g, dh_out)
    return dh, dg.reshape(-1), dw1, dw3, dw2


def _conv_taps(ext, k):
    return ext if k == 0 else pltpu.roll(ext, k, axis=0)


def _conv_fwd(name, zx, col0, w, b, pad):
    L = zx.shape[0]
    C = w.shape[1]
    tr, tc = _tile(L, 704, 8), _tile(C, 512, LANES)
    cb = col0 // tc
    assert col0 % tc == 0

    def body(u_ref, halo_ref, w_ref, b_ref, o_ref):
        ext = jnp.concatenate([halo_ref[...], u_ref[...]], axis=0)
        pre = b_ref[...] + sum(_conv_taps(ext, 3 - k)[8:] * w_ref[k:k + 1, :] for k in range(4))
        rows = _iota(pre.shape, 0) + pl.program_id(1) * tr
        o_ref[...] = jnp.where(rows >= pad, pre * _sigmoid(pre), 0.0)

    return pl.pallas_call(
        body, name=name, grid=(C // tc, L // tr),
        in_specs=[pl.BlockSpec((tr, tc), lambda j, i: (i, cb + j)),
                  pl.BlockSpec((8, tc), lambda j, i: (jnp.maximum(i * (tr // 8) - 1, 0), cb + j)),
                  pl.BlockSpec((4, tc), lambda j, i: (0, j)), pl.BlockSpec((1, tc), lambda j, i: (0, j))],
        out_specs=pl.BlockSpec((tr, tc), lambda j, i: (i, j)),
        out_shape=jax.ShapeDtypeStruct((L, C), F32), compiler_params=_params("parallel", "parallel"),
    )(zx, zx, w, b)


def _conv_bwd_pre(name, dact, zx, col0, w, b, pad):
    L = zx.shape[0]
    C = w.shape[1]
    tr, tc = _tile(L, 704, 8), _tile(C, 512, LANES)
    cb = col0 // tc

    def body(d_ref, u_ref, halo_ref, w_ref, b_ref, dp_ref, dw_ref, db_ref):
        ext = jnp.concatenate([halo_ref[...], u_ref[...]], axis=0)
        taps = [_conv_taps(ext, 3 - k)[8:] for k in range(4)]
        pre = b_ref[...] + sum(taps[k] * w_ref[k:k + 1, :] for k in range(4))
        s = _sigmoid(pre)
        rows = _iota(pre.shape, 0) + pl.program_id(1) * tr
        dpre = jnp.where(rows >= pad, d_ref[...] * (s * (1.0 + pre * (1.0 - s))), 0.0)
        dp_ref[...] = dpre

        @pl.when(pl.program_id(1) == 0)
        def _():
            dw_ref[...] = jnp.zeros_like(dw_ref)
            db_ref[...] = jnp.zeros_like(db_ref)

        db_ref[...] += jnp.sum(dpre, axis=0, keepdims=True)
        dw_ref[...] += jnp.concatenate([jnp.sum(dpre * taps[k], axis=0, keepdims=True) for k in range(4)], axis=0)

    return pl.pallas_call(
        body, name=name, grid=(C // tc, L // tr),
        in_specs=[pl.BlockSpec((tr, tc), lambda j, i: (i, j)),
                  pl.BlockSpec((tr, tc), lambda j, i: (i, cb + j)),
                  pl.BlockSpec((8, tc), lambda j, i: (jnp.maximum(i * (tr // 8) - 1, 0), cb + j)),
                  pl.BlockSpec((4, tc), lambda j, i: (0, j)), pl.BlockSpec((1, tc), lambda j, i: (0, j))],
        out_specs=[pl.BlockSpec((tr, tc), lambda j, i: (i, j)), pl.BlockSpec((4, tc), lambda j, i: (0, j)),
                   pl.BlockSpec((1, tc), lambda j, i: (0, j))],
        out_shape=[jax.ShapeDtypeStruct((L, C), F32), jax.ShapeDtypeStruct((4, C), F32), jax.ShapeDtypeStruct((1, C), F32)],
        compiler_params=_params("parallel", "arbitrary"),
    )(dact, zx, zx, w, b)


def _conv_bwd_in(name, dpre, w):
    L, C = dpre.shape
    tr, tc = _tile(L, 704, 16), _tile(C, 512, LANES)
    nr = L // tr

    def body(d_ref, halo_ref, w_ref, o_ref):
        halo = jnp.where(pl.program_id(1) == nr - 1, 0.0, halo_ref[...])
        ext = jnp.concatenate([d_ref[...], halo], axis=0)
        acc = ext[:tr] * w_ref[3:4, :]
        for k in range(3):
            acc = acc + pltpu.roll(ext, tr + 8 - (3 - k), axis=0)[:tr] * w_ref[k:k + 1, :]
        o_ref[...] = acc.astype(BF16)

    return pl.pallas_call(
        body, name=name, grid=(C // tc, nr),
        in_specs=[pl.BlockSpec((tr, tc), lambda j, i: (i, j)),
                  pl.BlockSpec((8, tc), lambda j, i: (jnp.minimum((i + 1) * (tr // 8), L // 8 - 1), j)),
                  pl.BlockSpec((4, tc), lambda j, i: (0, j))],
        out_specs=pl.BlockSpec((tr, tc), lambda j, i: (i, j)),
        out_shape=jax.ShapeDtypeStruct((L, C), BF16), compiler_params=_params("parallel", "parallel"),
    )(dpre, dpre, w)


def _dt_fwd(name, zx, col0, bias_row, nheads, pad):
    L = zx.shape[0]
    cb = col0 // LANES

    def body(x_ref, b_ref, dt_ref, dtt_ref):
        v = x_ref[...] + b_ref[...]
        sp = jnp.maximum(v, 0.0) + jnp.log(1.0 + jnp.exp(-jnp.abs(v)))
        rows = _iota(v.shape, 0) + pl.program_id(0) * BLK
        dt = jnp.where((rows >= pad) & (_iota(v.shape, 1) < nheads), sp, 0.0)
        dt_ref[...] = dt
        dtt_ref[...] = dt.T

    return pl.pallas_call(
        body, name=name, grid=(L // BLK,),
        in_specs=[pl.BlockSpec((BLK, LANES), lambda i: (i, cb)), pl.BlockSpec((1, LANES), lambda i: (0, 0))],
        out_specs=[pl.BlockSpec((BLK, LANES), lambda i: (i, 0)), pl.BlockSpec((LANES, BLK), lambda i: (0, i))],
        out_shape=[jax.ShapeDtypeStruct((L, LANES), F32), jax.ShapeDtypeStruct((LANES, L), F32)],
        compiler_params=_params("parallel"),
    )(zx, bias_row)


def _dt_bwd(name, ddt, zx, col0, bias_row, nheads, pad):
    L = zx.shape[0]
    cb = col0 // LANES

    def body(d_ref, x_ref, b_ref, o_ref, db_ref):
        v = x_ref[...] + b_ref[...]
        rows = _iota(v.shape, 0) + pl.program_id(0) * BLK
        g = jnp.where((rows >= pad) & (_iota(v.shape, 1) < nheads), d_ref[...] * _sigmoid(v), 0.0)
        o_ref[...] = g.astype(BF16)

        @pl.when(pl.program_id(0) == 0)
        def _():
            db_ref[...] = jnp.zeros_like(db_ref)

        db_ref[...] += jnp.sum(g, axis=0, keepdims=True)

    return pl.pallas_call(
        body, name=name, grid=(L // BLK,),
        in_specs=[pl.BlockSpec((BLK, LANES), lambda i: (i, 0)), pl.BlockSpec((BLK, LANES), lambda i: (i, cb)),
                  pl.BlockSpec((1, LANES), lambda i: (0, 0))],
        out_specs=[pl.BlockSpec((BLK, LANES), lambda i: (i, 0)), pl.BlockSpec((1, LANES), lambda i: (0, 0))],
        out_shape=[jax.ShapeDtypeStruct((L, LANES), BF16), jax.ShapeDtypeStruct((1, LANES), F32)],
        compiler_params=_params("arbitrary"),
    )(ddt, zx, bias_row)


def _ssd_common(dt, dtt, a_row, a_col):
    tril = (_iota((BLK, BLK), 0) >= _iota((BLK, BLK), 1)).astype(BF16)
    cum = _dotx(tril, dt * a_row, split="b")
    cumt = _dotx(dtt * a_col, tril, NT)
    return cum, cumt


def _ssd_fwd(name, xbc, dt, dtt, a_row, a_col, expand, di, ng):
    L = xbc.shape[0]
    nc = L // BLK
    hpg = di // HEAD // ng
    gw = hpg * HEAD
    assert gw % LANES == 0

    def body(x_ref, dt_ref, dtt_ref, ar_ref, ac_ref, ex_ref, y_ref, st_ref, h_ref):
        @pl.when(pl.program_id(0) == 0)
        def _():
            h_ref[...] = jnp.zeros_like(h_ref)

        st_ref[0] = h_ref[...]
        dt, dtt = dt_ref[...], dtt_ref[...]
        cum, cumt = _ssd_common(dt, dtt, ar_ref[...], ac_ref[...])
        ex = ex_ref[...]
        ecum_x = _dotx(jnp.exp(cum), ex)
        wend_x = _dotx(jnp.exp(cum[BLK - 1:BLK, :] - cum) * dt, ex)
        ecl = jnp.broadcast_to(jnp.exp(cumt[:, BLK - 1:BLK]), (LANES, LANES))
        decay_h = _dotx(ex, ecl, TN, split="b")
        causal = _iota((BLK, BLK), 0) >= _iota((BLK, BLK), 1)
        low = _iota((BLK, LANES), 1) < HEAD
        for g in range(ng):
            xg = x_ref[:, g * gw:(g + 1) * gw]
            bg = x_ref[:, di + g * D_STATE:di + (g + 1) * D_STATE].astype(BF16)
            cg = x_ref[:, di + (ng + g) * D_STATE:di + (ng + g + 1) * D_STATE].astype(BF16)
            hg = h_ref[g * gw:(g + 1) * gw, :]
            gram = _dot(cg, bg, NT)
            yoff = _dot(cg, hg.astype(BF16), NT) * ecum_x[:, g * gw:(g + 1) * gw]
            parts = []
            for j in range(gw // LANES):
                xp = xg[:, j * LANES:(j + 1) * LANES].astype(BF16)
                yd = []
                for hh in range(2):
                    h = g * hpg + 2 * j + hh
                    seg = cum[:, h:h + 1] - cumt[h:h + 1, :]
                    m = gram * jnp.where(causal, jnp.exp(jnp.minimum(seg, 0.0)), 0.0) * dtt[h:h + 1, :]
                    yd.append(_dot(m.astype(BF16), xp))
                parts.append(jnp.where(low, yd[0], yd[1]))
            y_ref[:, g * gw:(g + 1) * gw] = jnp.concatenate(parts, axis=1) + yoff
            xw = (xg * wend_x[:, g * gw:(g + 1) * gw]).astype(BF16)
            h_ref[g * gw:(g + 1) * gw, :] = hg * decay_h[g * gw:(g + 1) * gw, :] + _dot(xw, bg, TN)

    W = xbc.shape[1]
    full = lambda r, c: pl.BlockSpec((r, c), lambda i: (0, 0))
    return pl.pallas_call(
        body, name=name, grid=(nc,),
        in_specs=[pl.BlockSpec((BLK, W), lambda i: (i, 0)), pl.BlockSpec((BLK, LANES), lambda i: (i, 0)),
                  pl.BlockSpec((LANES, BLK), lambda i: (0, i)), full(1, LANES), full(LANES, LANES), full(LANES, di)],
        out_specs=[pl.BlockSpec((BLK, di), lambda i: (i, 0)), pl.BlockSpec((1, di, D_STATE), lambda i: (i, 0, 0))],
        out_shape=[jax.ShapeDtypeStruct((L, di), F32), jax.ShapeDtypeStruct((nc, di, D_STATE), F32)],
        scratch_shapes=[pltpu.VMEM((di, D_STATE), F32)], compiler_params=_params("arbitrary"),
    )(xbc, dt, dtt, a_row, a_col, expand)


def _ssd_bwd(name, xbc, dt, dtt, a_row, a_col, expand, states, dy, d_x, di, ng):
    L, W = xbc.shape
    nc = L // BLK
    hpg = di // HEAD // ng
    gw = hpg * HEAD

    def body(x_ref, dt_ref, dtt_ref, ar_ref, ac_ref, ex_ref, st_ref, dy_ref, dx_ref_in, dxo_ref, ddt_ref, da_ref, dh_ref):
        @pl.when(pl.program_id(0) == 0)
        def _():
            dh_ref[...] = jnp.zeros_like(dh_ref)
            da_ref[...] = jnp.zeros_like(da_ref)

        dt, dtt, a_row = dt_ref[...], dtt_ref[...], ar_ref[...]
        cum, cumt = _ssd_common(dt, dtt, a_row, ac_ref[...])
        ex = ex_ref[...]
        ecum = jnp.exp(cum)
        ecum_x = _dotx(ecum, ex)
        e_s = jnp.exp(cum[BLK - 1:BLK, :] - cum)
        wend_x = _dotx(e_s * dt, ex)
        ecl_col = jnp.exp(cumt[:, BLK - 1:BLK])
        decay_h = _dotx(ex, jnp.broadcast_to(ecl_col, (LANES, LANES)), TN, split="b")
        causal = _iota((BLK, BLK), 0) >= _iota((BLK, BLK), 1)
        low = _iota((BLK, LANES), 1) < HEAD
        lane = _iota((1, LANES), 1)
        sub = _iota((LANES, 1), 0)
        dcum_c = jnp.zeros((BLK, LANES), F32)
        dcum_r = jnp.zeros((LANES, BLK), F32)
        ddt_r = jnp.zeros((LANES, BLK), F32)
        zoff = []
        dwend_src = []
        for g in range(ng):
            gs = slice(g * gw, (g + 1) * gw)
            xg = x_ref[:, gs]
            bg = x_ref[:, di + g * D_STATE:di + (g + 1) * D_STATE].astype(BF16)
            cg = x_ref[:, di + (ng + g) * D_STATE:di + (ng + g + 1) * D_STATE].astype(BF16)
            hprev = st_ref[0, gs, :]
            dhn = dh_ref[gs, :]
            dyg = dy_ref[:, gs]
            gram = _dot(cg, bg, NT)
            dgram = jnp.zeros((BLK, BLK), F32)
            dxg = []
            for j in range(gw // LANES):
                xp = xg[:, j * LANES:(j + 1) * LANES].astype(BF16)
                dyp = dyg[:, j * LANES:(j + 1) * LANES]
                dxh = []
                for hh in range(2):
                    h = g * hpg + 2 * j + hh
                    seg = cum[:, h:h + 1] - cumt[h:h + 1, :]
                    lm = jnp.where(causal, jnp.exp(jnp.minimum(seg, 0.0)), 0.0)
                    dtr = dtt[h:h + 1, :]
                    m = gram * lm * dtr
                    dym = jnp.where(low if hh == 0 else ~low, dyp, 0.0).astype(BF16)
                    dxh.append(_dot(m.astype(BF16), dym, TN))
                    dm = _dot(dym, xp, NT)
                    dgram = dgram + dm * lm * dtr
                    v = dm * gram * lm
                    wv = v * dtr
                    ddt_r = ddt_r + jnp.where(sub == h, jnp.sum(v, axis=0, keepdims=True), 0.0)
                    dcum_r = dcum_r - jnp.where(sub == h, jnp.sum(wv, axis=0, keepdims=True), 0.0)
                    dcum_c = dcum_c + jnp.where(lane == h, jnp.sum(wv, axis=1, keepdims=True), 0.0)
                dxg.append(jnp.where(low, dxh[0], dxh[1]))
            dx_diag = jnp.concatenate(dxg, axis=1)
            hb = hprev.astype(BF16)
            yoff = _dot(cg, hb, NT) * ecum_x[:, gs]
            dye = (dyg * ecum_x[:, gs]).astype(BF16)
            dcg = _dot(dye, hb) + _dot(dgram.astype(BF16), bg)
            dbg = _dot(dgram.astype(BF16), cg, TN)
            dh_prev = _dot(dye, cg, TN)
            zoff.append(dyg * yoff)
            dhb = dhn.astype(BF16)
            dxw = _dot(bg, dhb, NT)
            xw = (xg * wend_x[:, gs]).astype(BF16)
            dbg = dbg + _dot(xw, dhb)
            dwend_src.append(dxw * xg)
            dxo_ref[:, gs] = dx_diag + dxw * wend_x[:, gs] + dyg * dx_ref_in[:, gs]
            dxo_ref[:, di + g * D_STATE:di + (g + 1) * D_STATE] = dbg
            dxo_ref[:, di + (ng + g) * D_STATE:di + (ng + g + 1) * D_STATE] = dcg
            prod = dhn * hprev
            dd = jnp.sum(_dotx(ex[:, gs], prod, split="b"), axis=1, keepdims=True)
            dcum_r = dcum_r + jnp.where(_iota((1, BLK), 1) == BLK - 1, dd * ecl_col, 0.0)
            dh_ref[gs, :] = dhn * decay_h[gs, :] + dh_prev
        dcum_c = dcum_c + _dotx(jnp.concatenate(zoff, axis=1), ex, NT)
        dwend = _dotx(jnp.concatenate(dwend_src, axis=1), ex, NT)
        ddt_c = dwend * e_s
        de = dwend * dt * e_s
        dcum_c = dcum_c - de + jnp.where(_iota((BLK, 1), 0) == BLK - 1, jnp.sum(de, axis=0, keepdims=True), 0.0)
        dcum = dcum_c + dcum_r.T
        triu = (_iota((BLK, BLK), 0) <= _iota((BLK, BLK), 1)).astype(BF16)
        da = _dotx(triu, dcum, split="b")
        ddt_ref[...] = ddt_c + ddt_r.T + da * a_row
        da_ref[...] += jnp.sum(da * dt, axis=0, keepdims=True)

    rev = lambda i: nc - 1 - i
    full = lambda r, c: pl.BlockSpec((r, c), lambda i: (0, 0))
    return pl.pallas_call(
        body, name=name, grid=(nc,),
        in_specs=[pl.BlockSpec((BLK, W), lambda i: (rev(i), 0)), pl.BlockSpec((BLK, LANES), lambda i: (rev(i), 0)),
                  pl.BlockSpec((LANES, BLK), lambda i: (0, rev(i))), full(1, LANES), full(LANES, LANES), full(LANES, di),
                  pl.BlockSpec((1, di, D_STATE), lambda i: (rev(i), 0, 0)), pl.BlockSpec((BLK, di), lambda i: (rev(i), 0)),
                  full(1, di)],
        out_specs=[pl.BlockSpec((BLK, W), lambda i: (rev(i), 0)), pl.BlockSpec((BLK, LANES), lambda i: (rev(i), 0)),
                   full(1, LANES)],
        out_shape=[jax.ShapeDtypeStruct((L, W), F32), jax.ShapeDtypeStruct((L, LANES), F32),
                   jax.ShapeDtypeStruct((1, LANES), F32)],
        scratch_shapes=[pltpu.VMEM((di, D_STATE), F32)], compiler_params=_params("arbitrary"),
    )(xbc, dt, dtt, a_row, a_col, expand, states, dy, d_x)


def _group_sums(v, gsz):
    cols = []
    for j in range(0, v.shape[1], gsz):
        s = jnp.sum(v[:, j:j + gsz], axis=1, keepdims=True)
        cols.append(jnp.broadcast_to(s, (v.shape[0], gsz)))
    return jnp.concatenate(cols, axis=1)


def _gate_fwd(name, y, xbc, zx, d_x, ng_row, gsz):
    L, di = y.shape
    tr = _tile(L, 512, 16)

    def body(y_ref, x_ref, z_ref, d_ref, g_ref, o_ref):
        z = z_ref[...]
        y2 = (y_ref[...] + d_ref[...] * x_ref[...]) * (z * _sigmoid(z))
        r = lax.rsqrt(_group_sums(y2 * y2, gsz) * (1.0 / gsz) + RMS_EPS)
        o_ref[...] = (y2 * r * g_ref[...]).astype(BF16)

    row = pl.BlockSpec((tr, di), lambda i: (i, 0))
    vec = pl.BlockSpec((1, di), lambda i: (0, 0))
    return pl.pallas_call(
        body, name=name, grid=(L // tr,), in_specs=[row, row, row, vec, vec], out_specs=row,
        out_shape=jax.ShapeDtypeStruct((L, di), BF16), compiler_params=_params("parallel"),
    )(y, xbc, zx, d_x, ng_row)


def _gate_bwd(name, dy3, y, xbc, zx, d_x, ng_row, gsz):
    L, di = y.shape
    tr = _tile(L, 256, 16)

    def body(dy_ref, y_ref, x_ref, z_ref, d_ref, g_ref, dz_ref, dy1_ref, dg_ref, dd_ref):
        z, x = z_ref[...], x_ref[...]
        s = _sigmoid(z)
        sz = z * s
        y1 = y_ref[...] + d_ref[...] * x
        y2 = y1 * sz
        r = lax.rsqrt(_group_sums(y2 * y2, gsz) * (1.0 / gsz) + RMS_EPS)
        yg = y2 * r
        dy3 = dy_ref[...]
        dyg = dy3 * g_ref[...]
        dy2 = r * (dyg - yg * (_group_sums(dyg * yg, gsz) * (1.0 / gsz)))
        dz_ref[...] = (dy2 * y1 * (s * (1.0 + z * (1.0 - s)))).astype(BF16)
        dy1 = dy2 * sz
        dy1_ref[...] = dy1

        @pl.when(pl.program_id(0) == 0)
        def _():
            dg_ref[...] = jnp.zeros_like(dg_ref)
            dd_ref[...] = jnp.zeros_like(dd_ref)

        dg_ref[...] += jnp.sum(dy3 * yg, axis=0, keepdims=True)
        dd_ref[...] += jnp.sum(dy1 * x, axis=0, keepdims=True)

    row = pl.BlockSpec((tr, di), lambda i: (i, 0))
    vec = pl.BlockSpec((1, di), lambda i: (0, 0))
    return pl.pallas_call(
        body, name=name, grid=(L // tr,), in_specs=[row, row, row, row, vec, vec], out_specs=[row, row, vec, vec],
        out_shape=[jax.ShapeDtypeStruct((L, di), BF16), jax.ShapeDtypeStruct((L, di), F32),
                   jax.ShapeDtypeStruct((1, di), F32), jax.ShapeDtypeStruct((1, di), F32)],
        compiler_params=_params("arbitrary"),
    )(dy3, y, xbc, zx, d_x, ng_row)


SB_EDGE = 4
SB_INNER = 8


def _sb_heads(x):
    low = _iota(x.shape, 1) < HEAD
    zero = jnp.zeros_like(x)
    return jnp.concatenate([jnp.where(low, x, zero), jnp.where(low, zero, x)], axis=0)


def _sb_unheads(x2):
    return jnp.where(_iota((BLK, LANES), 1) < HEAD, x2[:BLK], x2[BLK:])


def _sb_tiles(qq, kblks, dmat, scol, thrs, s0s):
    n = range(len(kblks))
    z = [_dot(qq, kblks[u], NT) for u in n]
    e = [jnp.exp(-jnp.abs(z[u])) for u in n]
    l1 = [jnp.log(1.0 + e[u]) for u in n]
    lsz = [jnp.minimum(z[u], 0.0) - l1[u] for u in n]
    if thrs is None:
        return z, e, lsz, [-jnp.maximum(z[u], 0.0) - l1[u] for u in n], None
    valid = [(dmat > thrs[u]) & (scol >= s0s[u]) for u in n]
    lkm = [jnp.where(valid[u], -jnp.maximum(z[u], 0.0) - l1[u], 0.0) for u in n]
    return z, e, lsz, lkm, valid


def _sb_sweep(group, i, init):
    ue, ui = SB_EDGE, SB_INNER
    carry = group(True, ue, 0, init)
    n_inner = jnp.maximum((i - ue) // ui, 0)
    carry = lax.fori_loop(0, n_inner, lambda g, cr: group(False, ui, ue + g * ui, cr), carry)
    off = ue + n_inner * ui
    n_tail = jnp.maximum((i - off + ue) // ue, 0)
    return lax.fori_loop(0, n_tail, lambda g, cr: group(True, ue, off + g * ue, cr), carry)


def _sb_fwd(name, q, k, v, pad):
    L, D = q.shape
    nb = L // BLK

    def body(q_ref, k_ref, v_ref, o_ref, o32_ref):
        i = pl.program_id(1)
        after = (_iota((BLK, BLK), 0) > _iota((BLK, BLK), 1)).astype(BF16)
        qq = _sb_heads(q_ref[...] * 0.125)
        dmat = (_iota((2 * BLK, BLK), 0) & (BLK - 1)) - _iota((2 * BLK, BLK), 1)
        scol = _iota((2 * BLK, BLK), 1)

        def group(masked, U, off, carry):
            c, acc = carry
            rng = range(U)
            kraw = [i - (off + u) for u in rng]
            kb = [jnp.maximum(kraw[u], 0) for u in rng]
            rows = [pl.ds(pl.multiple_of(kb[u] * BLK, BLK), BLK) for u in rng]
            thr = [jnp.where(kraw[u] >= 0, (kb[u] - i) * BLK, BLK) for u in rng] if masked else None
            s0 = [pad - kb[u] * BLK for u in rng] if masked else None
            _, _, lsz, lkm, valid = _sb_tiles(qq, [k_ref[rows[u], :] for u in rng], dmat, scol, thr, s0)
            cum = [_dotx(lkm[u], after, parts=2) for u in rng]
            a = []
            for u in rng:
                w = jnp.exp(lsz[u] + (c + cum[u]))
                a.append((jnp.where(valid[u], w, 0.0) if masked else w).astype(BF16))
                c = c + jnp.sum(lkm[u], axis=1, keepdims=True)
            for u in rng:
                acc = acc + _dot(a[u], v_ref[rows[u], :])
            return c, acc

        _, acc = _sb_sweep(group, i, (jnp.zeros((2 * BLK, 1), F32), jnp.zeros((2 * BLK, LANES), F32)))
        out = _sb_unheads(acc)
        o_ref[...] = out.astype(BF16)
        o32_ref[...] = out

    blk = pl.BlockSpec((BLK, LANES), lambda p, i: (i, p))
    col = pl.BlockSpec((L, LANES), lambda p, i: (0, p))
    return pl.pallas_call(
        body, name=name, grid=(D // LANES, nb), in_specs=[blk, col, col], out_specs=[blk, blk],
        out_shape=[jax.ShapeDtypeStruct((L, D), BF16), jax.ShapeDtypeStruct((L, D), F32)],
        compiler_params=_params("parallel", "arbitrary"),
    )(q, k, v)


def _sb_bwd(name, q, k, v, do, o32, pad):
    L, D = q.shape
    nb = L // BLK

    def body(q_ref, k_ref, v_ref, do_ref, o_ref, dq_ref, dk_ref, dv_ref):
        i = pl.program_id(1)

        @pl.when(i == 0)
        def _():
            dk_ref[...] = jnp.zeros_like(dk_ref)
            dv_ref[...] = jnp.zeros_like(dv_ref)

        after = (_iota((BLK, BLK), 0) > _iota((BLK, BLK), 1)).astype(BF16)
        from_j = (_iota((BLK, BLK), 0) >= _iota((BLK, BLK), 1)).astype(BF16)
        qq = _sb_heads(q_ref[...] * 0.125)
        dd = _sb_heads(do_ref[...])
        dmat = (_iota((2 * BLK, BLK), 0) & (BLK - 1)) - _iota((2 * BLK, BLK), 1)
        scol = _iota((2 * BLK, BLK), 1)
        o2 = jnp.concatenate([o_ref[...], o_ref[...]], axis=0)
        total = jnp.sum(dd.astype(F32) * o2, axis=1, keepdims=True)

        def group(masked, U, off, carry):
            c, met, acc = carry
            rng = range(U)
            kraw = [i - (off + u) for u in rng]
            kb = [jnp.maximum(kraw[u], 0) for u in rng]
            rows = [pl.ds(pl.multiple_of(kb[u] * BLK, BLK), BLK) for u in rng]
            thr = [jnp.where(kraw[u] >= 0, (kb[u] - i) * BLK, BLK) for u in rng] if masked else None
            s0 = [pad - kb[u] * BLK for u in rng] if masked else None
            kblk = [k_ref[rows[u], :] for u in rng]
            z, e, lsz, lkm, valid = _sb_tiles(qq, kblk, dmat, scol, thr, s0)
            da = [_dot(dd, v_ref[rows[u], :], NT) for u in rng]
            cum = [_dotx(lkm[u], after, parts=2) for u in rng]
            a = []
            for u in rng:
                w = jnp.exp(lsz[u] + (c + cum[u]))
                a.append((jnp.where(valid[u], w, 0.0) if masked else w).astype(BF16))
                c = c + jnp.sum(lkm[u], axis=1, keepdims=True)
            dlog = [da[u] * a[u].astype(F32) for u in rng]
            here = [_dotx(dlog[u], from_j, parts=2) for u in rng]
            dz = []
            for u in rng:
                inv = 1.0 / (1.0 + e[u])
                sig = jnp.where(z[u] >= 0.0, inv, e[u] * inv)
                pull = (total - (met + here[u])) * sig
                dz.append((dlog[u] * (1.0 - sig) - (jnp.where(valid[u], pull, 0.0) if masked else pull)).astype(BF16))
                met = met + jnp.sum(dlog[u], axis=1, keepdims=True)
            for u in rng:
                dk_ref[rows[u], :] += _dot(dz[u], qq, TN)
                dv_ref[rows[u], :] += _dot(a[u], dd, TN)
                acc = acc + _dot(dz[u], kblk[u])
            return c, met, acc

        col1 = jnp.zeros((2 * BLK, 1), F32)
        _, _, acc = _sb_sweep(group, i, (col1, col1, jnp.zeros((2 * BLK, LANES), F32)))
        dq_ref[...] = _sb_unheads(acc) * 0.125

    blk = pl.BlockSpec((BLK, LANES), lambda p, i: (i, p))
    col = pl.BlockSpec((L, LANES), lambda p, i: (0, p))
    return pl.pallas_call(
        body, name=name, grid=(D // LANES, nb), in_specs=[blk, col, col, blk, blk], out_specs=[blk, col, col],
        out_shape=[jax.ShapeDtypeStruct((L, D), F32)] * 3, compiler_params=_params("parallel", "arbitrary"),
    )(q, k, v, do, o32)


def _local_step(x, tgt, w):
    S, D = x.shape
    nm = w["meta_tokens"].shape[0]
    pad = BLK - nm
    L = pad + nm + S
    assert L % BLK == 0 and 0 < nm <= BLK
    di = w["ssm_out_proj"].shape[0]
    nh = w["ssm_dt_bias"].shape[0]
    assert di == nh * HEAD and nh <= LANES
    conv_dim = w["ssm_conv_w"].shape[1]
    ng = (conv_dim - di) // (2 * D_STATE)
    zp = di + conv_dim + LANES
    g = w["norm_g"]
    grads = {}

    h0 = jnp.concatenate([jnp.zeros((pad, D), F32), w["meta_tokens"], x], axis=0)

    h1, s1 = _ffn_fwd("f00", h0, g[0, 0], w["ffn_w1"][0, 0], w["ffn_w3"][0, 0], w["ffn_w2"][0, 0])
    u0 = _rms_fwd("m_norm", h1, g[0, 1])
    w_in = jnp.concatenate([w["ssm_in_proj"], jnp.zeros((D, zp - w["ssm_in_proj"].shape[1]), BF16)], axis=1)
    tl = _tile(L, 704, 16)
    (zx,) = _matmul("m_inproj", [(u0, w_in)], "nn", tl, _tile(zp, 1024, LANES), D, [F32])
    conv_b = w["ssm_conv_b"].reshape(1, conv_dim)
    xbc = _conv_fwd("m_conv", zx, di, w["ssm_conv_w"], conv_b, pad)
    bias_row = jnp.zeros((1, LANES), F32).at[0, :nh].set(w["ssm_dt_bias"])
    dt, dtt = _dt_fwd("m_dt", zx, di + conv_dim, bias_row, nh, pad)
    a_neg = -jnp.exp(w["ssm_a_log"])
    a_row = jnp.zeros((1, LANES), F32).at[0, :nh].set(a_neg)
    a_col = jnp.broadcast_to(a_row.reshape(LANES, 1), (LANES, LANES))
    expand = (jnp.arange(LANES)[:, None] == (jnp.arange(di) // HEAD)[None, :]).astype(BF16)
    y_ssd, states = _ssd_fwd("m_ssd", xbc, dt, dtt, a_row, a_col, expand, di, ng)
    d_x = jnp.repeat(w["ssm_d"], HEAD).reshape(1, di)
    ssm_g = w["ssm_norm_g"].reshape(1, di)
    gsz = di // ng
    y3 = _gate_fwd("m_gate", y_ssd, xbc, zx, d_x, ssm_g, gsz)
    (h2,) = _matmul("m_outproj", [(y3, w["ssm_out_proj"])], "nn", tl, D, di, [F32],
                    epilogue=lambda accs, ex: [ex[0] + accs[0]], extras=[(h1, "mn")])
    h3, s2 = _ffn_fwd("f01", h2, g[0, 2], w["ffn_w1"][0, 1], w["ffn_w3"][0, 1], w["ffn_w2"][0, 1])

    kv_in = _rms_fwd("kv_norm", h3, w["kv_norm_g"])
    (k_raw,) = _matmul("kv_k", [(kv_in, w["w_k"])], "nn", tl, D, D, [F32])
    (v_sh,) = _matmul("kv_v", [(kv_in, w["w_v"])], "nn", tl, D, D, [BF16])
    kg = jnp.tile(w["k_norm_g"], D // HEAD).reshape(1, D)
    k_sh = _headrms_fwd("kv_knorm", k_raw, kg)

    h4, s3 = _ffn_fwd("f10", h3, g[1, 0], w["ffn_w1"][1, 0], w["ffn_w3"][1, 0], w["ffn_w2"][1, 0])
    u1 = _rms_fwd("a_norm", h4, g[1, 1])
    (q_raw,) = _matmul("a_q", [(u1, w["sb_w_q"])], "nn", tl, D, D, [F32])
    qg = jnp.tile(w["sb_q_norm_g"], D // HEAD).reshape(1, D)
    q = _headrms_fwd("a_qnorm", q_raw, qg)
    o, o32 = _sb_fwd("a_attn", q, k_sh, v_sh, pad)
    (h5,) = _matmul("a_o", [(o, w["sb_w_o"])], "nn", tl, D, D, [F32],
                    epilogue=lambda accs, ex: [ex[0] + accs[0]], extras=[(h4, "mn")])
    h6, s4 = _ffn_fwd("f11", h5, g[1, 2], w["ffn_w1"][1, 1], w["ffn_w3"][1, 1], w["ffn_w2"][1, 1])

    dh6, sq = _loss("loss", h6, tgt, pad + nm)

    dg = jnp.zeros_like(g)
    dw1 = [[None, None], [None, None]]
    dw3 = [[None, None], [None, None]]
    dw2 = [[None, None], [None, None]]
    dh5, dgv, dw1[1][1], dw3[1][1], dw2[1][1] = _ffn_bwd("b11", dh6, h5, g[1, 2], w["ffn_w1"][1, 1], w["ffn_w3"][1, 1],
                                                           w["ffn_w2"][1, 1], s4)
    dg = dg.at[1, 2].set(dgv)
    td = _tile(D, 512, LANES)
    (do,) = _matmul("b_do", [(dh5, w["sb_w_o"])], "nt", tl, D, D, [BF16])
    (grads["sb_w_o"],) = _matmul("b_dwo", [(o, dh5)], "tn", D, td, tl, [F32])
    dq, dk, dv = _sb_bwd("b_attn", q, k_sh, v_sh, do, o32, pad)
    dq_raw, dqg = _headrms_bwd("b_qnorm", dq, q_raw, qg)
    grads["sb_q_norm_g"] = dqg.reshape(D // HEAD, HEAD).sum(0)
    (grads["sb_w_q"],) = _matmul("b_dwq", [(u1, dq_raw)], "tn", D, td, tl, [F32])
    (du1,) = _matmul("b_du1", [(dq_raw, w["sb_w_q"])], "nt", tl, D, D, [F32])
    dh4, dgv = _rms_bwd("b_anorm", du1, h4, g[1, 1], dh5)
    dg = dg.at[1, 1].set(dgv.reshape(-1))
    dh3, dgv, dw1[1][0], dw3[1][0], dw2[1][0] = _ffn_bwd("b10", dh4, h3, g[1, 0], w["ffn_w1"][1, 0], w["ffn_w3"][1, 0],
                                                           w["ffn_w2"][1, 0], s3)
    dg = dg.at[1, 0].set(dgv)

    dk_raw, dkg = _headrms_bwd("b_knorm", dk, k_raw, kg)
    grads["k_norm_g"] = dkg.reshape(D // HEAD, HEAD).sum(0)
    (grads["w_k"],) = _matmul("b_dwk", [(kv_in, dk_raw)], "tn", D, td, tl, [F32])
    (grads["w_v"],) = _matmul("b_dwv", [(kv_in, dv)], "tn", D, td, tl, [F32])
    (dkv_in,) = _matmul("b_dkvin", [(dk_raw, w["w_k"]), (dv, w["w_v"])], "nt", tl, D, D, [F32])
    dh3, dgv = _rms_bwd("b_kvnorm", dkv_in, h3, w["kv_norm_g"], dh3)
    grads["kv_norm_g"] = dgv.reshape(-1)

    dh2, dgv, dw1[0][1], dw3[0][1], dw2[0][1] = _ffn_bwd("b01", dh3, h2, g[0, 2], w["ffn_w1"][0, 1], w["ffn_w3"][0, 1],
                                                           w["ffn_w2"][0, 1], s2)
    dg = dg.at[0, 2].set(dgv)
    (dy3,) = _matmul("b_dy3", [(dh2, w["ssm_out_proj"])], "nt", tl, _tile(di, 1024, LANES), D, [F32])
    (grads["ssm_out_proj"],) = _matmul("b_dwout", [(y3, dh2)], "tn", _tile(di, 1024, LANES), D, tl, [F32])
    dz, dy1, dssm_g, dd_x = _gate_bwd("b_gate", dy3, y_ssd, xbc, zx, d_x, ssm_g, gsz)
    grads["ssm_norm_g"] = dssm_g.reshape(-1)
    grads["ssm_d"] = dd_x.reshape(nh, HEAD).sum(1)
    dxbc, ddt, da = _ssd_bwd("b_ssd", xbc, dt, dtt, a_row, a_col, expand, states, dy1, d_x, di, ng)
    grads["ssm_a_log"] = da[0, :nh] * a_neg
    ddt_raw, dbias = _dt_bwd("b_dt", ddt, zx, di + conv_dim, bias_row, nh, pad)
    grads["ssm_dt_bias"] = dbias[0, :nh]
    dpre, grads["ssm_conv_w"], dconv_b = _conv_bwd_pre("b_convpre", dxbc, zx, di, w["ssm_conv_w"], conv_b, pad)
    grads["ssm_conv_b"] = dconv_b.reshape(-1)
    dxbc_raw = _conv_bwd_in("b_convin", dpre, w["ssm_conv_w"])
    dzx = jnp.concatenate([dz, dxbc_raw, ddt_raw], axis=1)
    (dw_in,) = _matmul("b_dwin", [(u0, dzx)], "tn", D, _tile(zp, 1024, LANES), tl, [F32])
    grads["ssm_in_proj"] = dw_in[:, :w["ssm_in_proj"].shape[1]]
    (du0,) = _matmul("b_du0", [(dzx, w_in)], "nt", _tile(L, 352, 16), td, zp, [F32])
    dh1, dgv = _rms_bwd("b_mnorm", du0, h1, g[0, 1], dh2)
    dg = dg.at[0, 1].set(dgv.reshape(-1))
    dh0, dgv, dw1[0][0], dw3[0][0], dw2[0][0] = _ffn_bwd("b00", dh1, h0, g[0, 0], w["ffn_w1"][0, 0], w["ffn_w3"][0, 0],
                                                           w["ffn_w2"][0, 0], s1)
    dg = dg.at[0, 0].set(dgv)

    grads["norm_g"] = dg
    grads["ffn_w1"] = jnp.stack([jnp.stack(r) for r in dw1])
    grads["ffn_w3"] = jnp.stack([jnp.stack(r) for r in dw3])
    grads["ffn_w2"] = jnp.stack([jnp.stack(r) for r in dw2])
    grads["meta_tokens"] = dh0[pad:pad + nm]
    return sq, dh0[pad + nm:], grads


HBM_SPEC = pl.BlockSpec(memory_space=pltpu.HBM)
PAIR_PIECES = 8


def _place():
    return lax.axis_index("x"), lax.axis_index("y"), lax.axis_index("c")


def _allgather8(name, blk):
    m, n = blk.shape

    def body(x_ref, out_ref, send_sems, recv_sems, local_sem):
        x, y, c = _place()
        me, sibling = (x, y, c), (x, y, 1 - c)
        chips = [(1 - x, y), (x, 1 - y), (1 - x, 1 - y)]

        def rows(px, py, pc):
            return out_ref.at[pl.ds((4 * px + 2 * py + pc) * m, m), :]

        def copy(k, block, to, src=None):
            return pltpu.make_async_remote_copy(
                src_ref=rows(*block) if src is None else src, dst_ref=rows(*block),
                send_sem=send_sems.at[k], recv_sem=recv_sems.at[k], device_id=to, device_id_type=MESH)

        mine = pltpu.make_async_copy(x_ref, rows(*me), local_sem)
        mine.start()
        first = [copy(0, me, sibling, src=x_ref)]
        first += [copy(1 + j, me, (*chip, c), src=x_ref) for j, chip in enumerate(chips)]
        for cp in first:
            cp.start()
        passed = [copy(4 + j, (*chip, c), sibling) for j, chip in enumerate(chips)]
        for j, chip in enumerate(chips):
            copy(1 + j, (*chip, c), me).wait_recv()
            passed[j].start()
        copy(0, sibling, me).wait_recv()
        for j, chip in enumerate(chips):
            copy(4 + j, (*chip, 1 - c), me).wait_recv()
        for cp in first + passed:
            cp.wait_send()
        mine.wait()

    return pl.pallas_call(
        body, name=name, out_shape=jax.ShapeDtypeStruct((8 * m, n), blk.dtype),
        in_specs=[HBM_SPEC], out_specs=HBM_SPEC,
        scratch_shapes=[pltpu.SemaphoreType.DMA((7,)), pltpu.SemaphoreType.DMA((7,)), pltpu.SemaphoreType.DMA],
    )(blk)


def _exchange4(name, g):
    _, m, n = g.shape

    def body(g_ref, out_ref, send_sems, recv_sems, local_sem):
        x, y, c = _place()
        me = 2 * x + y
        mine = pltpu.make_async_copy(g_ref.at[me], out_ref.at[me], local_sem)
        mine.start()
        sends, recvs = [], []
        for k, (fx, fy) in enumerate([(0, 1), (1, 0), (1, 1)]):
            px = 1 - x if fx else x
            py = 1 - y if fy else y
            pid = 2 * px + py
            sends.append(pltpu.make_async_remote_copy(
                src_ref=g_ref.at[pid], dst_ref=out_ref.at[me], send_sem=send_sems.at[k],
                recv_sem=recv_sems.at[k], device_id=(px, py, c), device_id_type=MESH))
            recvs.append(pltpu.make_async_remote_copy(
                src_ref=g_ref.at[me], dst_ref=out_ref.at[pid], send_sem=send_sems.at[k],
                recv_sem=recv_sems.at[k], device_id=(px, py, c), device_id_type=MESH))
        for cp in sends:
            cp.start()
        for cp in recvs:
            cp.wait_recv()
        for cp in sends:
            cp.wait_send()
        mine.wait()

    return pl.pallas_call(
        body, name=name, out_shape=jax.ShapeDtypeStruct(g.shape, g.dtype), in_specs=[HBM_SPEC], out_specs=HBM_SPEC,
        scratch_shapes=[pltpu.SemaphoreType.DMA((3,)), pltpu.SemaphoreType.DMA((3,)), pltpu.SemaphoreType.DMA],
    )(g)


def _pairswap(name, mine):
    m, n = mine.shape
    pieces = PAIR_PIECES if m % (8 * PAIR_PIECES) == 0 else 1
    pm = m // pieces

    def body(x_ref, got_ref, send_sems, recv_sems):
        x, y, c = _place()

        def copy(j):
            rows = pl.ds(j * pm, pm)
            return pltpu.make_async_remote_copy(
                src_ref=x_ref.at[rows, :], dst_ref=got_ref.at[rows, :], send_sem=send_sems.at[j],
                recv_sem=recv_sems.at[j], device_id=(x, y, 1 - c), device_id_type=MESH)

        copies = [copy(j) for j in range(pieces)]
        for cp in copies:
            cp.start()
        for cp in copies:
            cp.wait()

    return pl.pallas_call(
        body, name=name, out_shape=jax.ShapeDtypeStruct((m, n), mine.dtype), in_specs=[HBM_SPEC], out_specs=HBM_SPEC,
        scratch_shapes=[pltpu.SemaphoreType.DMA((pieces,)), pltpu.SemaphoreType.DMA((pieces,))],
    )(mine)


def _pairshare(name, half):
    got = _pairswap(name, half)
    south = lax.axis_index("c") == 0
    return jnp.concatenate([jnp.where(south, half, got), jnp.where(south, got, half)], axis=0)


def _sum_blocks(name, parts, out_dtype):
    k, m, n = parts.shape
    tr = _tile(m, 512, 16)

    def body(p_ref, o_ref):
        acc = p_ref[0].astype(F32)
        for s in range(1, k):
            acc = acc + p_ref[s].astype(F32)
        o_ref[...] = acc.astype(o_ref.dtype)

    return pl.pallas_call(
        body, name=name, grid=(m // tr,), in_specs=[pl.BlockSpec((k, tr, n), lambda i: (0, i, 0))],
        out_specs=pl.BlockSpec((tr, n), lambda i: (i, 0)), out_shape=jax.ShapeDtypeStruct((m, n), out_dtype),
        compiler_params=_params("parallel"),
    )(parts)


def _sum_pair(name, a, b, out_dtype):
    m, n = a.shape
    tr = _tile(m, 512, 16)

    def body(a_ref, b_ref, o_ref):
        o_ref[...] = (a_ref[...].astype(F32) + b_ref[...].astype(F32)).astype(o_ref.dtype)

    spec = pl.BlockSpec((tr, n), lambda i: (i, 0))
    return pl.pallas_call(
        body, name=name, grid=(m // tr,), in_specs=[spec, spec], out_specs=spec,
        out_shape=jax.ShapeDtypeStruct((m, n), out_dtype), compiler_params=_params("parallel"),
    )(a, b)


def _adamw(name, w, g, m, v):
    shape = w.shape
    cols = shape[-1]
    rows = math.prod(shape[:-1])
    tr = _tile(rows, 512, 8) if rows * cols > 2 ** 19 else rows

    def body(w_ref, g_ref, m_ref, v_ref, d_ref, mo_ref, vo_ref):
        gv = g_ref[...]
        m2 = ADAM_B1 * m_ref[...] + (1.0 - ADAM_B1) * gv
        v2 = ADAM_B2 * v_ref[...] + (1.0 - ADAM_B2) * (gv * gv)
        m_hat = m2 / (1.0 - ADAM_B1 ** ADAM_STEP)
        v_hat = v2 / (1.0 - ADAM_B2 ** ADAM_STEP)
        d_ref[...] = -ADAM_LR * (m_hat / (jnp.sqrt(v_hat) + ADAM_EPS) + ADAM_WD * w_ref[...])
        mo_ref[...] = m2
        vo_ref[...] = v2

    spec = pl.BlockSpec((tr, cols), lambda i: (i, 0))
    outs = pl.pallas_call(
        body, name=name, grid=(rows // tr,), in_specs=[spec] * 4, out_specs=[spec] * 3,
        out_shape=[jax.ShapeDtypeStruct((rows, cols), F32)] * 3, compiler_params=_params("parallel"),
    )(*(a.reshape(rows, cols) for a in (w, g, m, v)))
    return tuple(o.reshape(shape) for o in outs)


WEIGHTS = ["meta_tokens", "norm_g", "ffn_w1", "ffn_w3", "ffn_w2", "ssm_in_proj", "ssm_conv_w", "ssm_conv_b", "ssm_dt_bias",
           "ssm_a_log", "ssm_d", "ssm_norm_g", "ssm_out_proj", "kv_norm_g", "w_k", "k_norm_g", "w_v", "sb_w_q",
           "sb_q_norm_g", "sb_w_o"]
SHARD_AXIS = {"meta_tokens": 1, "norm_g": 2, "ffn_w1": 3, "ffn_w3": 3, "ffn_w2": 2, "ssm_in_proj": 2, "ssm_conv_w": 2,
              "ssm_conv_b": 1, "ssm_norm_g": 1, "ssm_out_proj": 1, "w_k": 0, "w_v": 0, "sb_w_q": 1, "sb_w_o": 1}
MATRICES = ["ffn_w1", "ffn_w3", "ffn_w2", "ssm_in_proj", "ssm_out_proj", "w_k", "w_v", "sb_w_q", "sb_w_o"]
VECTORS = [n for n in WEIGHTS if n in SHARD_AXIS and n not in MATRICES]
REPLICATED = [n for n in WEIGHTS if n not in SHARD_AXIS]
LAYER_AXIS = ("ssm_", "sb_")
PACK_COLS = 1024
N_CHIPS = 4


def _pack(arrays, row_mult, dtype):
    segs = []
    for a in arrays:
        n = math.prod(a.shape)
        r = -(-n // PACK_COLS)
        flat = a.reshape(-1).astype(dtype)
        if r * PACK_COLS != n:
            flat = jnp.pad(flat, (0, r * PACK_COLS - n))
        segs.append(flat.reshape(r, PACK_COLS))
    rows = sum(s.shape[0] for s in segs)
    extra = -rows % row_mult
    if extra:
        segs.append(jnp.zeros((extra, PACK_COLS), dtype))
    return jnp.concatenate(segs, axis=0)


def _unpack(packed, shapes):
    lead = packed.shape[:-2]
    out, r0 = [], 0
    for shp in shapes:
        n = math.prod(shp)
        r = -(-n // PACK_COLS)
        seg = packed[..., r0:r0 + r, :]
        if r * PACK_COLS != n:
            seg = seg.reshape(*lead, r * PACK_COLS)[..., :n]
        out.append(seg.reshape(*lead, *shp))
        r0 += r
    return out


def _join(stack, axis):
    return jnp.concatenate([stack[s] for s in range(N_CHIPS)], axis=axis)


def _shards(full, axis):
    n = full.shape[axis] // N_CHIPS
    return [lax.slice_in_dim(full, s * n, (s + 1) * n, axis=axis) for s in range(N_CHIPS)]


def _drop_layer(name, a):
    return a[0] if name.startswith(LAYER_AXIS) else a


def kernel(x, meta_tokens, norm_g, ffn_w1, ffn_w3, ffn_w2, ssm_in_proj, ssm_conv_w, ssm_conv_b, ssm_dt_bias, ssm_a_log, ssm_d, ssm_norm_g, ssm_out_proj, kv_norm_g, w_k, k_norm_g, w_v, sb_w_q, sb_q_norm_g, sb_w_o, loss_target, m_meta_tokens, m_norm_g, m_ffn_w1, m_ffn_w3, m_ffn_w2, m_ssm_in_proj, m_ssm_conv_w, m_ssm_conv_b, m_ssm_dt_bias, m_ssm_a_log, m_ssm_d, m_ssm_norm_g, m_ssm_out_proj, m_kv_norm_g, m_w_k, m_k_norm_g, m_w_v, m_sb_w_q, m_sb_q_norm_g, m_sb_w_o, v_meta_tokens, v_norm_g, v_ffn_w1, v_ffn_w3, v_ffn_w2, v_ssm_in_proj, v_ssm_conv_w, v_ssm_conv_b, v_ssm_dt_bias, v_ssm_a_log, v_ssm_d, v_ssm_norm_g, v_ssm_out_proj, v_kv_norm_g, v_w_k, v_k_norm_g, v_w_v, v_sb_w_q, v_sb_q_norm_g, v_sb_w_o):
    args = locals()
    w_in = {n: args[n] for n in WEIGHTS}
    m_in = {n: args["m_" + n] for n in WEIGHTS}
    v_in = {n: args["v_" + n] for n in WEIGHTS}
    c = lax.axis_index("c")

    def gather(names, dtype, row_mult):
        packed = _pack([w_in[n] for n in names], 2 * row_mult, dtype)
        half = packed.shape[0] // 2
        mine = lax.dynamic_slice_in_dim(packed, c * half, half, axis=0)
        allp = _allgather8(f"gather_{jnp.dtype(dtype).name}", mine).reshape(N_CHIPS, 2 * half, PACK_COLS)
        stacks = _unpack(allp, [w_in[n].shape for n in names])
        return {n: _join(s, SHARD_AXIS[n]) for n, s in zip(names, stacks)}

    full = {**gather(MATRICES, BF16, 16), **gather(VECTORS, F32, 8)}
    full.update({n: w_in[n] for n in REPLICATED})
    full = {n: _drop_layer(n, a) for n, a in full.items()}

    sq, grad_x, grads = _local_step(x[0], loss_target[0], full)
    loss = lax.psum(0.5 / x.shape[-1] * jnp.sum(sq), ("x", "y", "c"))
    grads = {n: (g[None] if n.startswith(LAYER_AXIS) else g) for n, g in grads.items()}

    sharded = MATRICES + VECTORS
    per_chip = [[] for _ in range(N_CHIPS)]
    for n in sharded:
        for p, s in enumerate(_shards(grads[n], SHARD_AXIS[n])):
            per_chip[p].append(s)
    contrib = jnp.stack([_pack(parts, 16 * PAIR_PIECES, BF16) for parts in per_chip])
    half_rows = contrib.shape[1] // 2
    halves = contrib.reshape(N_CHIPS, 2, half_rows, PACK_COLS)

    def take(core):
        return lax.dynamic_index_in_dim(halves, core, axis=1, keepdims=False).reshape(N_CHIPS * half_rows, PACK_COLS)

    theirs = _pairswap("grad_pair", take(1 - c))
    pair_sum = _sum_pair("grad_pair_sum", take(c), theirs, BF16).reshape(N_CHIPS, half_rows, PACK_COLS)
    got = _exchange4("grad_exchange", pair_sum)
    reduced = _pairshare("grad_share", _sum_blocks("grad_sum", got, F32))
    g_out = dict(zip(sharded, _unpack(reduced, [w_in[n].shape for n in sharded])))

    rep = _pack([grads[n] for n in REPLICATED], 8, F32)
    rep_sum = _sum_blocks("rep_sum", _allgather8("rep_gather", rep).reshape(8, rep.shape[0], PACK_COLS), F32)
    g_out.update(zip(REPLICATED, _unpack(rep_sum, [w_in[n].shape for n in REPLICATED])))

    delta, new_m, new_v = {}, {}, {}
    for n in WEIGHTS:
        delta[n], new_m[n], new_v[n] = _adamw(f"adamw_{n}", w_in[n], g_out[n], m_in[n], v_in[n])
    return (loss, grad_x[None], *[g_out[n] for n in WEIGHTS], *[delta[n] for n in WEIGHTS],
            *[new_m[n] for n in WEIGHTS], *[new_v[n] for n in WEIGHTS])
```

```python
import functools
import math

import jax
import jax.numpy as jnp
from jax import lax
from jax.experimental import pallas as pl
from jax.experimental.pallas import tpu as pltpu

F32, BF16 = jnp.float32, jnp.bfloat16
RMS_EPS = 1e-6
LANES = 128
HEAD = 64
D_STATE = 128
BLK = 128
FFN_RES = 0.5
VMEM_LIMIT = 56 * 2 ** 20
ADAM_LR, ADAM_B1, ADAM_B2, ADAM_EPS, ADAM_WD, ADAM_STEP = 0.001, 0.9, 0.999, 1e-08, 0.01, 10
MESH = pl.DeviceIdType.MESH

NN = (((1,), (0,)), ((), ()))
NT = (((1,), (1,)), ((), ()))
TN = (((0,), (0,)), ((), ()))


def _dot(a, b, dn=NN):
    return lax.dot_general(a, b, dn, preferred_element_type=F32)


def _split(x, parts):
    out = []
    for _ in range(parts):
        p = x.astype(BF16)
        out.append(p)
        x = x - p.astype(F32)
    return out


def _dotx(a, b, dn=NN, parts=3, split="a"):
    if split == "a":
        return sum(_dot(p, b, dn) for p in _split(a, parts))
    return sum(_dot(a, p, dn) for p in _split(b, parts))


def _tile(n, target, mult):
    best = None
    for d in range(mult, min(n, target) + 1, mult):
        if n % d == 0:
            best = d
    return n if best is None else best


def _params(*sem):
    return pltpu.CompilerParams(dimension_semantics=tuple(sem) if sem else None, vmem_limit_bytes=VMEM_LIMIT)


def _iota(shape, axis):
    return lax.broadcasted_iota(jnp.int32, shape, axis)


def _sigmoid(x):
    return 1.0 / (1.0 + jnp.exp(-x))


def _matmul(name, pairs, mode, tm, tn, tk, out_dtypes, epilogue=None, extras=(), separate=False):
    a0, b0 = pairs[0]
    if mode == "nn":
        (M, K), N = a0.shape, b0.shape[1]
    elif mode == "nt":
        (M, K), N = a0.shape, b0.shape[0]
    else:
        (K, M), N = a0.shape, b0.shape[1]
    assert M % tm == 0 and N % tn == 0 and K % tk == 0, (name, M, N, K, tm, tn, tk)
    nM, nN, nK = M // tm, N // tn, K // tk
    np_, ne, no = len(pairs), len(extras), len(out_dtypes)
    n_acc = np_ if separate else 1
    dn = {"nn": NN, "nt": NT, "tn": TN}[mode]

    def body(*refs):
        ab, ex = refs[:2 * np_], refs[2 * np_:2 * np_ + ne]
        outs, accs = refs[2 * np_ + ne:2 * np_ + ne + no], refs[2 * np_ + ne + no:]
        k = pl.program_id(2)

        def prod(i):
            return _dot(ab[2 * i][...].astype(BF16), ab[2 * i + 1][...].astype(BF16), dn)

        ps = [prod(i) for i in range(np_)]
        if not separate:
            ps = [functools.reduce(lambda u, v: u + v, ps)]

        def finish(vals):
            res = epilogue(vals, [e[...] for e in ex]) if epilogue is not None else vals
            for o, r in zip(outs, res):
                o[...] = r.astype(o.dtype)

        if nK == 1:
            finish(ps)
        else:
            @pl.when(k == 0)
            def _():
                for acc, p in zip(accs, ps):
                    acc[...] = p

            @pl.when(k > 0)
            def _():
                for acc, p in zip(accs, ps):
                    acc[...] += p

            @pl.when(k == nK - 1)
            def _():
                finish([acc[...] for acc in accs])

    if mode == "tn":
        a_spec = pl.BlockSpec((tk, tm), lambda n, m, k: (k, m))
    else:
        a_spec = pl.BlockSpec((tm, tk), lambda n, m, k: (m, k))
    if mode == "nt":
        b_spec = pl.BlockSpec((tn, tk), lambda n, m, k: (n, k))
    else:
        b_spec = pl.BlockSpec((tk, tn), lambda n, m, k: (k, n))
    in_specs, args = [], []
    for a, b in pairs:
        in_specs += [a_spec, b_spec]
        args += [a, b]
    for arr, kind in extras:
        if kind == "mn":
            in_specs.append(pl.BlockSpec((tm, tn), lambda n, m, k: (m, n)))
        else:
            in_specs.append(pl.BlockSpec((1, tn), lambda n, m, k: (0, n)))
        args.append(arr)
    out_specs = [pl.BlockSpec((tm, tn), lambda n, m, k: (m, n)) for _ in out_dtypes]
    res = pl.pallas_call(
        body, name=name, grid=(nN, nM, nK), in_specs=in_specs, out_specs=out_specs,
        out_shape=[jax.ShapeDtypeStruct((M, N), d) for d in out_dtypes],
        scratch_shapes=[pltpu.VMEM((tm, tn), F32) for _ in range(n_acc)] if nK > 1 else [],
        compiler_params=_params("parallel", "parallel", "arbitrary"),
    )(*args)
    return res


def _rms_fwd(name, h, g):
    L, D = h.shape
    tr = _tile(L, 1024, 16)

    def body(h_ref, g_ref, o_ref):
        x = h_ref[...]
        r = lax.rsqrt(jnp.mean(x * x, axis=-1, keepdims=True) + RMS_EPS)
        o_ref[...] = (x * r * g_ref[...]).astype(BF16)

    return pl.pallas_call(
        body, name=name, grid=(L // tr,),
        in_specs=[pl.BlockSpec((tr, D), lambda i: (i, 0)), pl.BlockSpec((1, D), lambda i: (0, 0))],
        out_specs=pl.BlockSpec((tr, D), lambda i: (i, 0)),
        out_shape=jax.ShapeDtypeStruct((L, D), BF16), compiler_params=_params("parallel"),
    )(h, g.reshape(1, D))


def _rms_bwd(name, dxn, h, g, dres):
    L, D = h.shape
    tr = _tile(L, 512, 8)

    def body(dxn_ref, h_ref, g_ref, dres_ref, dh_ref, dg_ref):
        x = h_ref[...]
        r = lax.rsqrt(jnp.mean(x * x, axis=-1, keepdims=True) + RMS_EPS)
        xh = x * r
        dxn = dxn_ref[...]
        dxh = dxn * g_ref[...]
        dh_ref[...] = dres_ref[...] + r * (dxh - xh * jnp.mean(dxh * xh, axis=-1, keepdims=True))

        @pl.when(pl.program_id(0) == 0)
        def _():
            dg_ref[...] = jnp.zeros_like(dg_ref)

        dg_ref[...] += jnp.sum(dxn * xh, axis=0, keepdims=True)

    row = pl.BlockSpec((tr, D), lambda i: (i, 0))
    vec = pl.BlockSpec((1, D), lambda i: (0, 0))
    return pl.pallas_call(
        body, name=name, grid=(L // tr,), in_specs=[row, row, vec, row], out_specs=[row, vec],
        out_shape=[jax.ShapeDtypeStruct((L, D), F32), jax.ShapeDtypeStruct((1, D), F32)],
        compiler_params=_params("arbitrary"),
    )(dxn, h, g.reshape(1, D), dres)


def _head_sums(x2):
    blockdiag = (_iota((LANES, LANES), 0) // HEAD == _iota((LANES, LANES), 1) // HEAD).astype(BF16)
    cols = [_dotx(x2[:, j:j + LANES], blockdiag) for j in range(0, x2.shape[1], LANES)]
    return jnp.concatenate(cols, axis=1) if len(cols) > 1 else cols[0]


def _headrms_fwd(name, raw, g):
    L, D = raw.shape
    tr = _tile(L, 512, 16)

    def body(x_ref, g_ref, o_ref):
        x = x_ref[...]
        r = lax.rsqrt(_head_sums(x * x) * (1.0 / HEAD) + RMS_EPS)
        o_ref[...] = (x * r * g_ref[...]).astype(BF16)

    return pl.pallas_call(
        body, name=name, grid=(L // tr,),
        in_specs=[pl.BlockSpec((tr, D), lambda i: (i, 0)), pl.BlockSpec((1, D), lambda i: (0, 0))],
        out_specs=pl.BlockSpec((tr, D), lambda i: (i, 0)),
        out_shape=jax.ShapeDtypeStruct((L, D), BF16), compiler_params=_params("parallel"),
    )(raw, g)


def _headrms_bwd(name, dy, raw, g):
    L, D = raw.shape
    tr = _tile(L, 512, 16)

    def body(dy_ref, x_ref, g_ref, dx_ref, dg_ref):
        x = x_ref[...]
        dy = dy_ref[...]
        r = lax.rsqrt(_head_sums(x * x) * (1.0 / HEAD) + RMS_EPS)
        xh = x * r
        dxh = dy * g_ref[...]
        dx_ref[...] = (r * (dxh - xh * (_head_sums(dxh * xh) * (1.0 / HEAD)))).astype(BF16)

        @pl.when(pl.program_id(0) == 0)
        def _():
            dg_ref[...] = jnp.zeros_like(dg_ref)

        dg_ref[...] += jnp.sum(dy * xh, axis=0, keepdims=True)

    row = pl.BlockSpec((tr, D), lambda i: (i, 0))
    vec = pl.BlockSpec((1, D), lambda i: (0, 0))
    return pl.pallas_call(
        body, name=name, grid=(L // tr,), in_specs=[row, row, vec], out_specs=[row, vec],
        out_shape=[jax.ShapeDtypeStruct((L, D), BF16), jax.ShapeDtypeStruct((1, D), F32)],
        compiler_params=_params("arbitrary"),
    )(dy, raw, g)


def _loss(name, h, tgt, pad_rows):
    L, D = h.shape
    nb = L // BLK
    assert pad_rows == BLK

    def body(h_ref, t_ref, dh_ref, s_ref):
        i = pl.program_id(0)

        @pl.when(i == 0)
        def _():
            s_ref[...] = jnp.zeros_like(s_ref)
            dh_ref[...] = jnp.zeros_like(dh_ref)

        @pl.when(i > 0)
        def _():
            e = h_ref[...] - t_ref[...]
            dh_ref[...] = e * (1.0 / D)
            s_ref[...] += jnp.sum(e * e, axis=0, keepdims=True)

    return pl.pallas_call(
        body, name=name, grid=(nb,),
        in_specs=[pl.BlockSpec((BLK, D), lambda i: (i, 0)), pl.BlockSpec((BLK, D), lambda i: (jnp.maximum(i - 1, 0), 0))],
        out_specs=[pl.BlockSpec((BLK, D), lambda i: (i, 0)), pl.BlockSpec((1, D), lambda i: (0, 0))],
        out_shape=[jax.ShapeDtypeStruct((L, D), F32), jax.ShapeDtypeStruct((1, D), F32)],
        compiler_params=_params("arbitrary"),
    )(h, tgt)


def _swiglu_up(name, xn, w1, w3):
    L, D = xn.shape
    Fd = w1.shape[1]
    tm, tn = _tile(L, 704, 16), _tile(Fd, 1408, LANES)

    def epi(accs, _):
        a, b = accs
        return [a, b, a * _sigmoid(a) * b]

    return _matmul(name, [(xn, w1), (xn, w3)], "nn", tm, tn, D, [BF16, BF16, BF16], epilogue=epi, separate=True)


def _swiglu_bwd(name, dh, w2, a, b):
    L, D = dh.shape
    Fd = w2.shape[0]
    tm, tn = _tile(L, 704, 16), _tile(Fd, 1408, LANES)

    def epi(accs, ex):
        dact = accs[0] * FFN_RES
        av, bv = ex[0].astype(F32), ex[1].astype(F32)
        s = _sigmoid(av)
        return [dact * bv * (s * (1.0 + av * (1.0 - s))), dact * av * s]

    return _matmul(name, [(dh, w2)], "nt", tm, tn, D, [BF16, BF16], epilogue=epi, extras=[(a, "mn"), (b, "mn")])


def _ffn_fwd(tag, h, g, w1, w3, w2):
    L, D = h.shape
    xn = _rms_fwd(f"{tag}_norm", h, g)
    a, b, act = _swiglu_up(f"{tag}_up", xn, w1, w3)
    tm = _tile(L, 704, 8)
    (h_out,) = _matmul(f"{tag}_down", [(act, w2)], "nn", tm, D, w2.shape[0], [F32],
                       epilogue=lambda accs, ex: [ex[0] + FFN_RES * accs[0]], extras=[(h, "mn")])
    return h_out, (xn, a, b, act)


def _ffn_bwd(tag, dh_out, h, g, w1, w3, w2, saved):
    xn, a, b, act = saved
    L, D = h.shape
    Fd = w2.shape[0]
    tl = _tile(L, 704, 16)
    da, db = _swiglu_bwd(f"{tag}_dact", dh_out, w2, a, b)
    (dw2,) = _matmul(f"{tag}_dw2", [(act, dh_out)], "tn", _tile(Fd, 1408, LANES), D, tl, [F32],
                     epilogue=lambda accs, ex: [FFN_RES * accs[0]])
    (dw1,) = _matmul(f"{tag}_dw1", [(xn, da)], "tn", D, _tile(Fd, 1408, LANES), tl, [F32])
    (dw3,) = _matmul(f"{tag}_dw3", [(xn, db)], "tn", D, _tile(Fd, 1408, LANES), tl, [F32])
    (dxn,) = _matmul(f"{tag}_dxn", [(da, w1), (db, w3)], "nt", _tile(L, 704, 8), _tile(D, 512, LANES), Fd, [F32])
    dh, dg = _rms_bwd(f"{tag}_dnorm", dxn, h, g, dh_out)
    return dh, dg.reshape(-1), dw1, dw3, dw2


def _conv_taps(ext, k):
    return ext if k == 0 else pltpu.roll(ext, k, axis=0)


def _conv_fwd(name, zx, col0, w, b, pad):
    L = zx.shape[0]
    C = w.shape[1]
    tr, tc = _tile(L, 704, 8), _tile(C, 512, LANES)
    cb = col0 // tc
    assert col0 % tc == 0

    def body(u_ref, halo_ref, w_ref, b_ref, o_ref):
        ext = jnp.concatenate([halo_ref[...], u_ref[...]], axis=0)
        pre = b_ref[...] + sum(_conv_taps(ext, 3 - k)[8:] * w_ref[k:k + 1, :] for k in range(4))
        rows = _iota(pre.shape, 0) + pl.program_id(1) * tr
        o_ref[...] = jnp.where(rows >= pad, pre * _sigmoid(pre), 0.0)

    return pl.pallas_call(
        body, name=name, grid=(C // tc, L // tr),
        in_specs=[pl.BlockSpec((tr, tc), lambda j, i: (i, cb + j)),
                  pl.BlockSpec((8, tc), lambda j, i: (jnp.maximum(i * (tr // 8) - 1, 0), cb + j)),
                  pl.BlockSpec((4, tc), lambda j, i: (0, j)), pl.BlockSpec((1, tc), lambda j, i: (0, j))],
        out_specs=pl.BlockSpec((tr, tc), lambda j, i: (i, j)),
        out_shape=jax.ShapeDtypeStruct((L, C), F32), compiler_params=_params("parallel", "parallel"),
    )(zx, zx, w, b)


def _conv_bwd_pre(name, dact, zx, col0, w, b, pad):
    L = zx.shape[0]
    C = w.shape[1]
    tr, tc = _tile(L, 704, 8), _tile(C, 512, LANES)
    cb = col0 // tc

    def body(d_ref, u_ref, halo_ref, w_ref, b_ref, dp_ref, dw_ref, db_ref):
        ext = jnp.concatenate([halo_ref[...], u_ref[...]], axis=0)
        taps = [_conv_taps(ext, 3 - k)[8:] for k in range(4)]
        pre = b_ref[...] + sum(taps[k] * w_ref[k:k + 1, :] for k in range(4))
        s = _sigmoid(pre)
        rows = _iota(pre.shape, 0) + pl.program_id(1) * tr
        dpre = jnp.where(rows >= pad, d_ref[...] * (s * (1.0 + pre * (1.0 - s))), 0.0)
        dp_ref[...] = dpre

        @pl.when(pl.program_id(1) == 0)
        def _():
            dw_ref[...] = jnp.zeros_like(dw_ref)
            db_ref[...] = jnp.zeros_like(db_ref)

        db_ref[...] += jnp.sum(dpre, axis=0, keepdims=True)
        dw_ref[...] += jnp.concatenate([jnp.sum(dpre * taps[k], axis=0, keepdims=True) for k in range(4)], axis=0)

    return pl.pallas_call(
        body, name=name, grid=(C // tc, L // tr),
        in_specs=[pl.BlockSpec((tr, tc), lambda j, i: (i, j)),
                  pl.BlockSpec((tr, tc), lambda j, i: (i, cb + j)),
                  pl.BlockSpec((8, tc), lambda j, i: (jnp.maximum(i * (tr // 8) - 1, 0), cb + j)),
                  pl.BlockSpec((4, tc), lambda j, i: (0, j)), pl.BlockSpec((1, tc), lambda j, i: (0, j))],
        out_specs=[pl.BlockSpec((tr, tc), lambda j, i: (i, j)), pl.BlockSpec((4, tc), lambda j, i: (0, j)),
                   pl.BlockSpec((1, tc), lambda j, i: (0, j))],
        out_shape=[jax.ShapeDtypeStruct((L, C), F32), jax.ShapeDtypeStruct((4, C), F32), jax.ShapeDtypeStruct((1, C), F32)],
        compiler_params=_params("parallel", "arbitrary"),
    )(dact, zx, zx, w, b)


def _conv_bwd_in(name, dpre, w):
    L, C = dpre.shape
    tr, tc = _tile(L, 704, 16), _tile(C, 512, LANES)
    nr = L // tr

    def body(d_ref, halo_ref, w_ref, o_ref):
        halo = jnp.where(pl.program_id(1) == nr - 1, 0.0, halo_ref[...])
        ext = jnp.concatenate([d_ref[...], halo], axis=0)
        acc = ext[:tr] * w_ref[3:4, :]
        for k in range(3):
            acc = acc + pltpu.roll(ext, tr + 8 - (3 - k), axis=0)[:tr] * w_ref[k:k + 1, :]
        o_ref[...] = acc.astype(BF16)

    return pl.pallas_call(
        body, name=name, grid=(C // tc, nr),
        in_specs=[pl.BlockSpec((tr, tc), lambda j, i: (i, j)),
                  pl.BlockSpec((8, tc), lambda j, i: (jnp.minimum((i + 1) * (tr // 8), L // 8 - 1), j)),
                  pl.BlockSpec((4, tc), lambda j, i: (0, j))],
        out_specs=pl.BlockSpec((tr, tc), lambda j, i: (i, j)),
        out_shape=jax.ShapeDtypeStruct((L, C), BF16), compiler_params=_params("parallel", "parallel"),
    )(dpre, dpre, w)


def _dt_fwd(name, zx, col0, bias_row, nheads, pad):
    L = zx.shape[0]
    cb = col0 // LANES

    def body(x_ref, b_ref, dt_ref, dtt_ref):
        v = x_ref[...] + b_ref[...]
        sp = jnp.maximum(v, 0.0) + jnp.log(1.0 + jnp.exp(-jnp.abs(v)))
        rows = _iota(v.shape, 0) + pl.program_id(0) * BLK
        dt = jnp.where((rows >= pad) & (_iota(v.shape, 1) < nheads), sp, 0.0)
        dt_ref[...] = dt
        dtt_ref[...] = dt.T

    return pl.pallas_call(
        body, name=name, grid=(L // BLK,),
        in_specs=[pl.BlockSpec((BLK, LANES), lambda i: (i, cb)), pl.BlockSpec((1, LANES), lambda i: (0, 0))],
        out_specs=[pl.BlockSpec((BLK, LANES), lambda i: (i, 0)), pl.BlockSpec((LANES, BLK), lambda i: (0, i))],
        out_shape=[jax.ShapeDtypeStruct((L, LANES), F32), jax.ShapeDtypeStruct((LANES, L), F32)],
        compiler_params=_params("parallel"),
    )(zx, bias_row)


def _dt_bwd(name, ddt, zx, col0, bias_row, nheads, pad):
    L = zx.shape[0]
    cb = col0 // LANES

    def body(d_ref, x_ref, b_ref, o_ref, db_ref):
        v = x_ref[...] + b_ref[...]
        rows = _iota(v.shape, 0) + pl.program_id(0) * BLK
        g = jnp.where((rows >= pad) & (_iota(v.shape, 1) < nheads), d_ref[...] * _sigmoid(v), 0.0)
        o_ref[...] = g.astype(BF16)

        @pl.when(pl.program_id(0) == 0)
        def _():
            db_ref[...] = jnp.zeros_like(db_ref)

        db_ref[...] += jnp.sum(g, axis=0, keepdims=True)

    return pl.pallas_call(
        body, name=name, grid=(L // BLK,),
        in_specs=[pl.BlockSpec((BLK, LANES), lambda i: (i, 0)), pl.BlockSpec((BLK, LANES), lambda i: (i, cb)),
                  pl.BlockSpec((1, LANES), lambda i: (0, 0))],
        out_specs=[pl.BlockSpec((BLK, LANES), lambda i: (i, 0)), pl.BlockSpec((1, LANES), lambda i: (0, 0))],
        out_shape=[jax.ShapeDtypeStruct((L, LANES), BF16), jax.ShapeDtypeStruct((1, LANES), F32)],
        compiler_params=_params("arbitrary"),
    )(ddt, zx, bias_row)


def _ssd_common(dt, dtt, a_row, a_col):
    tril = (_iota((BLK, BLK), 0) >= _iota((BLK, BLK), 1)).astype(BF16)
    cum = _dotx(tril, dt * a_row, split="b")
    cumt = _dotx(dtt * a_col, tril, NT)
    return cum, cumt


def _ssd_fwd(name, xbc, dt, dtt, a_row, a_col, expand, di, ng):
    L = xbc.shape[0]
    nc = L // BLK
    hpg = di // HEAD // ng
    gw = hpg * HEAD
    assert gw % LANES == 0

    def body(x_ref, dt_ref, dtt_ref, ar_ref, ac_ref, ex_ref, y_ref, st_ref, h_ref):
        @pl.when(pl.program_id(0) == 0)
        def _():
            h_ref[...] = jnp.zeros_like(h_ref)

        st_ref[0] = h_ref[...]
        dt, dtt = dt_ref[...], dtt_ref[...]
        cum, cumt = _ssd_common(dt, dtt, ar_ref[...], ac_ref[...])
        ex = ex_ref[...]
        ecum_x = _dotx(jnp.exp(cum), ex)
        wend_x = _dotx(jnp.exp(cum[BLK - 1:BLK, :] - cum) * dt, ex)
        ecl = jnp.broadcast_to(jnp.exp(cumt[:, BLK - 1:BLK]), (LANES, LANES))
        decay_h = _dotx(ex, ecl, TN, split="b")
        causal = _iota((BLK, BLK), 0) >= _iota((BLK, BLK), 1)
        low = _iota((BLK, LANES), 1) < HEAD
        for g in range(ng):
            xg = x_ref[:, g * gw:(g + 1) * gw]
            bg = x_ref[:, di + g * D_STATE:di + (g + 1) * D_STATE].astype(BF16)
            cg = x_ref[:, di + (ng + g) * D_STATE:di + (ng + g + 1) * D_STATE].astype(BF16)
            hg = h_ref[g * gw:(g + 1) * gw, :]
            gram = _dot(cg, bg, NT)
            yoff = _dot(cg, hg.astype(BF16), NT) * ecum_x[:, g * gw:(g + 1) * gw]
            parts = []
            for j in range(gw // LANES):
                xp = xg[:, j * LANES:(j + 1) * LANES].astype(BF16)
                yd = []
                for hh in range(2):
                    h = g * hpg + 2 * j + hh
                    seg = cum[:, h:h + 1] - cumt[h:h + 1, :]
                    m = gram * jnp.where(causal, jnp.exp(jnp.minimum(seg, 0.0)), 0.0) * dtt[h:h + 1, :]
                    yd.append(_dot(m.astype(BF16), xp))
                parts.append(jnp.where(low, yd[0], yd[1]))
            y_ref[:, g * gw:(g + 1) * gw] = jnp.concatenate(parts, axis=1) + yoff
            xw = (xg * wend_x[:, g * gw:(g + 1) * gw]).astype(BF16)
            h_ref[g * gw:(g + 1) * gw, :] = hg * decay_h[g * gw:(g + 1) * gw, :] + _dot(xw, bg, TN)

    W = xbc.shape[1]
    full = lambda r, c: pl.BlockSpec((r, c), lambda i: (0, 0))
    return pl.pallas_call(
        body, name=name, grid=(nc,),
        in_specs=[pl.BlockSpec((BLK, W), lambda i: (i, 0)), pl.BlockSpec((BLK, LANES), lambda i: (i, 0)),
                  pl.BlockSpec((LANES, BLK), lambda i: (0, i)), full(1, LANES), full(LANES, LANES), full(LANES, di)],
        out_specs=[pl.BlockSpec((BLK, di), lambda i: (i, 0)), pl.BlockSpec((1, di, D_STATE), lambda i: (i, 0, 0))],
        out_shape=[jax.ShapeDtypeStruct((L, di), F32), jax.ShapeDtypeStruct((nc, di, D_STATE), F32)],
        scratch_shapes=[pltpu.VMEM((di, D_STATE), F32)], compiler_params=_params("arbitrary"),
    )(xbc, dt, dtt, a_row, a_col, expand)


def _ssd_bwd(name, xbc, dt, dtt, a_row, a_col, expand, states, dy, d_x, di, ng):
    L, W = xbc.shape
    nc = L // BLK
    hpg = di // HEAD // ng
    gw = hpg * HEAD

    def body(x_ref, dt_ref, dtt_ref, ar_ref, ac_ref, ex_ref, st_ref, dy_ref, dx_ref_in, dxo_ref, ddt_ref, da_ref, dh_ref):
        @pl.when(pl.program_id(0) == 0)
        def _():
            dh_ref[...] = jnp.zeros_like(dh_ref)
            da_ref[...] = jnp.zeros_like(da_ref)

        dt, dtt, a_row = dt_ref[...], dtt_ref[...], ar_ref[...]
        cum, cumt = _ssd_common(dt, dtt, a_row, ac_ref[...])
        ex = ex_ref[...]
        ecum = jnp.exp(cum)
        ecum_x = _dotx(ecum, ex)
        e_s = jnp.exp(cum[BLK - 1:BLK, :] - cum)
        wend_x = _dotx(e_s * dt, ex)
        ecl_col = jnp.exp(cumt[:, BLK - 1:BLK])
        decay_h = _dotx(ex, jnp.broadcast_to(ecl_col, (LANES, LANES)), TN, split="b")
        causal = _iota((BLK, BLK), 0) >= _iota((BLK, BLK), 1)
        low = _iota((BLK, LANES), 1) < HEAD
        lane = _iota((1, LANES), 1)
        sub = _iota((LANES, 1), 0)
        dcum_c = jnp.zeros((BLK, LANES), F32)
        dcum_r = jnp.zeros((LANES, BLK), F32)
        ddt_r = jnp.zeros((LANES, BLK), F32)
        zoff = []
        dwend_src = []
        for g in range(ng):
            gs = slice(g * gw, (g + 1) * gw)
            xg = x_ref[:, gs]
            bg = x_ref[:, di + g * D_STATE:di + (g + 1) * D_STATE].astype(BF16)
            cg = x_ref[:, di + (ng + g) * D_STATE:di + (ng + g + 1) * D_STATE].astype(BF16)
            hprev = st_ref[0, gs, :]
            dhn = dh_ref[gs, :]
            dyg = dy_ref[:, gs]
            gram = _dot(cg, bg, NT)
            dgram = jnp.zeros((BLK, BLK), F32)
            dxg = []
            for j in range(gw // LANES):
                xp = xg[:, j * LANES:(j + 1) * LANES].astype(BF16)
                dyp = dyg[:, j * LANES:(j + 1) * LANES]
                dxh = []
                for hh in range(2):
                    h = g * hpg + 2 * j + hh
                    seg = cum[:, h:h + 1] - cumt[h:h + 1, :]
                    lm = jnp.where(causal, jnp.exp(jnp.minimum(seg, 0.0)), 0.0)
                    dtr = dtt[h:h + 1, :]
                    m = gram * lm * dtr
                    dym = jnp.where(low if hh == 0 else ~low, dyp, 0.0).astype(BF16)
                    dxh.append(_dot(m.astype(BF16), dym, TN))
                    dm = _dot(dym, xp, NT)
                    dgram = dgram + dm * lm * dtr
                    v = dm * gram * lm
                    wv = v * dtr
                    ddt_r = ddt_r + jnp.where(sub == h, jnp.sum(v, axis=0, keepdims=True), 0.0)
                    dcum_r = dcum_r - jnp.where(sub == h, jnp.sum(wv, axis=0, keepdims=True), 0.0)
                    dcum_c = dcum_c + jnp.where(lane == h, jnp.sum(wv, axis=1, keepdims=True), 0.0)
                dxg.append(jnp.where(low, dxh[0], dxh[1]))
            dx_diag = jnp.concatenate(dxg, axis=1)
            hb = hprev.astype(BF16)
            yoff = _dot(cg, hb, NT) * ecum_x[:, gs]
            dye = (dyg * ecum_x[:, gs]).astype(BF16)
            dcg = _dot(dye, hb) + _dot(dgram.astype(BF16), bg)
            dbg = _dot(dgram.astype(BF16), cg, TN)
            dh_prev = _dot(dye, cg, TN)
            zoff.append(dyg * yoff)
            dhb = dhn.astype(BF16)
            dxw = _dot(bg, dhb, NT)
            xw = (xg * wend_x[:, gs]).astype(BF16)
            dbg = dbg + _dot(xw, dhb)
            dwend_src.append(dxw * xg)
            dxo_ref[:, gs] = dx_diag + dxw * wend_x[:, gs] + dyg * dx_ref_in[:, gs]
            dxo_ref[:, di + g * D_STATE:di + (g + 1) * D_STATE] = dbg
            dxo_ref[:, di + (ng + g) * D_STATE:di + (ng + g + 1) * D_STATE] = dcg
            prod = dhn * hprev
            dd = jnp.sum(_dotx(ex[:, gs], prod, split="b"), axis=1, keepdims=True)
            dcum_r = dcum_r + jnp.where(_iota((1, BLK), 1) == BLK - 1, dd * ecl_col, 0.0)
            dh_ref[gs, :] = dhn * decay_h[gs, :] + dh_prev
        dcum_c = dcum_c + _dotx(jnp.concatenate(zoff, axis=1), ex, NT)
        dwend = _dotx(jnp.concatenate(dwend_src, axis=1), ex, NT)
        ddt_c = dwend * e_s
        de = dwend * dt * e_s
        dcum_c = dcum_c - de + jnp.where(_iota((BLK, 1), 0) == BLK - 1, jnp.sum(de, axis=0, keepdims=True), 0.0)
        dcum = dcum_c + dcum_r.T
        triu = (_iota((BLK, BLK), 0) <= _iota((BLK, BLK), 1)).astype(BF16)
        da = _dotx(triu, dcum, split="b")
        ddt_ref[...] = ddt_c + ddt_r.T + da * a_row
        da_ref[...] += jnp.sum(da * dt, axis=0, keepdims=True)

    rev = lambda i: nc - 1 - i
    full = lambda r, c: pl.BlockSpec((r, c), lambda i: (0, 0))
    return pl.pallas_call(
        body, name=name, grid=(nc,),
        in_specs=[pl.BlockSpec((BLK, W), lambda i: (rev(i), 0)), pl.BlockSpec((BLK, LANES), lambda i: (rev(i), 0)),
                  pl.BlockSpec((LANES, BLK), lambda i: (0, rev(i))), full(1, LANES), full(LANES, LANES), full(LANES, di),
                  pl.BlockSpec((1, di, D_STATE), lambda i: (rev(i), 0, 0)), pl.BlockSpec((BLK, di), lambda i: (rev(i), 0)),
                  full(1, di)],
        out_specs=[pl.BlockSpec((BLK, W), lambda i: (rev(i), 0)), pl.BlockSpec((BLK, LANES), lambda i: (rev(i), 0)),
                   full(1, LANES)],
        out_shape=[jax.ShapeDtypeStruct((L, W), F32), jax.ShapeDtypeStruct((L, LANES), F32),
                   jax.ShapeDtypeStruct((1, LANES), F32)],
        scratch_shapes=[pltpu.VMEM((di, D_STATE), F32)], compiler_params=_params("arbitrary"),
    )(xbc, dt, dtt, a_row, a_col, expand, states, dy, d_x)


def _group_sums(v, gsz):
    cols = []
    for j in range(0, v.shape[1], gsz):
        s = jnp.sum(v[:, j:j + gsz], axis=1, keepdims=True)
        cols.append(jnp.broadcast_to(s, (v.shape[0], gsz)))
    return jnp.concatenate(cols, axis=1)


def _gate_fwd(name, y, xbc, zx, d_x, ng_row, gsz):
    L, di = y.shape
    tr = _tile(L, 512, 16)

    def body(y_ref, x_ref, z_ref, d_ref, g_ref, o_ref):
        z = z_ref[...]
        y2 = (y_ref[...] + d_ref[...] * x_ref[...]) * (z * _sigmoid(z))
        r = lax.rsqrt(_group_sums(y2 * y2, gsz) * (1.0 / gsz) + RMS_EPS)
        o_ref[...] = (y2 * r * g_ref[...]).astype(BF16)

    row = pl.BlockSpec((tr, di), lambda i: (i, 0))
    vec = pl.BlockSpec((1, di), lambda i: (0, 0))
    return pl.pallas_call(
        body, name=name, grid=(L // tr,), in_specs=[row, row, row, vec, vec], out_specs=row,
        out_shape=jax.ShapeDtypeStruct((L, di), BF16), compiler_params=_params("parallel"),
    )(y, xbc, zx, d_x, ng_row)


def _gate_bwd(name, dy3, y, xbc, zx, d_x, ng_row, gsz):
    L, di = y.shape
    tr = _tile(L, 256, 16)

    def body(dy_ref, y_ref, x_ref, z_ref, d_ref, g_ref, dz_ref, dy1_ref, dg_ref, dd_ref):
        z, x = z_ref[...], x_ref[...]
        s = _sigmoid(z)
        sz = z * s
        y1 = y_ref[...] + d_ref[...] * x
        y2 = y1 * sz
        r = lax.rsqrt(_group_sums(y2 * y2, gsz) * (1.0 / gsz) + RMS_EPS)
        yg = y2 * r
        dy3 = dy_ref[...]
        dyg = dy3 * g_ref[...]
        dy2 = r * (dyg - yg * (_group_sums(dyg * yg, gsz) * (1.0 / gsz)))
        dz_ref[...] = (dy2 * y1 * (s * (1.0 + z * (1.0 - s)))).astype(BF16)
        dy1 = dy2 * sz
        dy1_ref[...] = dy1

        @pl.when(pl.program_id(0) == 0)
        def _():
            dg_ref[...] = jnp.zeros_like(dg_ref)
            dd_ref[...] = jnp.zeros_like(dd_ref)

        dg_ref[...] += jnp.sum(dy3 * yg, axis=0, keepdims=True)
        dd_ref[...] += jnp.sum(dy1 * x, axis=0, keepdims=True)

    row = pl.BlockSpec((tr, di), lambda i: (i, 0))
    vec = pl.BlockSpec((1, di), lambda i: (0, 0))
    return pl.pallas_call(
        body, name=name, grid=(L // tr,), in_specs=[row, row, row, row, vec, vec], out_specs=[row, row, vec, vec],
        out_shape=[jax.ShapeDtypeStruct((L, di), BF16), jax.ShapeDtypeStruct((L, di), F32),
                   jax.ShapeDtypeStruct((1, di), F32), jax.ShapeDtypeStruct((1, di), F32)],
        compiler_params=_params("arbitrary"),
    )(dy3, y, xbc, zx, d_x, ng_row)


SB_EDGE = 4
SB_INNER = 8


def _sb_heads(x):
    low = _iota(x.shape, 1) < HEAD
    zero = jnp.zeros_like(x)
    return jnp.concatenate([jnp.where(low, x, zero), jnp.where(low, zero, x)], axis=0)


def _sb_unheads(x2):
    return jnp.where(_iota((BLK, LANES), 1) < HEAD, x2[:BLK], x2[BLK:])


def _sb_tiles(qq, kblks, dmat, scol, thrs, s0s):
    n = range(len(kblks))
    z = [_dot(qq, kblks[u], NT) for u in n]
    e = [jnp.exp(-jnp.abs(z[u])) for u in n]
    l1 = [jnp.log(1.0 + e[u]) for u in n]
    lsz = [jnp.minimum(z[u], 0.0) - l1[u] for u in n]
    if thrs is None:
        return z, e, lsz, [-jnp.maximum(z[u], 0.0) - l1[u] for u in n], None
    valid = [(dmat > thrs[u]) & (scol >= s0s[u]) for u in n]
    lkm = [jnp.where(valid[u], -jnp.maximum(z[u], 0.0) - l1[u], 0.0) for u in n]
    return z, e, lsz, lkm, valid


def _sb_sweep(group, i, init):
    ue, ui = SB_EDGE, SB_INNER
    carry = group(True, ue, 0, init)
    n_inner = jnp.maximum((i - ue) // ui, 0)
    carry = lax.fori_loop(0, n_inner, lambda g, cr: group(False, ui, ue + g * ui, cr), carry)
    off = ue + n_inner * ui
    n_tail = jnp.maximum((i - off + ue) // ue, 0)
    return lax.fori_loop(0, n_tail, lambda g, cr: group(True, ue, off + g * ue, cr), carry)


def _sb_fwd(name, q, k, v, pad):
    L, D = q.shape
    nb = L // BLK

    def body(q_ref, k_ref, v_ref, o_ref, o32_ref):
        i = pl.program_id(1)
        after = (_iota((BLK, BLK), 0) > _iota((BLK, BLK), 1)).astype(BF16)
        qq = _sb_heads(q_ref[...] * 0.125)
        dmat = (_iota((2 * BLK, BLK), 0) & (BLK - 1)) - _iota((2 * BLK, BLK), 1)
        scol = _iota((2 * BLK, BLK), 1)

        def group(masked, U, off, carry):
            c, acc = carry
            rng = range(U)
            kraw = [i - (off + u) for u in rng]
            kb = [jnp.maximum(kraw[u], 0) for u in rng]
            rows = [pl.ds(pl.multiple_of(kb[u] * BLK, BLK), BLK) for u in rng]
            thr = [jnp.where(kraw[u] >= 0, (kb[u] - i) * BLK, BLK) for u in rng] if masked else None
            s0 = [pad - kb[u] * BLK for u in rng] if masked else None
            _, _, lsz, lkm, valid = _sb_tiles(qq, [k_ref[rows[u], :] for u in rng], dmat, scol, thr, s0)
            cum = [_dotx(lkm[u], after, parts=2) for u in rng]
            a = []
            for u in rng:
                w = jnp.exp(lsz[u] + (c + cum[u]))
                a.append((jnp.where(valid[u], w, 0.0) if masked else w).astype(BF16))
                c = c + jnp.sum(lkm[u], axis=1, keepdims=True)
            for u in rng:
                acc = acc + _dot(a[u], v_ref[rows[u], :])
            return c, acc

        _, acc = _sb_sweep(group, i, (jnp.zeros((2 * BLK, 1), F32), jnp.zeros((2 * BLK, LANES), F32)))
        out = _sb_unheads(acc)
        o_ref[...] = out.astype(BF16)
        o32_ref[...] = out

    blk = pl.BlockSpec((BLK, LANES), lambda p, i: (i, p))
    col = pl.BlockSpec((L, LANES), lambda p, i: (0, p))
    return pl.pallas_call(
        body, name=name, grid=(D // LANES, nb), in_specs=[blk, col, col], out_specs=[blk, blk],
        out_shape=[jax.ShapeDtypeStruct((L, D), BF16), jax.ShapeDtypeStruct((L, D), F32)],
        compiler_params=_params("parallel", "arbitrary"),
    )(q, k, v)


def _sb_bwd(name, q, k, v, do, o32, pad):
    L, D = q.shape
    nb = L // BLK

    def body(q_ref, k_ref, v_ref, do_ref, o_ref, dq_ref, dk_ref, dv_ref):
        i = pl.program_id(1)

        @pl.when(i == 0)
        def _():
            dk_ref[...] = jnp.zeros_like(dk_ref)
            dv_ref[...] = jnp.zeros_like(dv_ref)

        after = (_iota((BLK, BLK), 0) > _iota((BLK, BLK), 1)).astype(BF16)
        from_j = (_iota((BLK, BLK), 0) >= _iota((BLK, BLK), 1)).astype(BF16)
        qq = _sb_heads(q_ref[...] * 0.125)
        dd = _sb_heads(do_ref[...])
        dmat = (_iota((2 * BLK, BLK), 0) & (BLK - 1)) - _iota((2 * BLK, BLK), 1)
        scol = _iota((2 * BLK, BLK), 1)
        o2 = jnp.concatenate([o_ref[...], o_ref[...]], axis=0)
        total = jnp.sum(dd.astype(F32) * o2, axis=1, keepdims=True)

        def group(masked, U, off, carry):
            c, met, acc = carry
            rng = range(U)
            kraw = [i - (off + u) for u in rng]
            kb = [jnp.maximum(kraw[u], 0) for u in rng]
            rows = [pl.ds(pl.multiple_of(kb[u] * BLK, BLK), BLK) for u in rng]
            thr = [jnp.where(kraw[u] >= 0, (kb[u] - i) * BLK, BLK) for u in rng] if masked else None
            s0 = [pad - kb[u] * BLK for u in rng] if masked else None
            kblk = [k_ref[rows[u], :] for u in rng]
            z, e, lsz, lkm, valid = _sb_tiles(qq, kblk, dmat, scol, thr, s0)
            da = [_dot(dd, v_ref[rows[u], :], NT) for u in rng]
            cum = [_dotx(lkm[u], after, parts=2) for u in rng]
            a = []
            for u in rng:
                w = jnp.exp(lsz[u] + (c + cum[u]))
                a.append((jnp.where(valid[u], w, 0.0) if masked else w).astype(BF16))
                c = c + jnp.sum(lkm[u], axis=1, keepdims=True)
            dlog = [da[u] * a[u].astype(F32) for u in rng]
            here = [_dotx(dlog[u], from_j, parts=2) for u in rng]
            dz = []
            for u in rng:
                inv = 1.0 / (1.0 + e[u])
                sig = jnp.where(z[u] >= 0.0, inv, e[u] * inv)
                pull = (total - (met + here[u])) * sig
                dz.append((dlog[u] * (1.0 - sig) - (jnp.where(valid[u], pull, 0.0) if masked else pull)).astype(BF16))
                met = met + jnp.sum(dlog[u], axis=1, keepdims=True)
            for u in rng:
                dk_ref[rows[u], :] += _dot(dz[u], qq, TN)
                dv_ref[rows[u], :] += _dot(a[u], dd, TN)
                acc = acc + _dot(dz[u], kblk[u])
            return c, met, acc

        col1 = jnp.zeros((2 * BLK, 1), F32)
        _, _, acc = _sb_sweep(group, i, (col1, col1, jnp.zeros((2 * BLK, LANES), F32)))
        dq_ref[...] = _sb_unheads(acc) * 0.125

    blk = pl.BlockSpec((BLK, LANES), lambda p, i: (i, p))
    col = pl.BlockSpec((L, LANES), lambda p, i: (0, p))
    return pl.pallas_call(
        body, name=name, grid=(D // LANES, nb), in_specs=[blk, col, col, blk, blk], out_specs=[blk, col, col],
        out_shape=[jax.ShapeDtypeStruct((L, D), F32)] * 3, compiler_params=_params("parallel", "arbitrary"),
    )(q, k, v, do, o32)


def _local_step(x, tgt, w):
    S, D = x.shape
    nm = w["meta_tokens"].shape[0]
    pad = BLK - nm
    L = pad + nm + S
    assert L % BLK == 0 and 0 < nm <= BLK
    di = w["ssm_out_proj"].shape[0]
    nh = w["ssm_dt_bias"].shape[0]
    assert di == nh * HEAD and nh <= LANES
    conv_dim = w["ssm_conv_w"].shape[1]
    ng = (conv_dim - di) // (2 * D_STATE)
    zp = di + conv_dim + LANES
    g = w["norm_g"]
    grads = {}

    h0 = jnp.concatenate([jnp.zeros((pad, D), F32), w["meta_tokens"], x], axis=0)

    h1, s1 = _ffn_fwd("f00", h0, g[0, 0], w["ffn_w1"][0, 0], w["ffn_w3"][0, 0], w["ffn_w2"][0, 0])
    u0 = _rms_fwd("m_norm", h1, g[0, 1])
    w_in = jnp.concatenate([w["ssm_in_proj"], jnp.zeros((D, zp - w["ssm_in_proj"].shape[1]), BF16)], axis=1)
    tl = _tile(L, 704, 16)
    (zx,) = _matmul("m_inproj", [(u0, w_in)], "nn", tl, _tile(zp, 1024, LANES), D, [F32])
    conv_b = w["ssm_conv_b"].reshape(1, conv_dim)
    xbc = _conv_fwd("m_conv", zx, di, w["ssm_conv_w"], conv_b, pad)
    bias_row = jnp.zeros((1, LANES), F32).at[0, :nh].set(w["ssm_dt_bias"])
    dt, dtt = _dt_fwd("m_dt", zx, di + conv_dim, bias_row, nh, pad)
    a_neg = -jnp.exp(w["ssm_a_log"])
    a_row = jnp.zeros((1, LANES), F32).at[0, :nh].set(a_neg)
    a_col = jnp.broadcast_to(a_row.reshape(LANES, 1), (LANES, LANES))
    expand = (jnp.arange(LANES)[:, None] == (jnp.arange(di) // HEAD)[None, :]).astype(BF16)
    y_ssd, states = _ssd_fwd("m_ssd", xbc, dt, dtt, a_row, a_col, expand, di, ng)
    d_x = jnp.repeat(w["ssm_d"], HEAD).reshape(1, di)
    ssm_g = w["ssm_norm_g"].reshape(1, di)
    gsz = di // ng
    y3 = _gate_fwd("m_gate", y_ssd, xbc, zx, d_x, ssm_g, gsz)
    (h2,) = _matmul("m_outproj", [(y3, w["ssm_out_proj"])], "nn", tl, D, di, [F32],
                    epilogue=lambda accs, ex: [ex[0] + accs[0]], extras=[(h1, "mn")])
    h3, s2 = _ffn_fwd("f01", h2, g[0, 2], w["ffn_w1"][0, 1], w["ffn_w3"][0, 1], w["ffn_w2"][0, 1])

    kv_in = _rms_fwd("kv_norm", h3, w["kv_norm_g"])
    (k_raw,) = _matmul("kv_k", [(kv_in, w["w_k"])], "nn", tl, D, D, [F32])
    (v_sh,) = _matmul("kv_v", [(kv_in, w["w_v"])], "nn", tl, D, D, [BF16])
    kg = jnp.tile(w["k_norm_g"], D // HEAD).reshape(1, D)
    k_sh = _headrms_fwd("kv_knorm", k_raw, kg)

    h4, s3 = _ffn_fwd("f10", h3, g[1, 0], w["ffn_w1"][1, 0], w["ffn_w3"][1, 0], w["ffn_w2"][1, 0])
    u1 = _rms_fwd("a_norm", h4, g[1, 1])
    (q_raw,) = _matmul("a_q", [(u1, w["sb_w_q"])], "nn", tl, D, D, [F32])
    qg = jnp.tile(w["sb_q_norm_g"], D // HEAD).reshape(1, D)
    q = _headrms_fwd("a_qnorm", q_raw, qg)
    o, o32 = _sb_fwd("a_attn", q, k_sh, v_sh, pad)
    (h5,) = _matmul("a_o", [(o, w["sb_w_o"])], "nn", tl, D, D, [F32],
                    epilogue=lambda accs, ex: [ex[0] + accs[0]], extras=[(h4, "mn")])
    h6, s4 = _ffn_fwd("f11", h5, g[1, 2], w["ffn_w1"][1, 1], w["ffn_w3"][1, 1], w["ffn_w2"][1, 1])

    dh6, sq = _loss("loss", h6, tgt, pad + nm)

    dg = jnp.zeros_like(g)
    dw1 = [[None, None], [None, None]]
    dw3 = [[None, None], [None, None]]
    dw2 = [[None, None], [None, None]]
    dh5, dgv, dw1[1][1], dw3[1][1], dw2[1][1] = _ffn_bwd("b11", dh6, h5, g[1, 2], w["ffn_w1"][1, 1], w["ffn_w3"][1, 1],
                                                           w["ffn_w2"][1, 1], s4)
    dg = dg.at[1, 2].set(dgv)
    td = _tile(D, 512, LANES)
    (do,) = _matmul("b_do", [(dh5, w["sb_w_o"])], "nt", tl, D, D, [BF16])
    (grads["sb_w_o"],) = _matmul("b_dwo", [(o, dh5)], "tn", D, td, tl, [F32])
    dq, dk, dv = _sb_bwd("b_attn", q, k_sh, v_sh, do, o32, pad)
    dq_raw, dqg = _headrms_bwd("b_qnorm", dq, q_raw, qg)
    grads["sb_q_norm_g"] = dqg.reshape(D // HEAD, HEAD).sum(0)
    (grads["sb_w_q"],) = _matmul("b_dwq", [(u1, dq_raw)], "tn", D, td, tl, [F32])
    (du1,) = _matmul("b_du1", [(dq_raw, w["sb_w_q"])], "nt", tl, D, D, [F32])
    dh4, dgv = _rms_bwd("b_anorm", du1, h4, g[1, 1], dh5)
    dg = dg.at[1, 1].set(dgv.reshape(-1))
    dh3, dgv, dw1[1][0], dw3[1][0], dw2[1][0] = _ffn_bwd("b10", dh4, h3, g[1, 0], w["ffn_w1"][1, 0], w["ffn_w3"][1, 0],
                                                           w["ffn_w2"][1, 0], s3)
    dg = dg.at[1, 0].set(dgv)

    dk_raw, dkg = _headrms_bwd("b_knorm", dk, k_raw, kg)
    grads["k_norm_g"] = dkg.reshape(D // HEAD, HEAD).sum(0)
    (grads["w_k"],) = _matmul("b_dwk", [(kv_in, dk_raw)], "tn", D, td, tl, [F32])
    (grads["w_v"],) = _matmul("b_dwv", [(kv_in, dv)], "tn", D, td, tl, [F32])
    (dkv_in,) = _matmul("b_dkvin", [(dk_raw, w["w_k"]), (dv, w["w_v"])], "nt", tl, D, D, [F32])
    dh3, dgv = _rms_bwd("b_kvnorm", dkv_in, h3, w["kv_norm_g"], dh3)
    grads["kv_norm_g"] = dgv.reshape(-1)

    dh2, dgv, dw1[0][1], dw3[0][1], dw2[0][1] = _ffn_bwd("b01", dh3, h2, g[0, 2], w["ffn_w1"][0, 1], w["ffn_w3"][0, 1],
                                                           w["ffn_w2"][0, 1], s2)
    dg = dg.at[0, 2].set(dgv)
    (dy3,) = _matmul("b_dy3", [(dh2, w["ssm_out_proj"])], "nt", tl, _tile(di, 1024, LANES), D, [F32])
    (grads["ssm_out_proj"],) = _matmul("b_dwout", [(y3, dh2)], "tn", _tile(di, 1024, LANES), D, tl, [F32])
    dz, dy1, dssm_g, dd_x = _gate_bwd("b_gate", dy3, y_ssd, xbc, zx, d_x, ssm_g, gsz)
    grads["ssm_norm_g"] = dssm_g.reshape(-1)
    grads["ssm_d"] = dd_x.reshape(nh, HEAD).sum(1)
    dxbc, ddt, da = _ssd_bwd("b_ssd", xbc, dt, dtt, a_row, a_col, expand, states, dy1, d_x, di, ng)
    grads["ssm_a_log"] = da[0, :nh] * a_neg
    ddt_raw, dbias = _dt_bwd("b_dt", ddt, zx, di + conv_dim, bias_row, nh, pad)
    grads["ssm_dt_bias"] = dbias[0, :nh]
    dpre, grads["ssm_conv_w"], dconv_b = _conv_bwd_pre("b_convpre", dxbc, zx, di, w["ssm_conv_w"], conv_b, pad)
    grads["ssm_conv_b"] = dconv_b.reshape(-1)
    dxbc_raw = _conv_bwd_in("b_convin", dpre, w["ssm_conv_w"])
    dzx = jnp.concatenate([dz, dxbc_raw, ddt_raw], axis=1)
    (dw_in,) = _matmul("b_dwin", [(u0, dzx)], "tn", D, _tile(zp, 1024, LANES), tl, [F32])
    grads["ssm_in_proj"] = dw_in[:, :w["ssm_in_proj"].shape[1]]
    (du0,) = _matmul("b_du0", [(dzx, w_in)], "nt", _tile(L, 352, 16), td, zp, [F32])
    dh1, dgv = _rms_bwd("b_mnorm", du0, h1, g[0, 1], dh2)
    dg = dg.at[0, 1].set(dgv.reshape(-1))
    dh0, dgv, dw1[0][0], dw3[0][0], dw2[0][0] = _ffn_bwd("b00", dh1, h0, g[0, 0], w["ffn_w1"][0, 0], w["ffn_w3"][0, 0],
                                                           w["ffn_w2"][0, 0], s1)
    dg = dg.at[0, 0].set(dgv)

    grads["norm_g"] = dg
    grads["ffn_w1"] = jnp.stack([jnp.stack(r) for r in dw1])
    grads["ffn_w3"] = jnp.stack([jnp.stack(r) for r in dw3])
    grads["ffn_w2"] = jnp.stack([jnp.stack(r) for r in dw2])
    grads["meta_tokens"] = dh0[pad:pad + nm]
    return sq, dh0[pad + nm:], grads


HBM_SPEC = pl.BlockSpec(memory_space=pltpu.HBM)
PAIR_PIECES = 8
GATHER_PIECES = 4


def _place():
    return lax.axis_index("x"), lax.axis_index("y"), lax.axis_index("c")


def _allgather8(name, blk):
    m, n = blk.shape
    pieces = GATHER_PIECES if m % (16 * GATHER_PIECES) == 0 else 1
    pm = m // pieces
    rng = range(pieces)

    def body(x_ref, out_ref, send_sems, recv_sems, local_sem):
        x, y, c = _place()
        me, sibling = (x, y, c), (x, y, 1 - c)
        chips = [(1 - x, y), (x, 1 - y), (1 - x, 1 - y)]

        def rows(px, py, pc, j):
            return out_ref.at[pl.ds((4 * px + 2 * py + pc) * m + j * pm, pm), :]

        def copy(k, j, block, to, own=False):
            return pltpu.make_async_remote_copy(
                src_ref=x_ref.at[pl.ds(j * pm, pm), :] if own else rows(*block, j), dst_ref=rows(*block, j),
                send_sem=send_sems.at[k * pieces + j], recv_sem=recv_sems.at[k * pieces + j],
                device_id=to, device_id_type=MESH)

        mine = pltpu.make_async_copy(x_ref, out_ref.at[pl.ds((4 * x + 2 * y + c) * m, m), :], local_sem)
        mine.start()
        first = [copy(0, j, me, sibling, own=True) for j in rng]
        first += [copy(1 + i, j, me, (*chip, c), own=True) for j in rng for i, chip in enumerate(chips)]
        for cp in first:
            cp.start()
        passed = []
        for j in rng:
            for i, chip in enumerate(chips):
                copy(1 + i, j, (*chip, c), me).wait_recv()
                passed.append(copy(4 + i, j, (*chip, c), sibling))
                passed[-1].start()
        for j in rng:
            copy(0, j, sibling, me).wait_recv()
        for j in rng:
            for i, chip in enumerate(chips):
                copy(4 + i, j, (*chip, 1 - c), me).wait_recv()
        for cp in first + passed:
            cp.wait_send()
        mine.wait()

    return pl.pallas_call(
        body, name=name, out_shape=jax.ShapeDtypeStruct((8 * m, n), blk.dtype),
        in_specs=[HBM_SPEC], out_specs=HBM_SPEC,
        scratch_shapes=[pltpu.SemaphoreType.DMA((7 * pieces,)), pltpu.SemaphoreType.DMA((7 * pieces,)),
                        pltpu.SemaphoreType.DMA],
    )(blk)


def _exchange4(name, g):
    _, m, n = g.shape

    def body(g_ref, out_ref, send_sems, recv_sems, local_sem):
        x, y, c = _place()
        me = 2 * x + y
        mine = pltpu.make_async_copy(g_ref.at[me], out_ref.at[me], local_sem)
        mine.start()
        sends, recvs = [], []
        for k, (fx, fy) in enumerate([(0, 1), (1, 0), (1, 1)]):
            px = 1 - x if fx else x
            py = 1 - y if fy else y
            pid = 2 * px + py
            sends.append(pltpu.make_async_remote_copy(
                src_ref=g_ref.at[pid], dst_ref=out_ref.at[me], send_sem=send_sems.at[k],
                recv_sem=recv_sems.at[k], device_id=(px, py, c), device_id_type=MESH))
            recvs.append(pltpu.make_async_remote_copy(
                src_ref=g_ref.at[me], dst_ref=out_ref.at[pid], send_sem=send_sems.at[k],
                recv_sem=recv_sems.at[k], device_id=(px, py, c), device_id_type=MESH))
        for cp in sends:
            cp.start()
        for cp in recvs:
            cp.wait_recv()
        for cp in sends:
            cp.wait_send()
        mine.wait()

    return pl.pallas_call(
        body, name=name, out_shape=jax.ShapeDtypeStruct(g.shape, g.dtype), in_specs=[HBM_SPEC], out_specs=HBM_SPEC,
        scratch_shapes=[pltpu.SemaphoreType.DMA((3,)), pltpu.SemaphoreType.DMA((3,)), pltpu.SemaphoreType.DMA],
    )(g)


def _pairswap(name, mine):
    m, n = mine.shape
    pieces = PAIR_PIECES if m % (8 * PAIR_PIECES) == 0 else 1
    pm = m // pieces

    def body(x_ref, got_ref, send_sems, recv_sems):
        x, y, c = _place()

        def copy(j):
            rows = pl.ds(j * pm, pm)
            return pltpu.make_async_remote_copy(
                src_ref=x_ref.at[rows, :], dst_ref=got_ref.at[rows, :], send_sem=send_sems.at[j],
                recv_sem=recv_sems.at[j], device_id=(x, y, 1 - c), device_id_type=MESH)

        copies = [copy(j) for j in range(pieces)]
        for cp in copies:
            cp.start()
        for cp in copies:
            cp.wait()

    return pl.pallas_call(
        body, name=name, out_shape=jax.ShapeDtypeStruct((m, n), mine.dtype), in_specs=[HBM_SPEC], out_specs=HBM_SPEC,
        scratch_shapes=[pltpu.SemaphoreType.DMA((pieces,)), pltpu.SemaphoreType.DMA((pieces,))],
    )(mine)


def _pairshare(name, half):
    got = _pairswap(name, half)
    south = lax.axis_index("c") == 0
    return jnp.concatenate([jnp.where(south, half, got), jnp.where(south, got, half)], axis=0)


def _sum_blocks(name, parts, out_dtype):
    k, m, n = parts.shape
    tr = _tile(m, 512, 16)

    def body(p_ref, o_ref):
        acc = p_ref[0].astype(F32)
        for s in range(1, k):
            acc = acc + p_ref[s].astype(F32)
        o_ref[...] = acc.astype(o_ref.dtype)

    return pl.pallas_call(
        body, name=name, grid=(m // tr,), in_specs=[pl.BlockSpec((k, tr, n), lambda i: (0, i, 0))],
        out_specs=pl.BlockSpec((tr, n), lambda i: (i, 0)), out_shape=jax.ShapeDtypeStruct((m, n), out_dtype),
        compiler_params=_params("parallel"),
    )(parts)


def _sum_pair(name, a, b, out_dtype):
    m, n = a.shape
    tr = _tile(m, 512, 16)

    def body(a_ref, b_ref, o_ref):
        o_ref[...] = (a_ref[...].astype(F32) + b_ref[...].astype(F32)).astype(o_ref.dtype)

    spec = pl.BlockSpec((tr, n), lambda i: (i, 0))
    return pl.pallas_call(
        body, name=name, grid=(m // tr,), in_specs=[spec, spec], out_specs=spec,
        out_shape=jax.ShapeDtypeStruct((m, n), out_dtype), compiler_params=_params("parallel"),
    )(a, b)


def _adamw(name, w, g, m, v):
    shape = w.shape
    cols = shape[-1]
    rows = math.prod(shape[:-1])
    tr = _tile(rows, 512, 8) if rows * cols > 2 ** 19 else rows

    def body(w_ref, g_ref, m_ref, v_ref, d_ref, mo_ref, vo_ref):
        gv = g_ref[...]
        m2 = ADAM_B1 * m_ref[...] + (1.0 - ADAM_B1) * gv
        v2 = ADAM_B2 * v_ref[...] + (1.0 - ADAM_B2) * (gv * gv)
        m_hat = m2 / (1.0 - ADAM_B1 ** ADAM_STEP)
        v_hat = v2 / (1.0 - ADAM_B2 ** ADAM_STEP)
        d_ref[...] = -ADAM_LR * (m_hat / (jnp.sqrt(v_hat) + ADAM_EPS) + ADAM_WD * w_ref[...])
        mo_ref[...] = m2
        vo_ref[...] = v2

    spec = pl.BlockSpec((tr, cols), lambda i: (i, 0))
    outs = pl.pallas_call(
        body, name=name, grid=(rows // tr,), in_specs=[spec] * 4, out_specs=[spec] * 3,
        out_shape=[jax.ShapeDtypeStruct((rows, cols), F32)] * 3, compiler_params=_params("parallel"),
    )(*(a.reshape(rows, cols) for a in (w, g, m, v)))
    return tuple(o.reshape(shape) for o in outs)


WEIGHTS = ["meta_tokens", "norm_g", "ffn_w1", "ffn_w3", "ffn_w2", "ssm_in_proj", "ssm_conv_w", "ssm_conv_b", "ssm_dt_bias",
           "ssm_a_log", "ssm_d", "ssm_norm_g", "ssm_out_proj", "kv_norm_g", "w_k", "k_norm_g", "w_v", "sb_w_q",
           "sb_q_norm_g", "sb_w_o"]
SHARD_AXIS = {"meta_tokens": 1, "norm_g": 2, "ffn_w1": 3, "ffn_w3": 3, "ffn_w2": 2, "ssm_in_proj": 2, "ssm_conv_w": 2,
              "ssm_conv_b": 1, "ssm_norm_g": 1, "ssm_out_proj": 1, "w_k": 0, "w_v": 0, "sb_w_q": 1, "sb_w_o": 1}
MATRICES = ["ffn_w1", "ffn_w3", "ffn_w2", "ssm_in_proj", "ssm_out_proj", "w_k", "w_v", "sb_w_q", "sb_w_o"]
VECTORS = [n for n in WEIGHTS if n in SHARD_AXIS and n not in MATRICES]
REPLICATED = [n for n in WEIGHTS if n not in SHARD_AXIS]
LAYER_AXIS = ("ssm_", "sb_")
PACK_COLS = 1024
N_CHIPS = 4


def _pack(arrays, row_mult, dtype):
    segs = []
    for a in arrays:
        n = math.prod(a.shape)
        r = -(-n // PACK_COLS)
        flat = a.reshape(-1).astype(dtype)
        if r * PACK_COLS != n:
            flat = jnp.pad(flat, (0, r * PACK_COLS - n))
        segs.append(flat.reshape(r, PACK_COLS))
    rows = sum(s.shape[0] for s in segs)
    extra = -rows % row_mult
    if extra:
        segs.append(jnp.zeros((extra, PACK_COLS), dtype))
    return jnp.concatenate(segs, axis=0)


def _unpack(packed, shapes):
    lead = packed.shape[:-2]
    out, r0 = [], 0
    for shp in shapes:
        n = math.prod(shp)
        r = -(-n // PACK_COLS)
        seg = packed[..., r0:r0 + r, :]
        if r * PACK_COLS != n:
            seg = seg.reshape(*lead, r * PACK_COLS)[..., :n]
        out.append(seg.reshape(*lead, *shp))
        r0 += r
    return out


def _join(stack, axis):
    return jnp.concatenate([stack[s] for s in range(N_CHIPS)], axis=axis)


def _shards(full, axis):
    n = full.shape[axis] // N_CHIPS
    return [lax.slice_in_dim(full, s * n, (s + 1) * n, axis=axis) for s in range(N_CHIPS)]


def _drop_layer(name, a):
    return a[0] if name.startswith(LAYER_AXIS) else a


def kernel(x, meta_tokens, norm_g, ffn_w1, ffn_w3, ffn_w2, ssm_in_proj, ssm_conv_w, ssm_conv_b, ssm_dt_bias, ssm_a_log, ssm_d, ssm_norm_g, ssm_out_proj, kv_norm_g, w_k, k_norm_g, w_v, sb_w_q, sb_q_norm_g, sb_w_o, loss_target, m_meta_tokens, m_norm_g, m_ffn_w1, m_ffn_w3, m_ffn_w2, m_ssm_in_proj, m_ssm_conv_w, m_ssm_conv_b, m_ssm_dt_bias, m_ssm_a_log, m_ssm_d, m_ssm_norm_g, m_ssm_out_proj, m_kv_norm_g, m_w_k, m_k_norm_g, m_w_v, m_sb_w_q, m_sb_q_norm_g, m_sb_w_o, v_meta_tokens, v_norm_g, v_ffn_w1, v_ffn_w3, v_ffn_w2, v_ssm_in_proj, v_ssm_conv_w, v_ssm_conv_b, v_ssm_dt_bias, v_ssm_a_log, v_ssm_d, v_ssm_norm_g, v_ssm_out_proj, v_kv_norm_g, v_w_k, v_k_norm_g, v_w_v, v_sb_w_q, v_sb_q_norm_g, v_sb_w_o):
    args = locals()
    w_in = {n: args[n] for n in WEIGHTS}
    m_in = {n: args["m_" + n] for n in WEIGHTS}
    v_in = {n: args["v_" + n] for n in WEIGHTS}
    c = lax.axis_index("c")

    def gather(names, dtype, row_mult):
        packed = _pack([w_in[n] for n in names], 2 * row_mult, dtype)
        half = packed.shape[0] // 2
        mine = lax.dynamic_slice_in_dim(packed, c * half, half, axis=0)
        allp = _allgather8(f"gather_{jnp.dtype(dtype).name}", mine).reshape(N_CHIPS, 2 * half, PACK_COLS)
        stacks = _unpack(allp, [w_in[n].shape for n in names])
        return {n: _join(s, SHARD_AXIS[n]) for n, s in zip(names, stacks)}

    full = {**gather(MATRICES, BF16, 16 * GATHER_PIECES), **gather(VECTORS, F32, 8)}
    full.update({n: w_in[n] for n in REPLICATED})
    full = {n: _drop_layer(n, a) for n, a in full.items()}

    sq, grad_x, grads = _local_step(x[0], loss_target[0], full)
    loss = lax.psum(0.5 / x.shape[-1] * jnp.sum(sq), ("x", "y", "c"))
    grads = {n: (g[None] if n.startswith(LAYER_AXIS) else g) for n, g in grads.items()}

    sharded = MATRICES + VECTORS
    per_chip = [[] for _ in range(N_CHIPS)]
    for n in sharded:
        for p, s in enumerate(_shards(grads[n], SHARD_AXIS[n])):
            per_chip[p].append(s)
    contrib = jnp.stack([_pack(parts, 16 * PAIR_PIECES, BF16) for parts in per_chip])
    half_rows = contrib.shape[1] // 2
    halves = contrib.reshape(N_CHIPS, 2, half_rows, PACK_COLS)

    def take(core):
        return lax.dynamic_index_in_dim(halves, core, axis=1, keepdims=False).reshape(N_CHIPS * half_rows, PACK_COLS)

    theirs = _pairswap("grad_pair", take(1 - c))
    pair_sum = _sum_pair("grad_pair_sum", take(c), theirs, BF16).reshape(N_CHIPS, half_rows, PACK_COLS)
    got = _exchange4("grad_exchange", pair_sum)
    reduced = _pairshare("grad_share", _sum_blocks("grad_sum", got, F32))
    g_out = dict(zip(sharded, _unpack(reduced, [w_in[n].shape for n in sharded])))

    rep = _pack([grads[n] for n in REPLICATED], 8, F32)
    rep_sum = _sum_blocks("rep_sum", _allgather8("rep_gather", rep).reshape(8, rep.shape[0], PACK_COLS), F32)
    g_out.update(zip(REPLICATED, _unpack(rep_sum, [w_in[n].shape for n in REPLICATED])))

    delta, new_m, new_v = {}, {}, {}
    for n in WEIGHTS:
        delta[n], new_m[n], new_v[n] = _adamw(f"adamw_{n}", w_in[n], g_out[n], m_in[n], v_in[n])
    return (loss, grad_x[None], *[g_out[n] for n in WEIGHTS], *[delta[n] for n in WEIGHTS],
            *[new_m[n] for n in WEIGHTS], *[new_v[n] for n in WEIGHTS])
```
